```python
import math
import jax
import jax.numpy as jnp
from jax import lax
import numpy as np

D_MODEL = 1024
BATCH = 32
SEQ = 256
DEPTH = 1
DEC_BATCH = 4
DEC_SEQ = 2048
PAST_LEN = 512

GRID_W = 64
POS_BASE = 10000.0
CONV_K = 5
GDN_HEADS = 8
GDN_DK = 128
GDN_DV = 128
GDN_CHUNK = 64
SSD_HEADS = 32
SSD_P = 64
SSD_INNER = SSD_HEADS * SSD_P
SSD_GROUPS = 4
SSD_N = 128
SSD_CHUNK = 128
N_GROUPS = 4
EXPERTS_PER_GROUP = 8
N_EXPERTS = N_GROUPS * EXPERTS_PER_GROUP
TOP_K_IN_GROUP = 2
D_EXPERT = 256
DT_MIN = 0.001
DT_MAX = 0.1
EPS = 1e-6
ALPHA = (2.0 * DEPTH) ** 0.25
IN_SPLITS = (2 * GDN_HEADS * GDN_DK + GDN_HEADS * GDN_DV, GDN_HEADS * GDN_DV, 2 * GDN_HEADS, 2 * GDN_HEADS,
             SSD_INNER + 2 * SSD_GROUPS * SSD_N, SSD_INNER, 2 * SSD_HEADS, 2 * D_MODEL)
IN_COLS = sum(IN_SPLITS)

kernel_name = "bidir_gdn_ssd_hiermoe_prefix_dit"


def _split_cols(t, sizes):
    return jnp.split(t, np.cumsum(sizes)[:-1].tolist(), axis=-1)


def _layernorm(x, g=None, b=None):
    xf = x.astype(jnp.float32)
    mu = jnp.mean(xf, axis=-1, keepdims=True)
    var = jnp.mean(jnp.square(xf - mu), axis=-1, keepdims=True)
    y = (xf - mu) * lax.rsqrt(var + EPS)
    if g is not None:
        y = y * g.astype(jnp.float32) + b.astype(jnp.float32)
    return y.astype(x.dtype)


def _rms(x):
    return x * lax.rsqrt(jnp.mean(jnp.square(x), axis=-1, keepdims=True) + EPS)


def _l2norm(x):
    return x * lax.rsqrt(jnp.sum(jnp.square(x), axis=-1, keepdims=True) + EPS)


def _flip(t):
    return jnp.flip(t, axis=1)


def _grid_pos_embed(n_tokens, d):
    rows = n_tokens // GRID_W
    rr, cc = jnp.meshgrid(jnp.arange(rows, dtype=jnp.float32), jnp.arange(GRID_W, dtype=jnp.float32), indexing="ij")
    quarter = d // 4
    freqs = jnp.exp(-math.log(POS_BASE) * jnp.arange(quarter, dtype=jnp.float32) / quarter)
    ang_r = rr.reshape(-1, 1) * freqs
    ang_c = cc.reshape(-1, 1) * freqs
    return jnp.concatenate([jnp.sin(ang_r), jnp.cos(ang_r), jnp.sin(ang_c), jnp.cos(ang_c)], axis=-1)


def _dwconv_centred(u, w, bias):
    k, ch = w.shape
    pad = k // 2
    out = lax.conv_general_dilated(u, w[:, None, :].astype(u.dtype), window_strides=(1,), padding=((pad, pad),),
                                   dimension_numbers=("NWC", "WIO", "NWC"), feature_group_count=ch)
    return out + bias.astype(u.dtype)


def _to_chunks(t, size):
    b, l, h = t.shape[:3]
    t = t.reshape(b, l // size, size, h, *t.shape[3:])
    return jnp.moveaxis(t, 3, 1)


def _gdn_chunked(q, k, v, beta, g, s0):
    b, l, h, _ = q.shape
    q, k, v, beta, g = (_to_chunks(t, GDN_CHUNK) for t in (q, k, v, beta, g))
    gc = jnp.cumsum(g, axis=-1)
    incl = jnp.tril(jnp.ones((GDN_CHUNK, GDN_CHUNK), dtype=bool))
    strict = jnp.tril(jnp.ones((GDN_CHUNK, GDN_CHUNK), dtype=bool), k=-1)
    decay = jnp.exp(jnp.where(incl, gc[..., :, None] - gc[..., None, :], -jnp.inf))
    kb = k * beta[..., None]
    lmat = jnp.where(strict, jnp.einsum("bhnik,bhnjk->bhnij", kb, k) * decay, 0.0)
    eye = jnp.eye(GDN_CHUNK, dtype=q.dtype)
    tinv = lax.linalg.triangular_solve(eye + lmat, jnp.broadcast_to(eye, lmat.shape), left_side=True,
                                       lower=True, unit_diagonal=True)
    u = tinv @ (v * beta[..., None])
    w = tinv @ (kb * jnp.exp(gc)[..., None])
    attn = jnp.einsum("bhnik,bhnjk->bhnij", q, k) * decay
    qg = q * jnp.exp(gc)[..., None]
    kdec = k * jnp.exp(gc[..., -1:] - gc)[..., None]
    glast = jnp.exp(gc[..., -1])

    def step(S, xs):
        qg_c, kdec_c, u_c, w_c, attn_c, glast_c = xs
        v_new = u_c - jnp.einsum("bhck,bhkv->bhcv", w_c, S)
        o = jnp.einsum("bhck,bhkv->bhcv", qg_c, S) + jnp.einsum("bhij,bhjv->bhiv", attn_c, v_new)
        S = S * glast_c[..., None, None] + jnp.einsum("bhck,bhcv->bhkv", kdec_c, v_new)
        return S, o

    xs = tuple(jnp.moveaxis(t, 2, 0) for t in (qg, kdec, u, w, attn, glast))
    s_fin, o = lax.scan(step, s0, xs)
    o = jnp.moveaxis(o, 0, 2).reshape(b, h, l, -1)
    return jnp.swapaxes(o, 1, 2), s_fin


def _ssd_chunked(x, dt, a_neg, bm, cm, s0):
    b, l, h, p = x.shape
    g, s = bm.shape[2], bm.shape[3]
    r = h // g
    n = l // SSD_CHUNK
    c = SSD_CHUNK
    xr = x.reshape(b, n, c, g, r, p)
    dtr = dt.reshape(b, n, c, g, r)
    acum = jnp.cumsum((dt * a_neg).reshape(b, n, c, g, r), axis=2)
    bmr = bm.reshape(b, n, c, g, s)
    cmr = cm.reshape(b, n, c, g, s)
    incl = jnp.tril(jnp.ones((c, c), dtype=bool))
    seg = acum[:, :, :, None] - acum[:, :, None, :]
    decay = jnp.exp(jnp.where(incl[:, :, None, None], seg, -jnp.inf))
    cb = jnp.einsum("bnigs,bnjgs->bnijg", cmr, bmr)
    y_diag = jnp.einsum("bnijg,bnijgr,bnjgrp->bnigrp", cb, decay, dtr[..., None] * xr)
    x_end = xr * (dtr * jnp.exp(acum[:, :, -1:] - acum))[..., None]
    states = jnp.einsum("bncgs,bncgrp->bngrps", bmr, x_end)
    chunk_decay = jnp.exp(acum[:, :, -1])

    def step(hst, xs):
        st, cd = xs
        return hst * cd[..., None, None] + st, hst

    h_fin, h_prev = lax.scan(step, s0.reshape(b, g, r, p, s),
                             (jnp.moveaxis(states, 1, 0), jnp.moveaxis(chunk_decay, 1, 0)))
    h_prev = jnp.moveaxis(h_prev, 0, 1)
    y_off = jnp.einsum("bncgs,bngrps->bncgrp", cmr, h_prev) * jnp.exp(acum)[..., None]
    y = (y_diag + y_off).reshape(b, l, h, p)
    return y, h_fin.reshape(b, h, p, s)


def _token_mixer(h, s_gdn0, s_ssd0, lp):
    f32 = jnp.float32
    b, l, _ = h.shape
    proj = h @ lp["w_in"]
    qkv, z_g, beta_raw, a_raw, xbc, z_s, dt_raw, gate_raw = _split_cols(proj, IN_SPLITS)
    qkv = jax.nn.silu(_dwconv_centred(qkv, lp["gdn_conv_w"], lp["gdn_conv_b"])).astype(f32)
    q, k, v = _split_cols(qkv, (GDN_HEADS * GDN_DK, GDN_HEADS * GDN_DK, GDN_HEADS * GDN_DV))
    q = _l2norm(q.reshape(b, l, GDN_HEADS, GDN_DK)) * (GDN_DK ** -0.5)
    k = _l2norm(k.reshape(b, l, GDN_HEADS, GDN_DK))
    v = v.reshape(b, l, GDN_HEADS, GDN_DV)
    beta = jax.nn.sigmoid(beta_raw.astype(f32)).reshape(b, l, 2, GDN_HEADS)
    gdec = -jnp.exp(lp["gdn_a_log"].astype(f32)) * jax.nn.softplus(
        a_raw.astype(f32).reshape(b, l, 2, GDN_HEADS) + lp["gdn_dt_bias"].astype(f32))
    s0g = s_gdn0.astype(f32)
    o_f, sg_f = _gdn_chunked(q, k, v, beta[:, :, 0], gdec[:, :, 0], s0g[:, 0])
    o_b, sg_b = _gdn_chunked(_flip(q), _flip(k), _flip(v), _flip(beta[:, :, 1]), _flip(gdec[:, :, 1]), s0g[:, 1])
    o = _rms(o_f + _flip(o_b)) * lp["gdn_norm_w"].astype(f32) * jax.nn.silu(
        z_g.astype(f32).reshape(b, l, GDN_HEADS, GDN_DV))
    u_g = o.reshape(b, l, GDN_HEADS * GDN_DV).astype(h.dtype) @ lp["w_gdn_out"]
    xbc = jax.nn.silu(_dwconv_centred(xbc, lp["ssd_conv_w"], lp["ssd_conv_b"])).astype(f32)
    xs, bm, cm = _split_cols(xbc, (SSD_INNER, SSD_GROUPS * SSD_N, SSD_GROUPS * SSD_N))
    xs = xs.reshape(b, l, SSD_HEADS, SSD_P)
    bm = bm.reshape(b, l, SSD_GROUPS, SSD_N)
    cm = cm.reshape(b, l, SSD_GROUPS, SSD_N)
    dt = jax.nn.softplus(dt_raw.astype(f32).reshape(b, l, 2, SSD_HEADS) + lp["ssd_dt_bias"].astype(f32))
    a_neg = -jnp.exp(lp["ssd_a_log"].astype(f32))
    s0s = s_ssd0.astype(f32)
    y_f, ss_f = _ssd_chunked(xs, dt[:, :, 0], a_neg[0], bm, cm, s0s[:, 0])
    y_b, ss_b = _ssd_chunked(_flip(xs), _flip(dt[:, :, 1]), a_neg[1], _flip(bm), _flip(cm), s0s[:, 1])
    y = y_f + _flip(y_b) + lp["ssd_d"].astype(f32)[:, None] * xs
    y = y.reshape(b, l, SSD_INNER) * jax.nn.silu(z_s.astype(f32))
    y = _rms(y.reshape(b, l, SSD_GROUPS, SSD_INNER // SSD_GROUPS)).reshape(b, l, SSD_INNER)
    y = y * lp["ssd_norm_w"].astype(f32)
    u_s = y.astype(h.dtype) @ lp["w_ssd_out"]
    gate_g, gate_s = jnp.split(gate_raw, 2, axis=-1)
    m = jax.nn.sigmoid(gate_g) * u_g + jax.nn.sigmoid(gate_s) * u_s
    out = m @ lp["w_o"]
    return out, jnp.stack([sg_f, sg_b], axis=1), jnp.stack([ss_f, ss_b], axis=1)


def _hier_moe(h, lp):
    f32 = jnp.float32
    hf = h.astype(f32)
    p_grp = jax.nn.softmax(hf @ lp["w_router_group"].astype(f32) + lp["b_router_group"].astype(f32), axis=-1)
    p_top, g_idx = lax.top_k(p_grp, 1)
    le = (hf @ lp["w_router_expert"].astype(f32) + lp["b_router_expert"].astype(f32)).reshape(
        *h.shape[:-1], N_GROUPS, EXPERTS_PER_GROUP)
    le_sel = jnp.einsum("blge,blg->ble", le, jax.nn.one_hot(g_idx[..., 0], N_GROUPS, dtype=f32))
    p_e = jax.nn.softmax(le_sel, axis=-1)
    w_top, e_idx = lax.top_k(p_e, TOP_K_IN_GROUP)
    w_top = w_top / jnp.sum(w_top, axis=-1, keepdims=True) * p_top
    expert_id = g_idx * EXPERTS_PER_GROUP + e_idx
    gates = jnp.sum(jax.nn.one_hot(expert_id, N_EXPERTS, dtype=f32) * w_top[..., None], axis=-2)
    hidden = jax.nn.silu(jnp.einsum("bld,edf->blef", h, lp["w_exp_gate"])) * jnp.einsum(
        "bld,edf->blef", h, lp["w_exp_up"])
    hidden = hidden * gates.astype(h.dtype)[..., None]
    return jnp.einsum("blef,efd->bld", hidden, lp["w_exp_down"])


def _adaln(cvec, w, b):
    return (jax.nn.silu(cvec.astype(jnp.float32)) @ w.astype(jnp.float32) + b.astype(jnp.float32))[:, None, :]


def _layer(x, ada, s_gdn0, s_ssd0, lp):
    shift1, scale1, gate1, shift2, scale2, gate2 = jnp.split(ada.astype(x.dtype), 6, axis=-1)
    h = _layernorm(x) * (1 + scale1) + shift1
    mix, s_gdn, s_ssd = _token_mixer(h, s_gdn0, s_ssd0, lp)
    x = _layernorm(ALPHA * x + gate1 * mix, lp["ln1_g"], lp["ln1_b"])
    h = _layernorm(x) * (1 + scale2) + shift2
    x = _layernorm(ALPHA * x + gate2 * _hier_moe(h, lp), lp["ln2_g"], lp["ln2_b"])
    return x, s_gdn, s_ssd


def setup_inputs(seed: int = 0) -> dict:
    key = jax.random.key(seed)
    ks = iter(jax.random.split(key, 48))
    f32 = jnp.float32
    beta_dn = (8.0 * DEPTH) ** -0.25

    def nrm(shape, scale):
        return jax.random.normal(next(ks), shape, f32) * scale

    def dt_bias(shape):
        u = jax.random.uniform(next(ks), shape, f32)
        dt = jnp.exp(u * (math.log(DT_MAX) - math.log(DT_MIN)) + math.log(DT_MIN))
        return dt + jnp.log(-jnp.expm1(-dt))

    def a_log(shape):
        return jnp.log(jax.random.uniform(next(ks), shape, f32, 1.0, 16.0))

    conv_ch_g = 2 * GDN_HEADS * GDN_DK + GDN_HEADS * GDN_DV
    conv_ch_s = SSD_INNER + 2 * SSD_GROUPS * SSD_N
    return {
        "x_prompt": nrm((BATCH, SEQ, D_MODEL), 1.0),
        "x_sample": nrm((DEC_BATCH, DEC_SEQ, D_MODEL), 1.0),
        "state_gdn": nrm((DEC_BATCH, DEPTH, 2, GDN_HEADS, GDN_DK, GDN_DV), 0.1),
        "state_ssd": nrm((DEC_BATCH, DEPTH, 2, SSD_HEADS, SSD_P, SSD_N), 0.1),
        "c": nrm((DEC_BATCH, D_MODEL), 1.0),
        "c_ctx": nrm((D_MODEL,), 1.0),
        "w_ada": nrm((DEPTH, D_MODEL, 6 * D_MODEL), 0.5 * D_MODEL ** -0.5),
        "b_ada": nrm((DEPTH, 6 * D_MODEL), 0.02),
        "w_in": nrm((DEPTH, D_MODEL, IN_COLS), D_MODEL ** -0.5),
        "gdn_conv_w": nrm((DEPTH, CONV_K, conv_ch_g), CONV_K ** -0.5),
        "gdn_conv_b": nrm((DEPTH, conv_ch_g), 0.02),
        "gdn_a_log": a_log((DEPTH, 2, GDN_HEADS)),
        "gdn_dt_bias": dt_bias((DEPTH, 2, GDN_HEADS)),
        "gdn_norm_w": 1.0 + nrm((DEPTH, GDN_DV), 0.02),
        "w_gdn_out": nrm((DEPTH, GDN_HEADS * GDN_DV, D_MODEL), beta_dn * (GDN_HEADS * GDN_DV) ** -0.5),
        "ssd_conv_w": nrm((DEPTH, CONV_K, conv_ch_s), CONV_K ** -0.5),
        "ssd_conv_b": nrm((DEPTH, conv_ch_s), 0.02),
        "ssd_a_log": a_log((DEPTH, 2, SSD_HEADS)),
        "ssd_dt_bias": dt_bias((DEPTH, 2, SSD_HEADS)),
        "ssd_d": 1.0 + nrm((DEPTH, SSD_HEADS), 0.02),
        "ssd_norm_w": 1.0 + nrm((DEPTH, SSD_INNER), 0.02),
        "w_ssd_out": nrm((DEPTH, SSD_INNER, D_MODEL), beta_dn * SSD_INNER ** -0.5),
        "w_o": nrm((DEPTH, D_MODEL, D_MODEL), beta_dn * D_MODEL ** -0.5),
        "ln1_g": 1.0 + nrm((DEPTH, D_MODEL), 0.02),
        "ln1_b": nrm((DEPTH, D_MODEL), 0.02),
        "w_router_group": nrm((DEPTH, D_MODEL, N_GROUPS), D_MODEL ** -0.5),
        "b_router_group": nrm((DEPTH, N_GROUPS), 0.01),
        "w_router_expert": nrm((DEPTH, D_MODEL, N_EXPERTS), D_MODEL ** -0.5),
        "b_router_expert": nrm((DEPTH, N_EXPERTS), 0.01),
        "w_exp_gate": nrm((DEPTH, N_EXPERTS, D_MODEL, D_EXPERT), D_MODEL ** -0.5),
        "w_exp_up": nrm((DEPTH, N_EXPERTS, D_MODEL, D_EXPERT), D_MODEL ** -0.5),
        "w_exp_down": nrm((DEPTH, N_EXPERTS, D_EXPERT, D_MODEL), beta_dn * D_EXPERT ** -0.5),
        "ln2_g": 1.0 + nrm((DEPTH, D_MODEL), 0.02),
        "ln2_b": nrm((DEPTH, D_MODEL), 0.02),
    }


def reference(x_prompt, x_sample, state_gdn, state_ssd, c, c_ctx, w_ada, b_ada, w_in, gdn_conv_w, gdn_conv_b,
              gdn_a_log, gdn_dt_bias, gdn_norm_w, w_gdn_out, ssd_conv_w, ssd_conv_b, ssd_a_log, ssd_dt_bias,
              ssd_d, ssd_norm_w, w_ssd_out, w_o, ln1_g, ln1_b, w_router_group, b_router_group,
              w_router_expert, b_router_expert, w_exp_gate, w_exp_up, w_exp_down, ln2_g, ln2_b):
    f32 = jnp.float32
    xc = x_prompt
    pos = _grid_pos_embed(x_sample.shape[1], D_MODEL)
    xl = x_sample + pos.astype(x_sample.dtype)[None]
    zeros_gdn = jnp.zeros((xc.shape[0], 2, GDN_HEADS, GDN_DK, GDN_DV), f32)
    zeros_ssd = jnp.zeros((xc.shape[0], 2, SSD_HEADS, SSD_P, SSD_N), f32)
    new_gdn, new_ssd = [], []
    for l in range(DEPTH):
        lp = {
            "w_in": w_in[l], "gdn_conv_w": gdn_conv_w[l], "gdn_conv_b": gdn_conv_b[l],
            "gdn_a_log": gdn_a_log[l], "gdn_dt_bias": gdn_dt_bias[l], "gdn_norm_w": gdn_norm_w[l],
            "w_gdn_out": w_gdn_out[l], "ssd_conv_w": ssd_conv_w[l], "ssd_conv_b": ssd_conv_b[l],
            "ssd_a_log": ssd_a_log[l], "ssd_dt_bias": ssd_dt_bias[l], "ssd_d": ssd_d[l],
            "ssd_norm_w": ssd_norm_w[l], "w_ssd_out": w_ssd_out[l], "w_o": w_o[l],
            "ln1_g": ln1_g[l], "ln1_b": ln1_b[l], "w_router_group": w_router_group[l],
            "b_router_group": b_router_group[l], "w_router_expert": w_router_expert[l],
            "b_router_expert": b_router_expert[l], "w_exp_gate": w_exp_gate[l], "w_exp_up": w_exp_up[l],
            "w_exp_down": w_exp_down[l], "ln2_g": ln2_g[l], "ln2_b": ln2_b[l],
        }
        ada_ctx = _adaln(c_ctx[None], w_ada[l], b_ada[l])
        ada_lat = _adaln(c, w_ada[l], b_ada[l])
        xc, s_g, s_s = _layer(xc, ada_ctx, zeros_gdn, zeros_ssd, lp)
        xl, _, _ = _layer(xl, ada_lat, state_gdn[:, l], state_ssd[:, l], lp)
        new_gdn.append(s_g)
        new_ssd.append(s_s)
    new_state_gdn = jnp.stack(new_gdn, axis=1).astype(x_prompt.dtype)
    new_state_ssd = jnp.stack(new_ssd, axis=1).astype(x_prompt.dtype)
    return (xc, xl, new_state_gdn, new_state_ssd)
```

```python
import functools
import math

import jax
import jax.numpy as jnp
from jax import lax
from jax.experimental import pallas as pl
from jax.experimental.pallas import tpu as pltpu

F32 = jnp.float32
BF16 = jnp.bfloat16
HIGHEST = lax.Precision.HIGHEST

D_MODEL = 1024
GRID_W = 64
POS_BASE = 10000.0
CONV_K = 5
GDN_HEADS = 8
GDN_DK = 128
GDN_DV = 128
SSD_HEADS = 32
SSD_P = 64
SSD_INNER = SSD_HEADS * SSD_P
SSD_GROUPS = 4
SSD_N = 128
HEADS_PER_GROUP = SSD_HEADS // SSD_GROUPS
GROUP_W = HEADS_PER_GROUP * SSD_P
N_GROUPS = 4
EXPERTS_PER_GROUP = 8
N_EXPERTS = N_GROUPS * EXPERTS_PER_GROUP
D_EXPERT = 256
EPS = 1e-6
DEPTH = 1
ALPHA = (2.0 * DEPTH) ** 0.25

LANES = 128
SUBLANES = 8
CHUNK = 128
CONV_PAD = SUBLANES
VMEM_LIMIT = 56 * 1024 * 1024

COL_QKV = 0
COL_ZG = 3072
COL_XBC = 4096
COL_ZS = 7168
COL_GATE = 9216
PROJ_COLS = 11264
PROJ_TN = 1024
GDN_ROW_BETA = 0
GDN_ROW_A = 2
ROW_DT = GDN_HEADS * SUBLANES
SMALL_ROWS = 128

TOK_BLOCK = 512
POST_BLOCK = 256
MOE_BLOCK = 1024


def _bdot(a, b):
    return jnp.dot(a.astype(BF16), b.astype(BF16), preferred_element_type=F32)


def _silu(x):
    return x * jax.nn.sigmoid(x)


def _softplus(x):
    return jnp.maximum(x, 0.0) + jnp.log1p(jnp.exp(-jnp.abs(x)))


def _ln(x):
    mu = jnp.mean(x, axis=-1, keepdims=True)
    xc = x - mu
    var = jnp.mean(xc * xc, axis=-1, keepdims=True)
    return xc * lax.rsqrt(var + EPS)


def _lane_scan(x, reverse):
    lane = lax.broadcasted_iota(jnp.int32, x.shape, 1)
    s = 1
    while s < CHUNK:
        if reverse:
            shifted = pltpu.roll(x, CHUNK - s, axis=1)
            x = x + jnp.where(lane < CHUNK - s, shifted, 0.0)
        else:
            shifted = pltpu.roll(x, s, axis=1)
            x = x + jnp.where(lane >= s, shifted, 0.0)
        s *= 2
    return x


def _tri_masks(reverse):
    row = lax.broadcasted_iota(jnp.int32, (CHUNK, CHUNK), 0)
    col = lax.broadcasted_iota(jnp.int32, (CHUNK, CHUNK), 1)
    if reverse:
        return row <= col, row < col
    return row >= col, row > col


def _decay_matrix(acc_row, incl):
    acc_rb = jnp.broadcast_to(acc_row, (CHUNK, CHUNK))
    acc_col = acc_rb.T
    decay = jnp.exp(jnp.where(incl, acc_col - acc_rb, -jnp.inf))
    return decay, acc_col


def _conv_tap_sum(pad_ref, t0, lo, hi, w_ref, b_ref):
    acc = jnp.broadcast_to(b_ref[...], (CHUNK, hi - lo))
    for j in range(CONV_K):
        start = t0 + CONV_PAD - CONV_K // 2 + j
        acc = acc + pad_ref[start:start + CHUNK, lo:hi] * w_ref[j:j + 1, :]
    return acc


def _ada_kernel(c_ref, w_ref, b_ref, o_ref):
    s = _silu(c_ref[...])
    o_ref[...] = jnp.dot(s, w_ref[...], precision=HIGHEST, preferred_element_type=F32) + b_ref[...]


def _ada(cvec, w_ada, b_ada):
    n_out = w_ada.shape[1]
    tn = 1024
    return pl.pallas_call(
        _ada_kernel,
        grid=(n_out // tn,),
        in_specs=[
            pl.BlockSpec((SUBLANES, D_MODEL), lambda j: (0, 0)),
            pl.BlockSpec((D_MODEL, tn), lambda j: (0, j)),
            pl.BlockSpec((1, tn), lambda j: (0, j)),
        ],
        out_specs=pl.BlockSpec((SUBLANES, tn), lambda j: (0, j)),
        out_shape=jax.ShapeDtypeStruct((SUBLANES, n_out), F32),
        compiler_params=pltpu.CompilerParams(dimension_semantics=("arbitrary",), vmem_limit_bytes=VMEM_LIMIT),
        name="ada",
    )(cvec, w_ada, b_ada.reshape(1, n_out))


def _inproj_kernel(*refs, has_pos):
    if has_pos:
        x_ref, pos_ref, ada_ref, w_ref, wst_ref, proj_ref, small_ref, h_ref = refs
    else:
        x_ref, ada_ref, w_ref, wst_ref, proj_ref, small_ref, h_ref = refs
        pos_ref = None

    @pl.when(pl.program_id(1) == 0)
    def _():
        x = x_ref[...]
        if pos_ref is not None:
            x = x + pos_ref[...]
        shift = ada_ref[:, 0:D_MODEL]
        scale = ada_ref[:, D_MODEL:2 * D_MODEL]
        h = (_ln(x) * (1.0 + scale) + shift).astype(BF16)
        h_ref[...] = h
        small_ref[...] = lax.dot_general(wst_ref[...], h, (((1,), (1,)), ((), ())), preferred_element_type=F32)

    proj_ref[...] = jnp.dot(h_ref[...], w_ref[...], preferred_element_type=F32)


def _in_proj(x2d, pos, ada3, w_big, w_small_t, seq_len, ada_row0):
    n_tok = x2d.shape[0]
    t = TOK_BLOCK
    blocks_per_seq = seq_len // t
    has_pos = pos is not None

    def ada_map(i, j):
        return (ada_row0 + (i // blocks_per_seq if has_pos else 0), 0, 0)

    in_specs = [pl.BlockSpec((t, D_MODEL), lambda i, j: (i, 0))]
    args = [x2d]
    if has_pos:
        in_specs.append(pl.BlockSpec((t, D_MODEL), lambda i, j: (i % blocks_per_seq, 0)))
        args.append(pos)
    in_specs += [
        pl.BlockSpec((None, 1, 6 * D_MODEL), ada_map),
        pl.BlockSpec((D_MODEL, PROJ_TN), lambda i, j: (0, j)),
        pl.BlockSpec((SMALL_ROWS, D_MODEL), lambda i, j: (0, 0)),
    ]
    args += [ada3, w_big, w_small_t]
    return pl.pallas_call(
        functools.partial(_inproj_kernel, has_pos=has_pos),
        grid=(n_tok // t, PROJ_COLS // PROJ_TN),
        in_specs=in_specs,
        out_specs=[
            pl.BlockSpec((t, PROJ_TN), lambda i, j: (i, j)),
            pl.BlockSpec((SMALL_ROWS, t), lambda i, j: (0, i)),
        ],
        out_shape=[
            jax.ShapeDtypeStruct((n_tok, PROJ_COLS), F32),
            jax.ShapeDtypeStruct((SMALL_ROWS, n_tok), F32),
        ],
        scratch_shapes=[pltpu.VMEM((t, D_MODEL), BF16)],
        compiler_params=pltpu.CompilerParams(dimension_semantics=("arbitrary", "arbitrary"),
                                             vmem_limit_bytes=VMEM_LIMIT),
        name="in_proj",
    )(*args)


INV_BASE = 8


def _hdot(a, b):
    return jnp.dot(a, b, precision=HIGHEST, preferred_element_type=F32)


def _unit_tri_inverse(nmat):
    row = lax.broadcasted_iota(jnp.int32, (CHUNK, CHUNK), 0)
    col = lax.broadcasted_iota(jnp.int32, (CHUNK, CHUNK), 1)

    def same_block(size):
        shift = int(math.log2(size))
        return (row >> shift) == (col >> shift)

    nd = jnp.where(same_block(INV_BASE), nmat, 0.0)
    xinv = (row == col).astype(F32) - nd
    pw = _hdot(nd, nd)
    size = 2
    while True:
        xinv = xinv + _hdot(xinv, pw)
        size *= 2
        if size >= INV_BASE:
            break
        pw = _hdot(pw, pw)
    size = INV_BASE
    while size < CHUNK:
        off = jnp.where(same_block(2 * size), jnp.where(same_block(size), 0.0, nmat), 0.0)
        xinv = xinv - _hdot(xinv, _hdot(off, xinv))
        size *= 2
    return xinv


def _gdn_unit(c, d, qn_ref, kn_ref, vn_ref, kt_ref, sc_ref, s_ref, o_ref):
    reverse = d == 1
    incl, strict = _tri_masks(reverse)
    q = qn_ref[c]
    k = kn_ref[c]
    v = vn_ref[c]
    kt = kt_ref[c]
    beta = sc_ref[c, d:d + 1, :]
    gc = sc_ref[c, 2 + d:3 + d, :]
    decay, gcol = _decay_matrix(gc, incl)
    beta_rb = jnp.broadcast_to(beta, (CHUNK, CHUNK))
    kk = _bdot(k, kt)
    qk = _bdot(q, kt)
    nmat = jnp.where(strict, kk * decay, 0.0) * beta_rb
    xinv = _unit_tri_inverse(nmat)
    glast = gc[:, 0:1] if reverse else gc[:, CHUNK - 1:CHUNK]
    eg_rb = jnp.broadcast_to(jnp.exp(gc), (CHUNK, CHUNK))
    ut = _bdot(xinv, v)
    wt = _bdot(xinv * eg_rb, k)
    state = s_ref[d]
    both = _bdot(jnp.concatenate([wt, q], axis=0), state)
    vt = ut - both[:CHUNK]
    o = jnp.exp(gcol) * both[CHUNK:] + _bdot(qk * decay * beta_rb, vt)
    kd = kt * jnp.broadcast_to(beta * jnp.exp(glast - gc), (CHUNK, CHUNK))
    s_ref[d] = state * jnp.exp(glast) + _bdot(kd, vt)
    o_ref[c] = o


def _gdn_kernel(*refs, n_chunks, has_s0, want_state):
    it = iter(refs)
    q_ref, k_ref, v_ref, z_ref, sm_ref = (next(it) for _ in range(5))
    wq_ref, wk_ref, wv_ref, bq_ref, bk_ref, bv_ref = (next(it) for _ in range(6))
    par_ref, nw_ref = (next(it) for _ in range(2))
    s0_ref = next(it) if has_s0 else None
    og_ref = next(it)
    sout_ref = next(it) if want_state else None
    pad_ref, qn_ref, kn_ref, vn_ref, kt_ref, of_ref, ob_ref, sc_ref, s_ref = (next(it) for _ in range(9))

    seq = n_chunks * CHUNK
    zero_rows = jnp.zeros((CONV_PAD, 3 * LANES), F32)
    pad_ref[0:CONV_PAD, :] = zero_rows
    pad_ref[CONV_PAD + seq:2 * CONV_PAD + seq, :] = zero_rows
    for c in range(n_chunks):
        t0 = c * CHUNK
        pad_ref[CONV_PAD + t0:CONV_PAD + t0 + CHUNK, 0:LANES] = q_ref[t0:t0 + CHUNK, :]
        pad_ref[CONV_PAD + t0:CONV_PAD + t0 + CHUNK, LANES:2 * LANES] = k_ref[t0:t0 + CHUNK, :]
        pad_ref[CONV_PAD + t0:CONV_PAD + t0 + CHUNK, 2 * LANES:3 * LANES] = v_ref[t0:t0 + CHUNK, :]

    a_neg = [-jnp.exp(par_ref[d:d + 1, :]) for d in range(2)]
    dt_bias = [par_ref[2 + d:3 + d, :] for d in range(2)]
    for c in range(n_chunks):
        t0 = c * CHUNK
        qa = _silu(_conv_tap_sum(pad_ref, t0, 0, LANES, wq_ref, bq_ref))
        ka = _silu(_conv_tap_sum(pad_ref, t0, LANES, 2 * LANES, wk_ref, bk_ref))
        va = _silu(_conv_tap_sum(pad_ref, t0, 2 * LANES, 3 * LANES, wv_ref, bv_ref))
        qn = qa * lax.rsqrt(jnp.sum(qa * qa, axis=-1, keepdims=True) + EPS) * (GDN_DK ** -0.5)
        kn = ka * lax.rsqrt(jnp.sum(ka * ka, axis=-1, keepdims=True) + EPS)
        qn_ref[c] = qn
        kn_ref[c] = kn
        vn_ref[c] = va
        kt_ref[c] = kn.T
        for d in range(2):
            braw = sm_ref[GDN_ROW_BETA + d:GDN_ROW_BETA + d + 1, t0:t0 + CHUNK]
            araw = sm_ref[GDN_ROW_A + d:GDN_ROW_A + d + 1, t0:t0 + CHUNK]
            g = a_neg[d] * _softplus(araw + dt_bias[d])
            sc_ref[c, d:d + 1, :] = jax.nn.sigmoid(braw)
            sc_ref[c, 2 + d:3 + d, :] = _lane_scan(g, reverse=d == 1)

    if has_s0:
        s_ref[...] = s0_ref[...]
    else:
        s_ref[...] = jnp.zeros(s_ref.shape, F32)

    def body(i, carry):
        _gdn_unit(i, 0, qn_ref, kn_ref, vn_ref, kt_ref, sc_ref, s_ref, of_ref)
        _gdn_unit(n_chunks - 1 - i, 1, qn_ref, kn_ref, vn_ref, kt_ref, sc_ref, s_ref, ob_ref)
        return carry

    lax.fori_loop(0, n_chunks, body, 0)

    if want_state:
        sout_ref[...] = s_ref[...]
    for c in range(n_chunks):
        t0 = c * CHUNK
        o = of_ref[c] + ob_ref[c]
        o = o * lax.rsqrt(jnp.mean(o * o, axis=-1, keepdims=True) + EPS)
        og_ref[t0:t0 + CHUNK, :] = (o * nw_ref[...] * _silu(z_ref[t0:t0 + CHUNK, :])).astype(BF16)


def _gdn(proj, small_t, conv_w, conv_b, head_params, norm_w, s0, bsz, seq_len, want_state):
    n_chunks = seq_len // CHUNK
    n_tok = bsz * seq_len
    has_s0 = s0 is not None
    col = lambda off: (lambda b, h: (b, off + h))
    cw = lambda off: (lambda b, h: (0, off + h))
    in_specs = [
        pl.BlockSpec((seq_len, LANES), col(COL_QKV // LANES)),
        pl.BlockSpec((seq_len, LANES), col(COL_QKV // LANES + GDN_HEADS)),
        pl.BlockSpec((seq_len, LANES), col(COL_QKV // LANES + 2 * GDN_HEADS)),
        pl.BlockSpec((seq_len, LANES), col(COL_ZG // LANES)),
        pl.BlockSpec((SUBLANES, seq_len), lambda b, h: (h, b)),
        pl.BlockSpec((CONV_K, LANES), cw(0)),
        pl.BlockSpec((CONV_K, LANES), cw(GDN_HEADS)),
        pl.BlockSpec((CONV_K, LANES), cw(2 * GDN_HEADS)),
        pl.BlockSpec((1, LANES), cw(0)),
        pl.BlockSpec((1, LANES), cw(GDN_HEADS)),
        pl.BlockSpec((1, LANES), cw(2 * GDN_HEADS)),
        pl.BlockSpec((None, SUBLANES, LANES), lambda b, h: (h, 0, 0)),
        pl.BlockSpec((1, LANES), lambda b, h: (0, 0)),
    ]
    args = [proj, proj, proj, proj, small_t, conv_w, conv_w, conv_w, conv_b, conv_b, conv_b,
            head_params, norm_w]
    state_spec = pl.BlockSpec((None, None, 2, None, GDN_DK, GDN_DV), lambda b, h: (b, 0, 0, h, 0, 0))
    if has_s0:
        in_specs.append(state_spec)
        args.append(s0)
    out_specs = [pl.BlockSpec((seq_len, LANES), lambda b, h: (b, h))]
    out_shape = [jax.ShapeDtypeStruct((n_tok, GDN_HEADS * GDN_DV), BF16)]
    if want_state:
        out_specs.append(state_spec)
        out_shape.append(jax.ShapeDtypeStruct((bsz, 1, 2, GDN_HEADS, GDN_DK, GDN_DV), F32))
    chunked = pltpu.VMEM((n_chunks, CHUNK, LANES), F32)
    return pl.pallas_call(
        functools.partial(_gdn_kernel, n_chunks=n_chunks, has_s0=has_s0, want_state=want_state),
        grid=(bsz, GDN_HEADS),
        in_specs=in_specs,
        out_specs=out_specs,
        out_shape=out_shape,
        scratch_shapes=[
            pltpu.VMEM((seq_len + 2 * CONV_PAD, 3 * LANES), F32),
            chunked, chunked, chunked, chunked, chunked, chunked,
            pltpu.VMEM((n_chunks, SUBLANES, CHUNK), F32),
            pltpu.VMEM((2, GDN_DK, GDN_DV), F32),
        ],
        compiler_params=pltpu.CompilerParams(dimension_semantics=("arbitrary", "arbitrary"),
                                             vmem_limit_bytes=VMEM_LIMIT),
        name="gdn",
    )(*args)


def _ssd_unit(c, d, xs_ref, bs_ref, cs_ref, bt_ref, dt_ref, ac_ref, ht_ref, y_ref):
    reverse = d == 1
    incl, _ = _tri_masks(reverse)
    x = xs_ref[c]
    cm = cs_ref[c]
    bt = bt_ref[c]
    cb = _bdot(cm, bt)
    y_off = _bdot(cm, ht_ref[d])
    dts = dt_ref[c]
    acs = ac_ref[c]
    for r in range(HEADS_PER_GROUP):
        lo, hi = r * SSD_P, (r + 1) * SSD_P
        row = d * HEADS_PER_GROUP + r
        dt = dts[row:row + 1, :]
        ac = acs[row:row + 1, :]
        decay, acol = _decay_matrix(ac, incl)
        m = cb * decay * jnp.broadcast_to(dt, (CHUNK, CHUNK))
        xr = x[:, lo:hi]
        y = _bdot(m, xr) + jnp.exp(acol)[:, 0:SSD_P] * y_off[:, lo:hi]
        y_ref[c, :, lo:hi] = y_ref[c, :, lo:hi] + y
        last = ac[:, 0:1] if reverse else ac[:, CHUNK - 1:CHUNK]
        st = _bdot(bt * jnp.broadcast_to(dt * jnp.exp(last - ac), (SSD_N, CHUNK)), xr)
        ht_ref[d, :, lo:hi] = ht_ref[d, :, lo:hi] * jnp.exp(last) + st


def _ssd_kernel(*refs, n_chunks, has_h0, want_state):
    it = iter(refs)
    x_ref, b_ref, c_ref, z_ref, dtf_ref, dtb_ref = (next(it) for _ in range(6))
    wx_ref, wb_ref, wc_ref, bx_ref, bb_ref, bc_ref = (next(it) for _ in range(6))
    alog_ref, dtbias_ref, dvec_ref, nw_ref = (next(it) for _ in range(4))
    h0_ref = next(it) if has_h0 else None
    yg_ref = next(it)
    hout_ref = next(it) if want_state else None
    pad_ref, xs_ref, bs_ref, cs_ref, bt_ref, dt_ref, ac_ref, y_ref, ht_ref = (next(it) for _ in range(9))

    seq = n_chunks * CHUNK
    width = GROUP_W + 2 * SSD_N
    zero_rows = jnp.zeros((CONV_PAD, width), F32)
    pad_ref[0:CONV_PAD, :] = zero_rows
    pad_ref[CONV_PAD + seq:2 * CONV_PAD + seq, :] = zero_rows
    for c in range(n_chunks):
        t0 = c * CHUNK
        pad_ref[CONV_PAD + t0:CONV_PAD + t0 + CHUNK, 0:GROUP_W] = x_ref[t0:t0 + CHUNK, :]
        pad_ref[CONV_PAD + t0:CONV_PAD + t0 + CHUNK, GROUP_W:GROUP_W + SSD_N] = b_ref[t0:t0 + CHUNK, :]
        pad_ref[CONV_PAD + t0:CONV_PAD + t0 + CHUNK, GROUP_W + SSD_N:width] = c_ref[t0:t0 + CHUNK, :]

    raw_dt = (dtf_ref, dtb_ref)
    for c in range(n_chunks):
        t0 = c * CHUNK
        xa = _silu(_conv_tap_sum(pad_ref, t0, 0, GROUP_W, wx_ref, bx_ref))
        ba = _silu(_conv_tap_sum(pad_ref, t0, GROUP_W, GROUP_W + SSD_N, wb_ref, bb_ref))
        ca = _silu(_conv_tap_sum(pad_ref, t0, GROUP_W + SSD_N, width, wc_ref, bc_ref))
        xs_ref[c] = xa
        bs_ref[c] = ba
        cs_ref[c] = ca
        bt_ref[c] = ba.T
        y_ref[c] = xa * dvec_ref[...]
        for d in range(2):
            dt = _softplus(raw_dt[d][:, t0:t0 + CHUNK] + dtbias_ref[d])
            lo, hi = d * HEADS_PER_GROUP, (d + 1) * HEADS_PER_GROUP
            dt_ref[c, lo:hi, :] = dt
            ac_ref[c, lo:hi, :] = _lane_scan(dt * (-jnp.exp(alog_ref[d])), reverse=d == 1)

    if has_h0:
        ht_ref[...] = h0_ref[...]
    else:
        ht_ref[...] = jnp.zeros(ht_ref.shape, F32)

    def body(i, carry):
        _ssd_unit(i, 0, xs_ref, bs_ref, cs_ref, bt_ref, dt_ref, ac_ref, ht_ref, y_ref)
        _ssd_unit(n_chunks - 1 - i, 1, xs_ref, bs_ref, cs_ref, bt_ref, dt_ref, ac_ref, ht_ref, y_ref)
        return carry

    lax.fori_loop(0, n_chunks, body, 0)

    if want_state:
        hout_ref[...] = ht_ref[...]
    for c in range(n_chunks):
        t0 = c * CHUNK
        y = y_ref[c] * _silu(z_ref[t0:t0 + CHUNK, :])
        y = y * lax.rsqrt(jnp.mean(y * y, axis=-1, keepdims=True) + EPS)
        yg_ref[t0:t0 + CHUNK, :] = (y * nw_ref[...]).astype(BF16)


def _ssd(proj, small_t, conv_w, conv_b, a_log_rep, dt_bias_rep, d_vec, norm_w, h0t, bsz, seq_len, want_state):
    n_chunks = seq_len // CHUNK
    n_tok = bsz * seq_len
    has_h0 = h0t is not None
    xbc_w = COL_XBC // GROUP_W
    bc_l = (COL_XBC + SSD_INNER) // LANES
    in_specs = [
        pl.BlockSpec((seq_len, GROUP_W), lambda b, g: (b, xbc_w + g)),
        pl.BlockSpec((seq_len, SSD_N), lambda b, g: (b, bc_l + g)),
        pl.BlockSpec((seq_len, SSD_N), lambda b, g: (b, bc_l + SSD_GROUPS + g)),
        pl.BlockSpec((seq_len, GROUP_W), lambda b, g: (b, COL_ZS // GROUP_W + g)),
        pl.BlockSpec((HEADS_PER_GROUP, seq_len), lambda b, g: (ROW_DT // HEADS_PER_GROUP + g, b)),
        pl.BlockSpec((HEADS_PER_GROUP, seq_len), lambda b, g: ((ROW_DT + SSD_HEADS) // HEADS_PER_GROUP + g, b)),
        pl.BlockSpec((CONV_K, GROUP_W), lambda b, g: (0, g)),
        pl.BlockSpec((CONV_K, SSD_N), lambda b, g: (0, SSD_INNER // SSD_N + g)),
        pl.BlockSpec((CONV_K, SSD_N), lambda b, g: (0, SSD_INNER // SSD_N + SSD_GROUPS + g)),
        pl.BlockSpec((1, GROUP_W), lambda b, g: (0, g)),
        pl.BlockSpec((1, SSD_N), lambda b, g: (0, SSD_INNER // SSD_N + g)),
        pl.BlockSpec((1, SSD_N), lambda b, g: (0, SSD_INNER // SSD_N + SSD_GROUPS + g)),
        pl.BlockSpec((2, HEADS_PER_GROUP, LANES), lambda b, g: (0, g, 0)),
        pl.BlockSpec((2, HEADS_PER_GROUP, LANES), lambda b, g: (0, g, 0)),
        pl.BlockSpec((1, GROUP_W), lambda b, g: (0, g)),
        pl.BlockSpec((1, GROUP_W), lambda b, g: (0, g)),
    ]
    args = [proj, proj, proj, proj, small_t, small_t, conv_w, conv_w, conv_w, conv_b, conv_b, conv_b,
            a_log_rep, dt_bias_rep, d_vec, norm_w]
    state_spec = pl.BlockSpec((None, 2, None, SSD_N, GROUP_W), lambda b, g: (b, 0, g, 0, 0))
    if has_h0:
        in_specs.append(state_spec)
        args.append(h0t)
    out_specs = [pl.BlockSpec((seq_len, GROUP_W), lambda b, g: (b, g))]
    out_shape = [jax.ShapeDtypeStruct((n_tok, SSD_INNER), BF16)]
    if want_state:
        out_specs.append(state_spec)
        out_shape.append(jax.ShapeDtypeStruct((bsz, 2, SSD_GROUPS, SSD_N, GROUP_W), F32))
    rows = pltpu.VMEM((n_chunks, 2 * HEADS_PER_GROUP, CHUNK), F32)
    return pl.pallas_call(
        functools.partial(_ssd_kernel, n_chunks=n_chunks, has_h0=has_h0, want_state=want_state),
        grid=(bsz, SSD_GROUPS),
        in_specs=in_specs,
        out_specs=out_specs,
        out_shape=out_shape,
        scratch_shapes=[
            pltpu.VMEM((seq_len + 2 * CONV_PAD, GROUP_W + 2 * SSD_N), F32),
            pltpu.VMEM((n_chunks, CHUNK, GROUP_W), F32),
            pltpu.VMEM((n_chunks, CHUNK, SSD_N), F32),
            pltpu.VMEM((n_chunks, CHUNK, SSD_N), F32),
            pltpu.VMEM((n_chunks, SSD_N, CHUNK), F32),
            rows, rows,
            pltpu.VMEM((n_chunks, CHUNK, GROUP_W), F32),
            pltpu.VMEM((2, SSD_N, GROUP_W), F32),
        ],
        compiler_params=pltpu.CompilerParams(dimension_semantics=("arbitrary", "arbitrary"),
                                             vmem_limit_bytes=VMEM_LIMIT),
        name="ssd",
    )(*args)


def _route(logits):
    lane = lax.broadcasted_iota(jnp.int32, logits.shape, 1)
    neg = -jnp.inf
    is_grp = (lane >= N_EXPERTS) & (lane < N_EXPERTS + N_GROUPS)
    gl = jnp.where(is_grp, logits, neg)
    gmax = jnp.max(gl, axis=-1, keepdims=True)
    ge = jnp.exp(gl - gmax)
    p_grp = ge / jnp.sum(ge, axis=-1, keepdims=True)
    p_top = jnp.max(p_grp, axis=-1, keepdims=True)
    g_idx = jnp.min(jnp.where(is_grp & (p_grp == p_top), lane, 2 * LANES), axis=-1, keepdims=True) - N_EXPERTS
    in_grp = (lane >= g_idx * EXPERTS_PER_GROUP) & (lane < (g_idx + 1) * EXPERTS_PER_GROUP)
    el = jnp.where(in_grp, logits, neg)
    emax = jnp.max(el, axis=-1, keepdims=True)
    ee = jnp.exp(el - emax)
    p_e = ee / jnp.sum(ee, axis=-1, keepdims=True)
    w1 = jnp.max(p_e, axis=-1, keepdims=True)
    i1 = jnp.min(jnp.where(in_grp & (p_e == w1), lane, 2 * LANES), axis=-1, keepdims=True)
    rest = jnp.where(in_grp & (lane != i1), p_e, -1.0)
    w2 = jnp.max(rest, axis=-1, keepdims=True)
    i2 = jnp.min(jnp.where(rest == w2, lane, 2 * LANES), axis=-1, keepdims=True)
    tot = w1 + w2
    return jnp.where(lane == i1, w1 / tot * p_top, 0.0) + jnp.where(lane == i2, w2 / tot * p_top, 0.0)


def _post_kernel(*refs, has_pos):
    it = iter(refs)
    x_ref = next(it)
    pos_ref = next(it) if has_pos else None
    og_ref, yg_ref, gg_ref, gs_ref, ada_ref = (next(it) for _ in range(5))
    wg_ref, ws_ref, wo_ref, g1_ref, b1_ref, wr_ref, br_ref = (next(it) for _ in range(7))
    x1_ref, h2_ref, gates_ref = (next(it) for _ in range(3))

    x = x_ref[...]
    if has_pos:
        x = x + pos_ref[...]
    u_g = jnp.dot(og_ref[...], wg_ref[...], preferred_element_type=F32)
    u_s = jnp.dot(yg_ref[...], ws_ref[...], preferred_element_type=F32)
    m = jax.nn.sigmoid(gg_ref[...]) * u_g + jax.nn.sigmoid(gs_ref[...]) * u_s
    mix = jnp.dot(m.astype(BF16), wo_ref[...], preferred_element_type=F32)
    gate1 = ada_ref[:, 2 * D_MODEL:3 * D_MODEL]
    shift2 = ada_ref[:, 3 * D_MODEL:4 * D_MODEL]
    scale2 = ada_ref[:, 4 * D_MODEL:5 * D_MODEL]
    x1 = _ln(ALPHA * x + gate1 * mix) * g1_ref[...] + b1_ref[...]
    x1_ref[...] = x1
    h2 = _ln(x1) * (1.0 + scale2) + shift2
    h2_ref[...] = h2.astype(BF16)
    logits = jnp.dot(h2, wr_ref[...], precision=HIGHEST, preferred_element_type=F32) + br_ref[...]
    gates_ref[...] = _route(logits)


def _post(x2d, pos, og, yg, proj, ada3, w_gdn_out, w_ssd_out, w_o, ln_g, ln_b, w_router, b_router,
          seq_len, ada_row0):
    n_tok = x2d.shape[0]
    t = POST_BLOCK
    blocks_per_seq = seq_len // t
    has_pos = pos is not None

    def ada_map(i):
        return (ada_row0 + (i // blocks_per_seq if has_pos else 0), 0, 0)

    const = lambda i: (0, 0)
    in_specs = [pl.BlockSpec((t, D_MODEL), lambda i: (i, 0))]
    args = [x2d]
    if has_pos:
        in_specs.append(pl.BlockSpec((t, D_MODEL), lambda i: (i % blocks_per_seq, 0)))
        args.append(pos)
    in_specs += [
        pl.BlockSpec((t, GDN_HEADS * GDN_DV), lambda i: (i, 0)),
        pl.BlockSpec((t, SSD_INNER), lambda i: (i, 0)),
        pl.BlockSpec((t, D_MODEL), lambda i: (i, COL_GATE // D_MODEL)),
        pl.BlockSpec((t, D_MODEL), lambda i: (i, COL_GATE // D_MODEL + 1)),
        pl.BlockSpec((None, 1, 6 * D_MODEL), ada_map),
        pl.BlockSpec((GDN_HEADS * GDN_DV, D_MODEL), const),
        pl.BlockSpec((SSD_INNER, D_MODEL), const),
        pl.BlockSpec((D_MODEL, D_MODEL), const),
        pl.BlockSpec((1, D_MODEL), const),
        pl.BlockSpec((1, D_MODEL), const),
        pl.BlockSpec((D_MODEL, LANES), const),
        pl.BlockSpec((1, LANES), const),
    ]
    args += [og, yg, proj, proj, ada3, w_gdn_out, w_ssd_out, w_o, ln_g, ln_b, w_router, b_router]
    return pl.pallas_call(
        functools.partial(_post_kernel, has_pos=has_pos),
        grid=(n_tok // t,),
        in_specs=in_specs,
        out_specs=[
            pl.BlockSpec((t, D_MODEL), lambda i: (i, 0)),
            pl.BlockSpec((t, D_MODEL), lambda i: (i, 0)),
            pl.BlockSpec((t, LANES), lambda i: (i, 0)),
        ],
        out_shape=[
            jax.ShapeDtypeStruct((n_tok, D_MODEL), F32),
            jax.ShapeDtypeStruct((n_tok, D_MODEL), BF16),
            jax.ShapeDtypeStruct((n_tok, LANES), F32),
        ],
        compiler_params=pltpu.CompilerParams(dimension_semantics=("arbitrary",), vmem_limit_bytes=VMEM_LIMIT),
        name="post",
    )(*args)


def _moe_kernel(h_ref, gates_ref, x1_ref, ada_ref, wg_ref, wu_ref, wd_ref, g2_ref, b2_ref, out_ref, acc_ref):
    e = pl.program_id(1)

    @pl.when(e == 0)
    def _():
        acc_ref[...] = jnp.zeros(acc_ref.shape, F32)

    h = h_ref[...]
    hid = _silu(jnp.dot(h, wg_ref[...], preferred_element_type=F32)) * jnp.dot(h, wu_ref[...],
                                                                                preferred_element_type=F32)
    gates = gates_ref[...]
    lane = lax.broadcasted_iota(jnp.int32, gates.shape, 1)
    gate_e = jnp.sum(jnp.where(lane == e, gates, 0.0), axis=-1, keepdims=True)
    acc_ref[...] += jnp.dot((hid * gate_e).astype(BF16), wd_ref[...], preferred_element_type=F32)

    @pl.when(e == N_EXPERTS - 1)
    def _():
        gate2 = ada_ref[:, 5 * D_MODEL:6 * D_MODEL]
        out_ref[...] = _ln(ALPHA * x1_ref[...] + gate2 * acc_ref[...]) * g2_ref[...] + b2_ref[...]


def _moe(h2, gates, x1, ada3, w_gate, w_up, w_down, ln_g, ln_b, seq_len, ada_row0, per_seq_ada):
    n_tok = h2.shape[0]
    t = MOE_BLOCK
    blocks_per_seq = seq_len // t
    assert not per_seq_ada or seq_len % t == 0

    def ada_map(i, e):
        return (ada_row0 + (i // blocks_per_seq if per_seq_ada else 0), 0, 0)

    return pl.pallas_call(
        _moe_kernel,
        grid=(n_tok // t, N_EXPERTS),
        in_specs=[
            pl.BlockSpec((t, D_MODEL), lambda i, e: (i, 0)),
            pl.BlockSpec((t, LANES), lambda i, e: (i, 0)),
            pl.BlockSpec((t, D_MODEL), lambda i, e: (i, 0)),
            pl.BlockSpec((None, 1, 6 * D_MODEL), ada_map),
            pl.BlockSpec((None, D_MODEL, D_EXPERT), lambda i, e: (e, 0, 0)),
            pl.BlockSpec((None, D_MODEL, D_EXPERT), lambda i, e: (e, 0, 0)),
            pl.BlockSpec((None, D_EXPERT, D_MODEL), lambda i, e: (e, 0, 0)),
            pl.BlockSpec((1, D_MODEL), lambda i, e: (0, 0)),
            pl.BlockSpec((1, D_MODEL), lambda i, e: (0, 0)),
        ],
        out_specs=pl.BlockSpec((t, D_MODEL), lambda i, e: (i, 0)),
        out_shape=jax.ShapeDtypeStruct((n_tok, D_MODEL), F32),
        scratch_shapes=[pltpu.VMEM((t, D_MODEL), F32)],
        compiler_params=pltpu.CompilerParams(dimension_semantics=("arbitrary", "arbitrary"),
                                             vmem_limit_bytes=VMEM_LIMIT),
        name="moe",
    )(h2, gates, x1, ada3, w_gate, w_up, w_down, ln_g, ln_b)


def _grid_pos_embed(n_tokens, d):
    rows = n_tokens // GRID_W
    rr, cc = jnp.meshgrid(jnp.arange(rows, dtype=F32), jnp.arange(GRID_W, dtype=F32), indexing="ij")
    quarter = d // 4
    freqs = jnp.exp(-math.log(POS_BASE) * jnp.arange(quarter, dtype=F32) / quarter)
    ang_r = rr.reshape(-1, 1) * freqs
    ang_c = cc.reshape(-1, 1) * freqs
    return jnp.concatenate([jnp.sin(ang_r), jnp.cos(ang_r), jnp.sin(ang_c), jnp.cos(ang_c)], axis=-1)


def _lane_rep(v):
    return jnp.broadcast_to(v[..., None], v.shape + (LANES,)).astype(F32)


def _stream(x3d, pos, ada3, ada_row0, s_gdn0, h_ssd0t, want_state, wts):
    bsz, seq_len, _ = x3d.shape
    x2d = x3d.reshape(bsz * seq_len, D_MODEL)
    proj, small_t = _in_proj(x2d, pos, ada3, wts["w_big"], wts["w_small_t"], seq_len, ada_row0)
    gdn_out = _gdn(proj, small_t, wts["gdn_conv_w"], wts["gdn_conv_b"], wts["gdn_params"],
                   wts["gdn_norm_w"], s_gdn0, bsz, seq_len, want_state)
    ssd_out = _ssd(proj, small_t, wts["ssd_conv_w"], wts["ssd_conv_b"], wts["ssd_a_log"], wts["ssd_dt_bias"],
                   wts["ssd_d"], wts["ssd_norm_w"], h_ssd0t, bsz, seq_len, want_state)
    x1, h2, gates = _post(x2d, pos, gdn_out[0], ssd_out[0], proj, ada3, wts["w_gdn_out"], wts["w_ssd_out"],
                          wts["w_o"], wts["ln1_g"], wts["ln1_b"], wts["w_router"], wts["b_router"],
                          seq_len, ada_row0)
    y = _moe(h2, gates, x1, ada3, wts["w_exp_gate"], wts["w_exp_up"], wts["w_exp_down"], wts["ln2_g"],
             wts["ln2_b"], seq_len, ada_row0, pos is not None)
    states = (gdn_out[1], ssd_out[1]) if want_state else None
    return y.reshape(bsz, seq_len, D_MODEL), states


def kernel(x_prompt, x_sample, state_gdn, state_ssd, c, c_ctx, w_ada, b_ada, w_in, gdn_conv_w, gdn_conv_b, gdn_a_log, gdn_dt_bias, gdn_norm_w, w_gdn_out, ssd_conv_w, ssd_conv_b, ssd_a_log, ssd_dt_bias, ssd_d, ssd_norm_w, w_ssd_out, w_o, ln1_g, ln1_b, w_router_group, b_router_group, w_router_expert, b_router_expert, w_exp_gate, w_exp_up, w_exp_down, ln2_g, ln2_b):
    assert w_in.shape[0] == DEPTH == 1
    l = 0
    bsz_c = x_prompt.shape[0]
    bsz_l, seq_l, _ = x_sample.shape

    cvec = jnp.zeros((SUBLANES, D_MODEL), F32).at[0].set(c_ctx).at[1:1 + bsz_l].set(c)
    ada3 = _ada(cvec, w_ada[l], b_ada[l]).reshape(SUBLANES, 1, 6 * D_MODEL)

    wi = w_in[l]
    o_zg = 3072
    o_beta = 4096
    o_a = 4112
    o_xbc = 4128
    o_zs = 7200
    o_dt = 9248
    o_gate = 9312
    w_big = jnp.concatenate([wi[:, 0:o_zg], wi[:, o_zg:o_beta], wi[:, o_xbc:o_zs], wi[:, o_zs:o_dt],
                             wi[:, o_gate:]], axis=1).astype(BF16)
    w_beta = wi[:, o_beta:o_a].T.reshape(2, GDN_HEADS, D_MODEL)
    w_a = wi[:, o_a:o_xbc].T.reshape(2, GDN_HEADS, D_MODEL)
    w_head = jnp.concatenate([w_beta, w_a, jnp.zeros((SUBLANES - 4, GDN_HEADS, D_MODEL), F32)], axis=0)
    w_head = jnp.transpose(w_head, (1, 0, 2)).reshape(ROW_DT, D_MODEL)
    w_small_t = jnp.concatenate([w_head, wi[:, o_dt:o_gate].T], axis=0).astype(BF16)
    assert w_small_t.shape == (SMALL_ROWS, D_MODEL)
    gdn_params = jnp.concatenate([gdn_a_log[l], gdn_dt_bias[l], jnp.zeros((SUBLANES - 4, GDN_HEADS), F32)], axis=0)
    gdn_params = _lane_rep(gdn_params.T)

    w_router = jnp.zeros((D_MODEL, LANES), F32)
    w_router = w_router.at[:, 0:N_EXPERTS].set(w_router_expert[l]).at[:, N_EXPERTS:N_EXPERTS + N_GROUPS].set(
        w_router_group[l])
    b_router = jnp.zeros((1, LANES), F32)
    b_router = b_router.at[0, 0:N_EXPERTS].set(b_router_expert[l]).at[0, N_EXPERTS:N_EXPERTS + N_GROUPS].set(
        b_router_group[l])

    wts = {
        "w_big": w_big, "w_small_t": w_small_t,
        "gdn_conv_w": gdn_conv_w[l], "gdn_conv_b": gdn_conv_b[l].reshape(1, -1),
        "gdn_params": gdn_params,
        "gdn_norm_w": gdn_norm_w[l].reshape(1, -1),
        "ssd_conv_w": ssd_conv_w[l], "ssd_conv_b": ssd_conv_b[l].reshape(1, -1),
        "ssd_a_log": _lane_rep(ssd_a_log[l]), "ssd_dt_bias": _lane_rep(ssd_dt_bias[l]),
        "ssd_d": jnp.repeat(ssd_d[l], SSD_P).reshape(1, -1), "ssd_norm_w": ssd_norm_w[l].reshape(1, -1),
        "w_gdn_out": w_gdn_out[l].astype(BF16), "w_ssd_out": w_ssd_out[l].astype(BF16),
        "w_o": w_o[l].astype(BF16),
        "ln1_g": ln1_g[l].reshape(1, -1), "ln1_b": ln1_b[l].reshape(1, -1),
        "w_router": w_router, "b_router": b_router,
        "w_exp_gate": w_exp_gate[l].astype(BF16), "w_exp_up": w_exp_up[l].astype(BF16),
        "w_exp_down": w_exp_down[l].astype(BF16),
        "ln2_g": ln2_g[l].reshape(1, -1), "ln2_b": ln2_b[l].reshape(1, -1),
    }

    pos = _grid_pos_embed(seq_l, D_MODEL)
    h0 = state_ssd[:, l].reshape(bsz_l, 2, SSD_GROUPS, HEADS_PER_GROUP, SSD_P, SSD_N)
    h0t = jnp.transpose(h0, (0, 1, 2, 5, 3, 4)).reshape(bsz_l, 2, SSD_GROUPS, SSD_N, GROUP_W)

    y_ctx, (s_gdn, h_ssd_t) = _stream(x_prompt, None, ada3, 0, None, None, True, wts)
    y_lat, _ = _stream(x_sample, pos, ada3, 1, state_gdn, h0t, False, wts)

    h_ssd = h_ssd_t.reshape(bsz_c, 2, SSD_GROUPS, SSD_N, HEADS_PER_GROUP, SSD_P)
    new_ssd = jnp.transpose(h_ssd, (0, 1, 2, 4, 5, 3)).reshape(bsz_c, 1, 2, SSD_HEADS, SSD_P, SSD_N)
    return (y_ctx, y_lat, s_gdn, new_ssd)
```

```python
import functools
import math

import jax
import jax.numpy as jnp
from jax import lax
from jax.experimental import pallas as pl
from jax.experimental.pallas import tpu as pltpu

F32 = jnp.float32
BF16 = jnp.bfloat16
HIGHEST = lax.Precision.HIGHEST

D_MODEL = 1024
GRID_W = 64
POS_BASE = 10000.0
CONV_K = 5
GDN_HEADS = 8
GDN_DK = 128
GDN_DV = 128
SSD_HEADS = 32
SSD_P = 64
SSD_INNER = SSD_HEADS * SSD_P
SSD_GROUPS = 4
SSD_N = 128
HEADS_PER_GROUP = SSD_HEADS // SSD_GROUPS
GROUP_W = HEADS_PER_GROUP * SSD_P
N_GROUPS = 4
EXPERTS_PER_GROUP = 8
N_EXPERTS = N_GROUPS * EXPERTS_PER_GROUP
D_EXPERT = 256
EPS = 1e-6
DEPTH = 1
ALPHA = (2.0 * DEPTH) ** 0.25

LANES = 128
SUBLANES = 8
CHUNK = 128
CONV_PAD = SUBLANES
VMEM_LIMIT = 56 * 1024 * 1024

COL_QKV = 0
COL_ZG = 3072
COL_XBC = 4096
COL_ZS = 7168
COL_GATE = 9216
PROJ_COLS = 11264
PROJ_TN = 1024
GDN_ROW_BETA = 0
GDN_ROW_A = 2
ROW_DT = GDN_HEADS * SUBLANES
SMALL_ROWS = 128

TOK_BLOCK = 512
POST_BLOCK = 256
MOE_BLOCK = 1024


def _bdot(a, b):
    return jnp.dot(a.astype(BF16), b.astype(BF16), preferred_element_type=F32)


def _silu(x):
    return x * jax.nn.sigmoid(x)


def _softplus(x):
    return jnp.maximum(x, 0.0) + jnp.log1p(jnp.exp(-jnp.abs(x)))


def _ln(x):
    mu = jnp.mean(x, axis=-1, keepdims=True)
    xc = x - mu
    var = jnp.mean(xc * xc, axis=-1, keepdims=True)
    return xc * lax.rsqrt(var + EPS)


def _lane_scan(x, reverse):
    lane = lax.broadcasted_iota(jnp.int32, x.shape, 1)
    s = 1
    while s < CHUNK:
        if reverse:
            shifted = pltpu.roll(x, CHUNK - s, axis=1)
            x = x + jnp.where(lane < CHUNK - s, shifted, 0.0)
        else:
            shifted = pltpu.roll(x, s, axis=1)
            x = x + jnp.where(lane >= s, shifted, 0.0)
        s *= 2
    return x


def _tri_masks(reverse):
    row = lax.broadcasted_iota(jnp.int32, (CHUNK, CHUNK), 0)
    col = lax.broadcasted_iota(jnp.int32, (CHUNK, CHUNK), 1)
    if reverse:
        return row <= col, row < col
    return row >= col, row > col


def _decay_matrix(acc_row, incl):
    acc_rb = jnp.broadcast_to(acc_row, (CHUNK, CHUNK))
    acc_col = acc_rb.T
    decay = jnp.exp(jnp.where(incl, acc_col - acc_rb, -jnp.inf))
    return decay, acc_col


def _conv_tap_sum(pad_ref, t0, lo, hi, w_ref, b_ref):
    acc = jnp.broadcast_to(b_ref[...], (CHUNK, hi - lo))
    for j in range(CONV_K):
        start = t0 + CONV_PAD - CONV_K // 2 + j
        acc = acc + pad_ref[start:start + CHUNK, lo:hi] * w_ref[j:j + 1, :]
    return acc


def _ada_kernel(c_ref, w_ref, b_ref, o_ref):
    s = _silu(c_ref[...])
    o_ref[...] = jnp.dot(s, w_ref[...], precision=HIGHEST, preferred_element_type=F32) + b_ref[...]


def _ada(cvec, w_ada, b_ada):
    n_out = w_ada.shape[1]
    tn = 1024
    return pl.pallas_call(
        _ada_kernel,
        grid=(n_out // tn,),
        in_specs=[
            pl.BlockSpec((SUBLANES, D_MODEL), lambda j: (0, 0)),
            pl.BlockSpec((D_MODEL, tn), lambda j: (0, j)),
            pl.BlockSpec((1, tn), lambda j: (0, j)),
        ],
        out_specs=pl.BlockSpec((SUBLANES, tn), lambda j: (0, j)),
        out_shape=jax.ShapeDtypeStruct((SUBLANES, n_out), F32),
        compiler_params=pltpu.CompilerParams(dimension_semantics=("arbitrary",), vmem_limit_bytes=VMEM_LIMIT),
        name="ada",
    )(cvec, w_ada, b_ada.reshape(1, n_out))


def _inproj_kernel(*refs, has_pos):
    if has_pos:
        x_ref, pos_ref, ada_ref, w_ref, wst_ref, proj_ref, small_ref, h_ref = refs
    else:
        x_ref, ada_ref, w_ref, wst_ref, proj_ref, small_ref, h_ref = refs
        pos_ref = None

    @pl.when(pl.program_id(1) == 0)
    def _():
        x = x_ref[...]
        if pos_ref is not None:
            x = x + pos_ref[...]
        shift = ada_ref[:, 0:D_MODEL]
        scale = ada_ref[:, D_MODEL:2 * D_MODEL]
        h = (_ln(x) * (1.0 + scale) + shift).astype(BF16)
        h_ref[...] = h
        small_ref[...] = lax.dot_general(wst_ref[...], h, (((1,), (1,)), ((), ())), preferred_element_type=F32)

    proj_ref[...] = jnp.dot(h_ref[...], w_ref[...], preferred_element_type=F32)


def _in_proj(x2d, pos, ada3, w_big, w_small_t, seq_len, ada_row0):
    n_tok = x2d.shape[0]
    t = TOK_BLOCK
    blocks_per_seq = seq_len // t
    has_pos = pos is not None

    def ada_map(i, j):
        return (ada_row0 + (i // blocks_per_seq if has_pos else 0), 0, 0)

    in_specs = [pl.BlockSpec((t, D_MODEL), lambda i, j: (i, 0))]
    args = [x2d]
    if has_pos:
        in_specs.append(pl.BlockSpec((t, D_MODEL), lambda i, j: (i % blocks_per_seq, 0)))
        args.append(pos)
    in_specs += [
        pl.BlockSpec((None, 1, 6 * D_MODEL), ada_map),
        pl.BlockSpec((D_MODEL, PROJ_TN), lambda i, j: (0, j)),
        pl.BlockSpec((SMALL_ROWS, D_MODEL), lambda i, j: (0, 0)),
    ]
    args += [ada3, w_big, w_small_t]
    return pl.pallas_call(
        functools.partial(_inproj_kernel, has_pos=has_pos),
        grid=(n_tok // t, PROJ_COLS // PROJ_TN),
        in_specs=in_specs,
        out_specs=[
            pl.BlockSpec((t, PROJ_TN), lambda i, j: (i, j)),
            pl.BlockSpec((SMALL_ROWS, t), lambda i, j: (0, i)),
        ],
        out_shape=[
            jax.ShapeDtypeStruct((n_tok, PROJ_COLS), F32),
            jax.ShapeDtypeStruct((SMALL_ROWS, n_tok), F32),
        ],
        scratch_shapes=[pltpu.VMEM((t, D_MODEL), BF16)],
        compiler_params=pltpu.CompilerParams(dimension_semantics=("arbitrary", "arbitrary"),
                                             vmem_limit_bytes=VMEM_LIMIT),
        name="in_proj",
    )(*args)


INV_BASE = 8
GDN_GROUP = 4


def _unit_tri_inverses(nmats):
    row = lax.broadcasted_iota(jnp.int32, (CHUNK, CHUNK), 0)
    col = lax.broadcasted_iota(jnp.int32, (CHUNK, CHUNK), 1)

    def same_block(size):
        shift = int(math.log2(size))
        return (row >> shift) == (col >> shift)

    eye = (row == col).astype(F32)
    base = same_block(INV_BASE)
    nds = [jnp.where(base, n, 0.0) for n in nmats]
    xs = [eye - nd for nd in nds]
    pws = [_bdot(nd, nd) for nd in nds]
    size = 2
    while True:
        xs = [x + _bdot(x, pw) for x, pw in zip(xs, pws)]
        size *= 2
        if size >= INV_BASE:
            break
        pws = [_bdot(pw, pw) for pw in pws]
    size = INV_BASE
    while size < CHUNK:
        pair, inner = same_block(2 * size), same_block(size)
        tmps = [_bdot(jnp.where(pair, jnp.where(inner, 0.0, n), 0.0), x) for n, x in zip(nmats, xs)]
        xs = [x - _bdot(x, t) for x, t in zip(xs, tmps)]
        size *= 2
    return xs


def _gdn_prepare(chunks, qn_ref, kn_ref, vn_ref, kt_ref, sc_ref, lhs_ref, sb_ref, ob_ref, eg_ref):
    grams = [_bdot(jnp.concatenate([kn_ref[g], qn_ref[g]], axis=0), kt_ref[g]) for g in chunks]
    units = []
    for g, gram in zip(chunks, grams):
        for d in range(2):
            reverse = d == 1
            incl, strict = _tri_masks(reverse)
            beta = sc_ref[g, d:d + 1, :]
            gc = sc_ref[g, 2 + d:3 + d, :]
            decay, gcol = _decay_matrix(gc, incl)
            beta_rb = jnp.broadcast_to(beta, (CHUNK, CHUNK))
            glast = gc[:, 0:1] if reverse else gc[:, CHUNK - 1:CHUNK]
            sc_ref[g, 4 + d:5 + d, :] = jnp.broadcast_to(jnp.exp(glast), (1, CHUNK))
            eg_ref[g, d] = jnp.exp(gcol)
            units.append(dict(
                g=g, d=d,
                nmat=jnp.where(strict, gram[0:CHUNK] * decay, 0.0) * beta_rb,
                att=gram[CHUNK:2 * CHUNK] * decay * beta_rb,
                kd_scale=beta * jnp.exp(glast - gc)))
    xinvs = _unit_tri_inverses([u["nmat"] for u in units])
    uws = [_bdot(x, jnp.concatenate([vn_ref[u["g"]], kn_ref[u["g"]] * eg_ref[u["g"], u["d"]]], axis=1))
           for u, x in zip(units, xinvs)]
    mixeds = []
    for u, uw in zip(units, uws):
        kd = kt_ref[u["g"]] * jnp.broadcast_to(u["kd_scale"], (CHUNK, CHUNK))
        mixeds.append(_bdot(jnp.concatenate([kd, u["att"]], axis=0), uw))
    for u, mixed in zip(units, mixeds):
        g, d = u["g"], u["d"]
        lhs_ref[g, d, 0:CHUNK, :] = mixed[0:CHUNK, GDN_DV:].astype(BF16)
        lhs_ref[g, d, CHUNK:2 * CHUNK, :] = qn_ref[g].astype(BF16)
        lhs_ref[g, d, 2 * CHUNK:3 * CHUNK, :] = mixed[CHUNK:, GDN_DV:].astype(BF16)
        sb_ref[g, d] = mixed[0:CHUNK, 0:GDN_DV]
        ob_ref[g, d] = mixed[CHUNK:, 0:GDN_DV]


def _gdn_steps(steps, lhs_ref, sb_ref, ob_ref, eg_ref, sc_ref, s_ref, o_refs):
    states = [s_ref[s, d] for _, s, d in steps]
    rs = [jnp.dot(lhs_ref[g, d], st.astype(BF16), preferred_element_type=F32)
          for (g, _, d), st in zip(steps, states)]
    for (g, s, d), st, r in zip(steps, states, rs):
        s_ref[s, d] = st * sc_ref[g, 4 + d:5 + d, :] - r[0:CHUNK] + sb_ref[g, d]
        o_refs[d][g] = eg_ref[g, d] * r[CHUNK:2 * CHUNK] - r[2 * CHUNK:3 * CHUNK] + ob_ref[g, d]


def _gdn_kernel(*refs, n_seq, n_chunks, has_s0, want_state):
    it = iter(refs)
    q_ref, k_ref, v_ref, z_ref, sm_ref = (next(it) for _ in range(5))
    wq_ref, wk_ref, wv_ref, bq_ref, bk_ref, bv_ref = (next(it) for _ in range(6))
    par_ref, nw_ref = (next(it) for _ in range(2))
    s0_ref = next(it) if has_s0 else None
    og_ref = next(it)
    sout_ref = next(it) if want_state else None
    pad_ref, qn_ref, kn_ref, vn_ref, kt_ref, of_ref, ob_ref, sc_ref, s_ref = (next(it) for _ in range(9))
    lhs_ref, sb_ref, ou_ref, eg_ref = (next(it) for _ in range(4))

    seq = n_chunks * CHUNK
    total = n_seq * n_chunks
    zero_rows = jnp.zeros((CONV_PAD, 3 * LANES), F32)
    for s in range(n_seq):
        pad_ref[s, 0:CONV_PAD, :] = zero_rows
        pad_ref[s, CONV_PAD + seq:2 * CONV_PAD + seq, :] = zero_rows
        for c in range(n_chunks):
            t0 = c * CHUNK
            r0 = s * seq + t0
            pad_ref[s, CONV_PAD + t0:CONV_PAD + t0 + CHUNK, 0:LANES] = q_ref[r0:r0 + CHUNK, :]
            pad_ref[s, CONV_PAD + t0:CONV_PAD + t0 + CHUNK, LANES:2 * LANES] = k_ref[r0:r0 + CHUNK, :]
            pad_ref[s, CONV_PAD + t0:CONV_PAD + t0 + CHUNK, 2 * LANES:3 * LANES] = v_ref[r0:r0 + CHUNK, :]

    a_neg = [-jnp.exp(par_ref[d:d + 1, :]) for d in range(2)]
    dt_bias = [par_ref[2 + d:3 + d, :] for d in range(2)]
    for s in range(n_seq):
        for c in range(n_chunks):
            t0 = c * CHUNK
            r0 = s * seq + t0
            g = s * n_chunks + c
            pad_s = pad_ref.at[s]
            qa = _silu(_conv_tap_sum(pad_s, t0, 0, LANES, wq_ref, bq_ref))
            ka = _silu(_conv_tap_sum(pad_s, t0, LANES, 2 * LANES, wk_ref, bk_ref))
            va = _silu(_conv_tap_sum(pad_s, t0, 2 * LANES, 3 * LANES, wv_ref, bv_ref))
            qn = qa * lax.rsqrt(jnp.sum(qa * qa, axis=-1, keepdims=True) + EPS) * (GDN_DK ** -0.5)
            kn = ka * lax.rsqrt(jnp.sum(ka * ka, axis=-1, keepdims=True) + EPS)
            qn_ref[g] = qn
            kn_ref[g] = kn
            vn_ref[g] = va
            kt_ref[g] = kn.T
            for d in range(2):
                braw = sm_ref[GDN_ROW_BETA + d:GDN_ROW_BETA + d + 1, r0:r0 + CHUNK]
                araw = sm_ref[GDN_ROW_A + d:GDN_ROW_A + d + 1, r0:r0 + CHUNK]
                gdec = a_neg[d] * _softplus(araw + dt_bias[d])
                sc_ref[g, d:d + 1, :] = jax.nn.sigmoid(braw)
                sc_ref[g, 2 + d:3 + d, :] = _lane_scan(gdec, reverse=d == 1)

    def prepare(i, carry):
        _gdn_prepare([i * GDN_GROUP + j for j in range(GDN_GROUP)], qn_ref, kn_ref, vn_ref, kt_ref, sc_ref,
                     lhs_ref, sb_ref, ou_ref, eg_ref)
        return carry

    lax.fori_loop(0, total // GDN_GROUP, prepare, 0)

    if has_s0:
        s_ref[...] = s0_ref[...]
    else:
        s_ref[...] = jnp.zeros(s_ref.shape, F32)

    def advance(i, carry):
        steps = []
        for s in range(n_seq):
            steps.append((s * n_chunks + i, s, 0))
            steps.append((s * n_chunks + n_chunks - 1 - i, s, 1))
        _gdn_steps(steps, lhs_ref, sb_ref, ou_ref, eg_ref, sc_ref, s_ref, (of_ref, ob_ref))
        return carry

    lax.fori_loop(0, n_chunks, advance, 0)

    if want_state:
        sout_ref[...] = s_ref[...]
    for g in range(total):
        r0 = g * CHUNK
        o = of_ref[g] + ob_ref[g]
        o = o * lax.rsqrt(jnp.mean(o * o, axis=-1, keepdims=True) + EPS)
        og_ref[r0:r0 + CHUNK, :] = (o * nw_ref[...] * _silu(z_ref[r0:r0 + CHUNK, :])).astype(BF16)


def _gdn(proj, small_t, conv_w, conv_b, head_params, norm_w, s0, bsz, seq_len, want_state):
    n_chunks = seq_len // CHUNK
    n_tok = bsz * seq_len
    has_s0 = s0 is not None
    n_seq = max(1, GDN_GROUP * 2 // n_chunks)
    assert bsz % n_seq == 0 and (n_seq * n_chunks) % GDN_GROUP == 0
    total = n_seq * n_chunks
    rows = n_seq * seq_len
    col = lambda off: (lambda b, h: (b, off + h))
    cw = lambda off: (lambda b, h: (0, off + h))
    in_specs = [
        pl.BlockSpec((rows, LANES), col(COL_QKV // LANES)),
        pl.BlockSpec((rows, LANES), col(COL_QKV // LANES + GDN_HEADS)),
        pl.BlockSpec((rows, LANES), col(COL_QKV // LANES + 2 * GDN_HEADS)),
        pl.BlockSpec((rows, LANES), col(COL_ZG // LANES)),
        pl.BlockSpec((SUBLANES, rows), lambda b, h: (h, b)),
        pl.BlockSpec((CONV_K, LANES), cw(0)),
        pl.BlockSpec((CONV_K, LANES), cw(GDN_HEADS)),
        pl.BlockSpec((CONV_K, LANES), cw(2 * GDN_HEADS)),
        pl.BlockSpec((1, LANES), cw(0)),
        pl.BlockSpec((1, LANES), cw(GDN_HEADS)),
        pl.BlockSpec((1, LANES), cw(2 * GDN_HEADS)),
        pl.BlockSpec((None, SUBLANES, LANES), lambda b, h: (h, 0, 0)),
        pl.BlockSpec((1, LANES), lambda b, h: (0, 0)),
    ]
    args = [proj, proj, proj, proj, small_t, conv_w, conv_w, conv_w, conv_b, conv_b, conv_b,
            head_params, norm_w]
    state_spec = pl.BlockSpec((n_seq, None, 2, None, GDN_DK, GDN_DV), lambda b, h: (b, 0, 0, h, 0, 0))
    if has_s0:
        in_specs.append(state_spec)
        args.append(s0)
    out_specs = [pl.BlockSpec((rows, LANES), lambda b, h: (b, h))]
    out_shape = [jax.ShapeDtypeStruct((n_tok, GDN_HEADS * GDN_DV), BF16)]
    if want_state:
        out_specs.append(state_spec)
        out_shape.append(jax.ShapeDtypeStruct((bsz, 1, 2, GDN_HEADS, GDN_DK, GDN_DV), F32))
    chunked = pltpu.VMEM((total, CHUNK, LANES), F32)
    per_dir = pltpu.VMEM((total, 2, CHUNK, LANES), F32)
    return pl.pallas_call(
        functools.partial(_gdn_kernel, n_seq=n_seq, n_chunks=n_chunks, has_s0=has_s0, want_state=want_state),
        grid=(bsz // n_seq, GDN_HEADS),
        in_specs=in_specs,
        out_specs=out_specs,
        out_shape=out_shape,
        scratch_shapes=[
            pltpu.VMEM((n_seq, seq_len + 2 * CONV_PAD, 3 * LANES), F32),
            chunked, chunked, chunked, chunked, chunked, chunked,
            pltpu.VMEM((total, SUBLANES, CHUNK), F32),
            pltpu.VMEM((n_seq, 2, GDN_DK, GDN_DV), F32),
            pltpu.VMEM((total, 2, 3 * CHUNK, GDN_DK), BF16),
            per_dir, per_dir, per_dir,
        ],
        compiler_params=pltpu.CompilerParams(dimension_semantics=("arbitrary", "arbitrary"),
                                             vmem_limit_bytes=VMEM_LIMIT),
        name="gdn",
    )(*args)


def _ssd_steps(steps, xs_ref, cs_ref, bt_ref, dt_ref, ac_ref, ht_ref, y_ref):
    lane = lax.broadcasted_iota(jnp.int32, (CHUNK, LANES), 1)
    low_half = lane < SSD_P
    shared = [(_bdot(cs_ref[c], bt_ref[c]), _bdot(cs_ref[c], ht_ref[d])) for c, d in steps]
    pairs_per_stage = 2
    for p0 in range(0, HEADS_PER_GROUP // 2, pairs_per_stage):
        units = []
        for (c, d), (cb, y_off) in zip(steps, shared):
            reverse = d == 1
            incl, _ = _tri_masks(reverse)
            bt = bt_ref[c]
            for p in range(p0, p0 + pairs_per_stage):
                halves = []
                for r in (2 * p, 2 * p + 1):
                    row = d * HEADS_PER_GROUP + r
                    dt = dt_ref[c, row:row + 1, :]
                    ac = ac_ref[c, row:row + 1, :]
                    decay, acol = _decay_matrix(ac, incl)
                    last = ac[:, 0:1] if reverse else ac[:, CHUNK - 1:CHUNK]
                    halves.append(dict(
                        m=(cb * decay * jnp.broadcast_to(dt, (CHUNK, CHUNK))).astype(BF16),
                        bts=(bt * jnp.broadcast_to(dt * jnp.exp(last - ac), (SSD_N, CHUNK))).astype(BF16),
                        acol=acol, elast=jnp.exp(last)))
                units.append(dict(c=c, d=d, p=p, halves=halves, y_off=y_off[:, p * LANES:(p + 1) * LANES]))
        for u in units:
            x = xs_ref[u["c"], :, u["p"] * LANES:(u["p"] + 1) * LANES]
            zero = jnp.zeros_like(x)
            xh = (jnp.where(low_half, x, zero), jnp.where(low_half, zero, x))
            u["yd"] = [jnp.dot(hf["m"], xm, preferred_element_type=F32) for hf, xm in zip(u["halves"], xh)]
            u["st"] = [jnp.dot(hf["bts"], xm, preferred_element_type=F32) for hf, xm in zip(u["halves"], xh)]
        for u in units:
            c, d, p = u["c"], u["d"], u["p"]
            h0, h1 = u["halves"]
            eoff = jnp.exp(jnp.where(low_half, h0["acol"], h1["acol"]))
            elast = jnp.where(low_half, h0["elast"], h1["elast"])
            cols = slice(p * LANES, (p + 1) * LANES)
            y_ref[c, :, cols] = y_ref[c, :, cols] + (u["yd"][0] + u["yd"][1]) + eoff * u["y_off"]
            ht_ref[d, :, cols] = ht_ref[d, :, cols] * elast + (u["st"][0] + u["st"][1])


def _ssd_kernel(*refs, n_chunks, has_h0, want_state):
    it = iter(refs)
    x_ref, b_ref, c_ref, z_ref, dtf_ref, dtb_ref = (next(it) for _ in range(6))
    wx_ref, wb_ref, wc_ref, bx_ref, bb_ref, bc_ref = (next(it) for _ in range(6))
    alog_ref, dtbias_ref, dvec_ref, nw_ref = (next(it) for _ in range(4))
    h0_ref = next(it) if has_h0 else None
    yg_ref = next(it)
    hout_ref = next(it) if want_state else None
    pad_ref, xs_ref, cs_ref, bt_ref, dt_ref, ac_ref, y_ref, ht_ref = (next(it) for _ in range(8))

    seq = n_chunks * CHUNK
    width = GROUP_W + 2 * SSD_N
    zero_rows = jnp.zeros((CONV_PAD, width), F32)
    pad_ref[0:CONV_PAD, :] = zero_rows
    pad_ref[CONV_PAD + seq:2 * CONV_PAD + seq, :] = zero_rows
    for c in range(n_chunks):
        t0 = c * CHUNK
        pad_ref[CONV_PAD + t0:CONV_PAD + t0 + CHUNK, 0:GROUP_W] = x_ref[t0:t0 + CHUNK, :]
        pad_ref[CONV_PAD + t0:CONV_PAD + t0 + CHUNK, GROUP_W:GROUP_W + SSD_N] = b_ref[t0:t0 + CHUNK, :]
        pad_ref[CONV_PAD + t0:CONV_PAD + t0 + CHUNK, GROUP_W + SSD_N:width] = c_ref[t0:t0 + CHUNK, :]

    raw_dt = (dtf_ref, dtb_ref)
    for c in range(n_chunks):
        t0 = c * CHUNK
        for lo in range(0, GROUP_W, LANES):
            cols = slice(lo, lo + LANES)
            xa = _silu(_conv_tap_sum(pad_ref, t0, lo, lo + LANES, wx_ref.at[:, cols], bx_ref.at[:, cols]))
            xs_ref[c, :, cols] = xa.astype(BF16)
            y_ref[c, :, cols] = xa * dvec_ref[:, cols]
        ba = _silu(_conv_tap_sum(pad_ref, t0, GROUP_W, GROUP_W + SSD_N, wb_ref, bb_ref))
        ca = _silu(_conv_tap_sum(pad_ref, t0, GROUP_W + SSD_N, width, wc_ref, bc_ref))
        cs_ref[c] = ca
        bt_ref[c] = ba.T
        for d in range(2):
            dt = _softplus(raw_dt[d][:, t0:t0 + CHUNK] + dtbias_ref[d])
            lo, hi = d * HEADS_PER_GROUP, (d + 1) * HEADS_PER_GROUP
            dt_ref[c, lo:hi, :] = dt
            ac_ref[c, lo:hi, :] = _lane_scan(dt * (-jnp.exp(alog_ref[d])), reverse=d == 1)

    if has_h0:
        ht_ref[...] = h0_ref[...]
    else:
        ht_ref[...] = jnp.zeros(ht_ref.shape, F32)

    def body(i, carry):
        _ssd_steps([(i, 0), (n_chunks - 1 - i, 1)], xs_ref, cs_ref, bt_ref, dt_ref, ac_ref, ht_ref, y_ref)
        return carry

    lax.fori_loop(0, n_chunks, body, 0)

    if want_state:
        hout_ref[...] = ht_ref[...]
    for c in range(n_chunks):
        t0 = c * CHUNK
        ssq = jnp.zeros((CHUNK, 1), F32)
        for lo in range(0, GROUP_W, LANES):
            cols = slice(lo, lo + LANES)
            y = y_ref[c, :, cols] * _silu(z_ref[t0:t0 + CHUNK, cols])
            y_ref[c, :, cols] = y
            ssq = ssq + jnp.sum(y * y, axis=-1, keepdims=True)
        inv = lax.rsqrt(ssq * (1.0 / GROUP_W) + EPS)
        for lo in range(0, GROUP_W, LANES):
            cols = slice(lo, lo + LANES)
            yg_ref[t0:t0 + CHUNK, cols] = (y_ref[c, :, cols] * inv * nw_ref[:, cols]).astype(BF16)


def _ssd(proj, small_t, conv_w, conv_b, a_log_rep, dt_bias_rep, d_vec, norm_w, h0t, bsz, seq_len, want_state):
    n_chunks = seq_len // CHUNK
    n_tok = bsz * seq_len
    has_h0 = h0t is not None
    xbc_w = COL_XBC // GROUP_W
    bc_l = (COL_XBC + SSD_INNER) // LANES
    in_specs = [
        pl.BlockSpec((seq_len, GROUP_W), lambda b, g: (b, xbc_w + g)),
        pl.BlockSpec((seq_len, SSD_N), lambda b, g: (b, bc_l + g)),
        pl.BlockSpec((seq_len, SSD_N), lambda b, g: (b, bc_l + SSD_GROUPS + g)),
        pl.BlockSpec((seq_len, GROUP_W), lambda b, g: (b, COL_ZS // GROUP_W + g)),
        pl.BlockSpec((HEADS_PER_GROUP, seq_len), lambda b, g: (ROW_DT // HEADS_PER_GROUP + g, b)),
        pl.BlockSpec((HEADS_PER_GROUP, seq_len), lambda b, g: ((ROW_DT + SSD_HEADS) // HEADS_PER_GROUP + g, b)),
        pl.BlockSpec((CONV_K, GROUP_W), lambda b, g: (0, g)),
        pl.BlockSpec((CONV_K, SSD_N), lambda b, g: (0, SSD_INNER // SSD_N + g)),
        pl.BlockSpec((CONV_K, SSD_N), lambda b, g: (0, SSD_INNER // SSD_N + SSD_GROUPS + g)),
        pl.BlockSpec((1, GROUP_W), lambda b, g: (0, g)),
        pl.BlockSpec((1, SSD_N), lambda b, g: (0, SSD_INNER // SSD_N + g)),
        pl.BlockSpec((1, SSD_N), lambda b, g: (0, SSD_INNER // SSD_N + SSD_GROUPS + g)),
        pl.BlockSpec((2, HEADS_PER_GROUP, LANES), lambda b, g: (0, g, 0)),
        pl.BlockSpec((2, HEADS_PER_GROUP, LANES), lambda b, g: (0, g, 0)),
        pl.BlockSpec((1, GROUP_W), lambda b, g: (0, g)),
        pl.BlockSpec((1, GROUP_W), lambda b, g: (0, g)),
    ]
    args = [proj, proj, proj, proj, small_t, small_t, conv_w, conv_w, conv_w, conv_b, conv_b, conv_b,
            a_log_rep, dt_bias_rep, d_vec, norm_w]
    state_spec = pl.BlockSpec((None, 2, None, SSD_N, GROUP_W), lambda b, g: (b, 0, g, 0, 0))
    if has_h0:
        in_specs.append(state_spec)
        args.append(h0t)
    out_specs = [pl.BlockSpec((seq_len, GROUP_W), lambda b, g: (b, g))]
    out_shape = [jax.ShapeDtypeStruct((n_tok, SSD_INNER), BF16)]
    if want_state:
        out_specs.append(state_spec)
        out_shape.append(jax.ShapeDtypeStruct((bsz, 2, SSD_GROUPS, SSD_N, GROUP_W), F32))
    rows = pltpu.VMEM((n_chunks, 2 * HEADS_PER_GROUP, CHUNK), F32)
    return pl.pallas_call(
        functools.partial(_ssd_kernel, n_chunks=n_chunks, has_h0=has_h0, want_state=want_state),
        grid=(bsz, SSD_GROUPS),
        in_specs=in_specs,
        out_specs=out_specs,
        out_shape=out_shape,
        scratch_shapes=[
            pltpu.VMEM((seq_len + 2 * CONV_PAD, GROUP_W + 2 * SSD_N), F32),
            pltpu.VMEM((n_chunks, CHUNK, GROUP_W), BF16),
            pltpu.VMEM((n_chunks, CHUNK, SSD_N), F32),
            pltpu.VMEM((n_chunks, SSD_N, CHUNK), F32),
            rows, rows,
            pltpu.VMEM((n_chunks, CHUNK, GROUP_W), F32),
            pltpu.VMEM((2, SSD_N, GROUP_W), F32),
        ],
        compiler_params=pltpu.CompilerParams(dimension_semantics=("arbitrary", "arbitrary"),
                                             vmem_limit_bytes=VMEM_LIMIT),
        name="ssd",
    )(*args)


def _route(logits):
    lane = lax.broadcasted_iota(jnp.int32, logits.shape, 1)
    neg = -jnp.inf
    is_grp = (lane >= N_EXPERTS) & (lane < N_EXPERTS + N_GROUPS)
    gl = jnp.where(is_grp, logits, neg)
    gmax = jnp.max(gl, axis=-1, keepdims=True)
    ge = jnp.exp(gl - gmax)
    p_grp = ge / jnp.sum(ge, axis=-1, keepdims=True)
    p_top = jnp.max(p_grp, axis=-1, keepdims=True)
    g_idx = jnp.min(jnp.where(is_grp & (p_grp == p_top), lane, 2 * LANES), axis=-1, keepdims=True) - N_EXPERTS
    in_grp = (lane >= g_idx * EXPERTS_PER_GROUP) & (lane < (g_idx + 1) * EXPERTS_PER_GROUP)
    el = jnp.where(in_grp, logits, neg)
    emax = jnp.max(el, axis=-1, keepdims=True)
    ee = jnp.exp(el - emax)
    p_e = ee / jnp.sum(ee, axis=-1, keepdims=True)
    w1 = jnp.max(p_e, axis=-1, keepdims=True)
    i1 = jnp.min(jnp.where(in_grp & (p_e == w1), lane, 2 * LANES), axis=-1, keepdims=True)
    rest = jnp.where(in_grp & (lane != i1), p_e, -1.0)
    w2 = jnp.max(rest, axis=-1, keepdims=True)
    i2 = jnp.min(jnp.where(rest == w2, lane, 2 * LANES), axis=-1, keepdims=True)
    tot = w1 + w2
    return jnp.where(lane == i1, w1 / tot * p_top, 0.0) + jnp.where(lane == i2, w2 / tot * p_top, 0.0)


def _post_kernel(*refs, has_pos):
    it = iter(refs)
    x_ref = next(it)
    pos_ref = next(it) if has_pos else None
    og_ref, yg_ref, gg_ref, gs_ref, ada_ref = (next(it) for _ in range(5))
    wg_ref, ws_ref, wo_ref, g1_ref, b1_ref, wr_ref, br_ref = (next(it) for _ in range(7))
    x1_ref, h2_ref, gates_ref = (next(it) for _ in range(3))

    x = x_ref[...]
    if has_pos:
        x = x + pos_ref[...]
    u_g = jnp.dot(og_ref[...], wg_ref[...], preferred_element_type=F32)
    u_s = jnp.dot(yg_ref[...], ws_ref[...], preferred_element_type=F32)
    m = jax.nn.sigmoid(gg_ref[...]) * u_g + jax.nn.sigmoid(gs_ref[...]) * u_s
    mix = jnp.dot(m.astype(BF16), wo_ref[...], preferred_element_type=F32)
    gate1 = ada_ref[:, 2 * D_MODEL:3 * D_MODEL]
    shift2 = ada_ref[:, 3 * D_MODEL:4 * D_MODEL]
    scale2 = ada_ref[:, 4 * D_MODEL:5 * D_MODEL]
    x1 = _ln(ALPHA * x + gate1 * mix) * g1_ref[...] + b1_ref[...]
    x1_ref[...] = x1
    h2 = _ln(x1) * (1.0 + scale2) + shift2
    h2_ref[...] = h2.astype(BF16)
    logits = jnp.dot(h2, wr_ref[...], precision=HIGHEST, preferred_element_type=F32) + br_ref[...]
    gates_ref[...] = _route(logits)


def _post(x2d, pos, og, yg, proj, ada3, w_gdn_out, w_ssd_out, w_o, ln_g, ln_b, w_router, b_router,
          seq_len, ada_row0):
    n_tok = x2d.shape[0]
    t = POST_BLOCK
    blocks_per_seq = seq_len // t
    has_pos = pos is not None

    def ada_map(i):
        return (ada_row0 + (i // blocks_per_seq if has_pos else 0), 0, 0)

    const = lambda i: (0, 0)
    in_specs = [pl.BlockSpec((t, D_MODEL), lambda i: (i, 0))]
    args = [x2d]
    if has_pos:
        in_specs.append(pl.BlockSpec((t, D_MODEL), lambda i: (i % blocks_per_seq, 0)))
        args.append(pos)
    in_specs += [
        pl.BlockSpec((t, GDN_HEADS * GDN_DV), lambda i: (i, 0)),
        pl.BlockSpec((t, SSD_INNER), lambda i: (i, 0)),
        pl.BlockSpec((t, D_MODEL), lambda i: (i, COL_GATE // D_MODEL)),
        pl.BlockSpec((t, D_MODEL), lambda i: (i, COL_GATE // D_MODEL + 1)),
        pl.BlockSpec((None, 1, 6 * D_MODEL), ada_map),
        pl.BlockSpec((GDN_HEADS * GDN_DV, D_MODEL), const),
        pl.BlockSpec((SSD_INNER, D_MODEL), const),
        pl.BlockSpec((D_MODEL, D_MODEL), const),
        pl.BlockSpec((1, D_MODEL), const),
        pl.BlockSpec((1, D_MODEL), const),
        pl.BlockSpec((D_MODEL, LANES), const),
        pl.BlockSpec((1, LANES), const),
    ]
    args += [og, yg, proj, proj, ada3, w_gdn_out, w_ssd_out, w_o, ln_g, ln_b, w_router, b_router]
    return pl.pallas_call(
        functools.partial(_post_kernel, has_pos=has_pos),
        grid=(n_tok // t,),
        in_specs=in_specs,
        out_specs=[
            pl.BlockSpec((t, D_MODEL), lambda i: (i, 0)),
            pl.BlockSpec((t, D_MODEL), lambda i: (i, 0)),
            pl.BlockSpec((t, LANES), lambda i: (i, 0)),
        ],
        out_shape=[
            jax.ShapeDtypeStruct((n_tok, D_MODEL), F32),
            jax.ShapeDtypeStruct((n_tok, D_MODEL), BF16),
            jax.ShapeDtypeStruct((n_tok, LANES), F32),
        ],
        compiler_params=pltpu.CompilerParams(dimension_semantics=("arbitrary",), vmem_limit_bytes=VMEM_LIMIT),
        name="post",
    )(*args)


def _moe_kernel(h_ref, gates_ref, x1_ref, ada_ref, wg_ref, wu_ref, wd_ref, g2_ref, b2_ref, out_ref, acc_ref):
    e = pl.program_id(1)

    @pl.when(e == 0)
    def _():
        acc_ref[...] = jnp.zeros(acc_ref.shape, F32)

    h = h_ref[...]
    hid = _silu(jnp.dot(h, wg_ref[...], preferred_element_type=F32)) * jnp.dot(h, wu_ref[...],
                                                                                preferred_element_type=F32)
    gates = gates_ref[...]
    lane = lax.broadcasted_iota(jnp.int32, gates.shape, 1)
    gate_e = jnp.sum(jnp.where(lane == e, gates, 0.0), axis=-1, keepdims=True)
    acc_ref[...] += jnp.dot((hid * gate_e).astype(BF16), wd_ref[...], preferred_element_type=F32)

    @pl.when(e == N_EXPERTS - 1)
    def _():
        gate2 = ada_ref[:, 5 * D_MODEL:6 * D_MODEL]
        out_ref[...] = _ln(ALPHA * x1_ref[...] + gate2 * acc_ref[...]) * g2_ref[...] + b2_ref[...]


def _moe(h2, gates, x1, ada3, w_gate, w_up, w_down, ln_g, ln_b, seq_len, ada_row0, per_seq_ada):
    n_tok = h2.shape[0]
    t = MOE_BLOCK
    blocks_per_seq = seq_len // t
    assert not per_seq_ada or seq_len % t == 0

    def ada_map(i, e):
        return (ada_row0 + (i // blocks_per_seq if per_seq_ada else 0), 0, 0)

    return pl.pallas_call(
        _moe_kernel,
        grid=(n_tok // t, N_EXPERTS),
        in_specs=[
            pl.BlockSpec((t, D_MODEL), lambda i, e: (i, 0)),
            pl.BlockSpec((t, LANES), lambda i, e: (i, 0)),
            pl.BlockSpec((t, D_MODEL), lambda i, e: (i, 0)),
            pl.BlockSpec((None, 1, 6 * D_MODEL), ada_map),
            pl.BlockSpec((None, D_MODEL, D_EXPERT), lambda i, e: (e, 0, 0)),
            pl.BlockSpec((None, D_MODEL, D_EXPERT), lambda i, e: (e, 0, 0)),
            pl.BlockSpec((None, D_EXPERT, D_MODEL), lambda i, e: (e, 0, 0)),
            pl.BlockSpec((1, D_MODEL), lambda i, e: (0, 0)),
            pl.BlockSpec((1, D_MODEL), lambda i, e: (0, 0)),
        ],
        out_specs=pl.BlockSpec((t, D_MODEL), lambda i, e: (i, 0)),
        out_shape=jax.ShapeDtypeStruct((n_tok, D_MODEL), F32),
        scratch_shapes=[pltpu.VMEM((t, D_MODEL), F32)],
        compiler_params=pltpu.CompilerParams(dimension_semantics=("arbitrary", "arbitrary"),
                                             vmem_limit_bytes=VMEM_LIMIT),
        name="moe",
    )(h2, gates, x1, ada3, w_gate, w_up, w_down, ln_g, ln_b)


def _grid_pos_embed(n_tokens, d):
    rows = n_tokens // GRID_W
    rr, cc = jnp.meshgrid(jnp.arange(rows, dtype=F32), jnp.arange(GRID_W, dtype=F32), indexing="ij")
    quarter = d // 4
    freqs = jnp.exp(-math.log(POS_BASE) * jnp.arange(quarter, dtype=F32) / quarter)
    ang_r = rr.reshape(-1, 1) * freqs
    ang_c = cc.reshape(-1, 1) * freqs
    return jnp.concatenate([jnp.sin(ang_r), jnp.cos(ang_r), jnp.sin(ang_c), jnp.cos(ang_c)], axis=-1)


def _lane_rep(v):
    return jnp.broadcast_to(v[..., None], v.shape + (LANES,)).astype(F32)


def _stream(x3d, pos, ada3, ada_row0, s_gdn0, h_ssd0t, want_state, wts):
    bsz, seq_len, _ = x3d.shape
    x2d = x3d.reshape(bsz * seq_len, D_MODEL)
    proj, small_t = _in_proj(x2d, pos, ada3, wts["w_big"], wts["w_small_t"], seq_len, ada_row0)
    gdn_out = _gdn(proj, small_t, wts["gdn_conv_w"], wts["gdn_conv_b"], wts["gdn_params"],
                   wts["gdn_norm_w"], s_gdn0, bsz, seq_len, want_state)
    ssd_out = _ssd(proj, small_t, wts["ssd_conv_w"], wts["ssd_conv_b"], wts["ssd_a_log"], wts["ssd_dt_bias"],
                   wts["ssd_d"], wts["ssd_norm_w"], h_ssd0t, bsz, seq_len, want_state)
    x1, h2, gates = _post(x2d, pos, gdn_out[0], ssd_out[0], proj, ada3, wts["w_gdn_out"], wts["w_ssd_out"],
                          wts["w_o"], wts["ln1_g"], wts["ln1_b"], wts["w_router"], wts["b_router"],
                          seq_len, ada_row0)
    y = _moe(h2, gates, x1, ada3, wts["w_exp_gate"], wts["w_exp_up"], wts["w_exp_down"], wts["ln2_g"],
             wts["ln2_b"], seq_len, ada_row0, pos is not None)
    states = (gdn_out[1], ssd_out[1]) if want_state else None
    return y.reshape(bsz, seq_len, D_MODEL), states


def kernel(x_prompt, x_sample, state_gdn, state_ssd, c, c_ctx, w_ada, b_ada, w_in, gdn_conv_w, gdn_conv_b, gdn_a_log, gdn_dt_bias, gdn_norm_w, w_gdn_out, ssd_conv_w, ssd_conv_b, ssd_a_log, ssd_dt_bias, ssd_d, ssd_norm_w, w_ssd_out, w_o, ln1_g, ln1_b, w_router_group, b_router_group, w_router_expert, b_router_expert, w_exp_gate, w_exp_up, w_exp_down, ln2_g, ln2_b):
    assert w_in.shape[0] == DEPTH == 1
    l = 0
    bsz_c = x_prompt.shape[0]
    bsz_l, seq_l, _ = x_sample.shape

    cvec = jnp.zeros((SUBLANES, D_MODEL), F32).at[0].set(c_ctx).at[1:1 + bsz_l].set(c)
    ada3 = _ada(cvec, w_ada[l], b_ada[l]).reshape(SUBLANES, 1, 6 * D_MODEL)

    wi = w_in[l]
    o_zg = 3072
    o_beta = 4096
    o_a = 4112
    o_xbc = 4128
    o_zs = 7200
    o_dt = 9248
    o_gate = 9312
    w_big = jnp.concatenate([wi[:, 0:o_zg], wi[:, o_zg:o_beta], wi[:, o_xbc:o_zs], wi[:, o_zs:o_dt],
                             wi[:, o_gate:]], axis=1).astype(BF16)
    w_beta = wi[:, o_beta:o_a].T.reshape(2, GDN_HEADS, D_MODEL)
    w_a = wi[:, o_a:o_xbc].T.reshape(2, GDN_HEADS, D_MODEL)
    w_head = jnp.concatenate([w_beta, w_a, jnp.zeros((SUBLANES - 4, GDN_HEADS, D_MODEL), F32)], axis=0)
    w_head = jnp.transpose(w_head, (1, 0, 2)).reshape(ROW_DT, D_MODEL)
    w_small_t = jnp.concatenate([w_head, wi[:, o_dt:o_gate].T], axis=0).astype(BF16)
    assert w_small_t.shape == (SMALL_ROWS, D_MODEL)
    gdn_params = jnp.concatenate([gdn_a_log[l], gdn_dt_bias[l], jnp.zeros((SUBLANES - 4, GDN_HEADS), F32)], axis=0)
    gdn_params = _lane_rep(gdn_params.T)

    w_router = jnp.zeros((D_MODEL, LANES), F32)
    w_router = w_router.at[:, 0:N_EXPERTS].set(w_router_expert[l]).at[:, N_EXPERTS:N_EXPERTS + N_GROUPS].set(
        w_router_group[l])
    b_router = jnp.zeros((1, LANES), F32)
    b_router = b_router.at[0, 0:N_EXPERTS].set(b_router_expert[l]).at[0, N_EXPERTS:N_EXPERTS + N_GROUPS].set(
        b_router_group[l])

    wts = {
        "w_big": w_big, "w_small_t": w_small_t,
        "gdn_conv_w": gdn_conv_w[l], "gdn_conv_b": gdn_conv_b[l].reshape(1, -1),
        "gdn_params": gdn_params,
        "gdn_norm_w": gdn_norm_w[l].reshape(1, -1),
        "ssd_conv_w": ssd_conv_w[l], "ssd_conv_b": ssd_conv_b[l].reshape(1, -1),
        "ssd_a_log": _lane_rep(ssd_a_log[l]), "ssd_dt_bias": _lane_rep(ssd_dt_bias[l]),
        "ssd_d": jnp.repeat(ssd_d[l], SSD_P).reshape(1, -1), "ssd_norm_w": ssd_norm_w[l].reshape(1, -1),
        "w_gdn_out": w_gdn_out[l].astype(BF16), "w_ssd_out": w_ssd_out[l].astype(BF16),
        "w_o": w_o[l].astype(BF16),
        "ln1_g": ln1_g[l].reshape(1, -1), "ln1_b": ln1_b[l].reshape(1, -1),
        "w_router": w_router, "b_router": b_router,
        "w_exp_gate": w_exp_gate[l].astype(BF16), "w_exp_up": w_exp_up[l].astype(BF16),
        "w_exp_down": w_exp_down[l].astype(BF16),
        "ln2_g": ln2_g[l].reshape(1, -1), "ln2_b": ln2_b[l].reshape(1, -1),
    }

    pos = _grid_pos_embed(seq_l, D_MODEL)
    h0 = state_ssd[:, l].reshape(bsz_l, 2, SSD_GROUPS, HEADS_PER_GROUP, SSD_P, SSD_N)
    h0t = jnp.transpose(h0, (0, 1, 2, 5, 3, 4)).reshape(bsz_l, 2, SSD_GROUPS, SSD_N, GROUP_W)

    y_ctx, (s_gdn, h_ssd_t) = _stream(x_prompt, None, ada3, 0, None, None, True, wts)
    y_lat, _ = _stream(x_sample, pos, ada3, 1, state_gdn, h0t, False, wts)

    h_ssd = h_ssd_t.reshape(bsz_c, 2, SSD_GROUPS, SSD_N, HEADS_PER_GROUP, SSD_P)
    new_ssd = jnp.transpose(h_ssd, (0, 1, 2, 4, 5, 3)).reshape(bsz_c, 1, 2, SSD_HEADS, SSD_P, SSD_N)
    return (y_ctx, y_lat, s_gdn, new_ssd)
```

```python
import functools
import math

import jax
import jax.numpy as jnp
from jax import lax
from jax.experimental import pallas as pl
from jax.experimental.pallas import tpu as pltpu

F32 = jnp.float32
BF16 = jnp.bfloat16
HIGHEST = lax.Precision.HIGHEST

D_MODEL = 1024
GRID_W = 64
POS_BASE = 10000.0
CONV_K = 5
GDN_HEADS = 8
GDN_DK = 128
GDN_DV = 128
SSD_HEADS = 32
SSD_P = 64
SSD_INNER = SSD_HEADS * SSD_P
SSD_GROUPS = 4
SSD_N = 128
HEADS_PER_GROUP = SSD_HEADS // SSD_GROUPS
GROUP_W = HEADS_PER_GROUP * SSD_P
N_GROUPS = 4
EXPERTS_PER_GROUP = 8
N_EXPERTS = N_GROUPS * EXPERTS_PER_GROUP
D_EXPERT = 256
EPS = 1e-6
DEPTH = 1
ALPHA = (2.0 * DEPTH) ** 0.25

LANES = 128
SUBLANES = 8
CHUNK = 128
CONV_PAD = SUBLANES
VMEM_LIMIT = 56 * 1024 * 1024

COL_QKV = 0
COL_ZG = 3072
COL_XBC = 4096
COL_ZS = 7168
COL_GATE = 9216
PROJ_COLS = 11264
PROJ_TN = 1024
GDN_ROW_BETA = 0
GDN_ROW_A = 2
ROW_DT = GDN_HEADS * SUBLANES
SMALL_ROWS = 128

TOK_BLOCK = 1024
POST_BLOCK = 256
MOE_WINDOW = 1024
MOE_TILE = 288
MOE_TILES = -(-MOE_WINDOW // MOE_TILE)
GROUP_LANE = N_EXPERTS


def _bdot(a, b):
    return jnp.dot(a.astype(BF16), b.astype(BF16), preferred_element_type=F32)


def _sigmoid(x):
    return 0.5 * jnp.tanh(0.5 * x) + 0.5


def _silu(x):
    return x * _sigmoid(x)


def _softplus(x):
    return jnp.maximum(x, 0.0) + jnp.log1p(jnp.exp(-jnp.abs(x)))


def _ln(x):
    mu = jnp.mean(x, axis=-1, keepdims=True)
    xc = x - mu
    var = jnp.mean(xc * xc, axis=-1, keepdims=True)
    return xc * lax.rsqrt(var + EPS)


def _lane_scan(x, reverse):
    lane = lax.broadcasted_iota(jnp.int32, x.shape, 1)
    s = 1
    while s < CHUNK:
        if reverse:
            shifted = pltpu.roll(x, CHUNK - s, axis=1)
            x = x + jnp.where(lane < CHUNK - s, shifted, 0.0)
        else:
            shifted = pltpu.roll(x, s, axis=1)
            x = x + jnp.where(lane >= s, shifted, 0.0)
        s *= 2
    return x


def _tri_masks(reverse):
    row = lax.broadcasted_iota(jnp.int32, (CHUNK, CHUNK), 0)
    col = lax.broadcasted_iota(jnp.int32, (CHUNK, CHUNK), 1)
    if reverse:
        return row <= col, row < col
    return row >= col, row > col


def _decay_matrix(acc_row, incl):
    acc_rb = jnp.broadcast_to(acc_row, (CHUNK, CHUNK))
    acc_col = acc_rb.T
    decay = jnp.exp(jnp.where(incl, acc_col - acc_rb, -jnp.inf))
    return decay, acc_col


def _conv_tap_sum(pad_ref, t0, lo, hi, w_ref, b_ref):
    acc = jnp.broadcast_to(b_ref[...], (CHUNK, hi - lo))
    for j in range(CONV_K):
        start = t0 + CONV_PAD - CONV_K // 2 + j
        acc = acc + pad_ref[start:start + CHUNK, lo:hi] * w_ref[j:j + 1, :]
    return acc


def _ada_kernel(c_ref, w_ref, b_ref, o_ref):
    s = _silu(c_ref[...])
    o_ref[...] = jnp.dot(s, w_ref[...], precision=HIGHEST, preferred_element_type=F32) + b_ref[...]


def _ada(cvec, w_ada, b_ada):
    n_out = w_ada.shape[1]
    tn = 1024
    return pl.pallas_call(
        _ada_kernel,
        grid=(n_out // tn,),
        in_specs=[
            pl.BlockSpec((SUBLANES, D_MODEL), lambda j: (0, 0)),
            pl.BlockSpec((D_MODEL, tn), lambda j: (0, j)),
            pl.BlockSpec((1, tn), lambda j: (0, j)),
        ],
        out_specs=pl.BlockSpec((SUBLANES, tn), lambda j: (0, j)),
        out_shape=jax.ShapeDtypeStruct((SUBLANES, n_out), F32),
        compiler_params=pltpu.CompilerParams(dimension_semantics=("arbitrary",), vmem_limit_bytes=VMEM_LIMIT),
        name="ada",
    )(cvec, w_ada, b_ada.reshape(1, n_out))


def _inproj_kernel(*refs, has_pos):
    if has_pos:
        x_ref, pos_ref, ada_ref, w_ref, wst_ref, proj_ref, small_ref, h_ref = refs
    else:
        x_ref, ada_ref, w_ref, wst_ref, proj_ref, small_ref, h_ref = refs
        pos_ref = None

    @pl.when(pl.program_id(1) == 0)
    def _():
        x = x_ref[...]
        if pos_ref is not None:
            x = x + pos_ref[...]
        shift = ada_ref[:, 0:D_MODEL]
        scale = ada_ref[:, D_MODEL:2 * D_MODEL]
        h = (_ln(x) * (1.0 + scale) + shift).astype(BF16)
        h_ref[...] = h
        small_ref[...] = lax.dot_general(wst_ref[...], h, (((1,), (1,)), ((), ())), preferred_element_type=F32)

    proj_ref[...] = jnp.dot(h_ref[...], w_ref[...], preferred_element_type=F32)


def _in_proj(x2d, pos, ada3, w_big, w_small_t, seq_len, ada_row0):
    n_tok = x2d.shape[0]
    t = TOK_BLOCK
    blocks_per_seq = seq_len // t
    has_pos = pos is not None

    def ada_map(i, j):
        return (ada_row0 + (i // blocks_per_seq if has_pos else 0), 0, 0)

    in_specs = [pl.BlockSpec((t, D_MODEL), lambda i, j: (i, 0))]
    args = [x2d]
    if has_pos:
        in_specs.append(pl.BlockSpec((t, D_MODEL), lambda i, j: (i % blocks_per_seq, 0)))
        args.append(pos)
    in_specs += [
        pl.BlockSpec((None, 1, 6 * D_MODEL), ada_map),
        pl.BlockSpec((D_MODEL, PROJ_TN), lambda i, j: (0, j)),
        pl.BlockSpec((SMALL_ROWS, D_MODEL), lambda i, j: (0, 0)),
    ]
    args += [ada3, w_big, w_small_t]
    return pl.pallas_call(
        functools.partial(_inproj_kernel, has_pos=has_pos),
        grid=(n_tok // t, PROJ_COLS // PROJ_TN),
        in_specs=in_specs,
        out_specs=[
            pl.BlockSpec((t, PROJ_TN), lambda i, j: (i, j)),
            pl.BlockSpec((SMALL_ROWS, t), lambda i, j: (0, i)),
        ],
        out_shape=[
            jax.ShapeDtypeStruct((n_tok, PROJ_COLS), F32),
            jax.ShapeDtypeStruct((SMALL_ROWS, n_tok), F32),
        ],
        scratch_shapes=[pltpu.VMEM((t, D_MODEL), BF16)],
        compiler_params=pltpu.CompilerParams(dimension_semantics=("arbitrary", "arbitrary"),
                                             vmem_limit_bytes=VMEM_LIMIT),
        name="in_proj",
    )(*args)


INV_BASE = 8
GDN_GROUP = 4


def _unit_tri_inverses(nmats):
    row = lax.broadcasted_iota(jnp.int32, (CHUNK, CHUNK), 0)
    col = lax.broadcasted_iota(jnp.int32, (CHUNK, CHUNK), 1)

    def same_block(size):
        shift = int(math.log2(size))
        return (row >> shift) == (col >> shift)

    eye = (row == col).astype(F32)
    base = same_block(INV_BASE)
    nds = [jnp.where(base, n, 0.0) for n in nmats]
    xs = [eye - nd for nd in nds]
    pws = [_bdot(nd, nd) for nd in nds]
    size = 2
    while True:
        xs = [x + _bdot(x, pw) for x, pw in zip(xs, pws)]
        size *= 2
        if size >= INV_BASE:
            break
        pws = [_bdot(pw, pw) for pw in pws]
    size = INV_BASE
    while size < CHUNK:
        pair, inner = same_block(2 * size), same_block(size)
        tmps = [_bdot(jnp.where(pair, jnp.where(inner, 0.0, n), 0.0), x) for n, x in zip(nmats, xs)]
        xs = [x - _bdot(x, t) for x, t in zip(xs, tmps)]
        size *= 2
    return xs


def _gdn_prepare(chunks, qn_ref, kn_ref, vn_ref, kt_ref, sc_ref, lhs_ref, sb_ref, ob_ref, eg_ref):
    grams = [_bdot(jnp.concatenate([kn_ref[g], qn_ref[g]], axis=0), kt_ref[g]) for g in chunks]
    units = []
    for g, gram in zip(chunks, grams):
        for d in range(2):
            reverse = d == 1
            incl, strict = _tri_masks(reverse)
            beta = sc_ref[g, d:d + 1, :]
            gc = sc_ref[g, 2 + d:3 + d, :]
            decay, gcol = _decay_matrix(gc, incl)
            beta_rb = jnp.broadcast_to(beta, (CHUNK, CHUNK))
            glast = gc[:, 0:1] if reverse else gc[:, CHUNK - 1:CHUNK]
            sc_ref[g, 4 + d:5 + d, :] = jnp.broadcast_to(jnp.exp(glast), (1, CHUNK))
            eg_ref[g, d] = jnp.exp(gcol)
            units.append(dict(
                g=g, d=d,
                nmat=jnp.where(strict, gram[0:CHUNK] * decay, 0.0) * beta_rb,
                att=gram[CHUNK:2 * CHUNK] * decay * beta_rb,
                kd_scale=beta * jnp.exp(glast - gc)))
    xinvs = _unit_tri_inverses([u["nmat"] for u in units])
    uws = [_bdot(x, jnp.concatenate([vn_ref[u["g"]], kn_ref[u["g"]] * eg_ref[u["g"], u["d"]]], axis=1))
           for u, x in zip(units, xinvs)]
    mixeds = []
    for u, uw in zip(units, uws):
        kd = kt_ref[u["g"]] * jnp.broadcast_to(u["kd_scale"], (CHUNK, CHUNK))
        mixeds.append(_bdot(jnp.concatenate([kd, u["att"]], axis=0), uw))
    for u, mixed in zip(units, mixeds):
        g, d = u["g"], u["d"]
        lhs_ref[g, d, 0:CHUNK, :] = mixed[0:CHUNK, GDN_DV:].astype(BF16)
        lhs_ref[g, d, CHUNK:2 * CHUNK, :] = qn_ref[g].astype(BF16)
        lhs_ref[g, d, 2 * CHUNK:3 * CHUNK, :] = mixed[CHUNK:, GDN_DV:].astype(BF16)
        sb_ref[g, d] = mixed[0:CHUNK, 0:GDN_DV]
        ob_ref[g, d] = mixed[CHUNK:, 0:GDN_DV]


def _gdn_steps(steps, lhs_ref, sb_ref, ob_ref, eg_ref, sc_ref, s_ref, o_refs):
    states = [s_ref[s, d] for _, s, d in steps]
    rs = [jnp.dot(lhs_ref[g, d], st.astype(BF16), preferred_element_type=F32)
          for (g, _, d), st in zip(steps, states)]
    for (g, s, d), st, r in zip(steps, states, rs):
        s_ref[s, d] = st * sc_ref[g, 4 + d:5 + d, :] - r[0:CHUNK] + sb_ref[g, d]
        o_refs[d][g] = eg_ref[g, d] * r[CHUNK:2 * CHUNK] - r[2 * CHUNK:3 * CHUNK] + ob_ref[g, d]


def _gdn_kernel(*refs, n_seq, n_chunks, has_s0, want_state):
    it = iter(refs)
    q_ref, k_ref, v_ref, z_ref, sm_ref = (next(it) for _ in range(5))
    wq_ref, wk_ref, wv_ref, bq_ref, bk_ref, bv_ref = (next(it) for _ in range(6))
    par_ref, nw_ref = (next(it) for _ in range(2))
    s0_ref = next(it) if has_s0 else None
    og_ref = next(it)
    sout_ref = next(it) if want_state else None
    pad_ref, qn_ref, kn_ref, vn_ref, kt_ref, of_ref, ob_ref, sc_ref, s_ref = (next(it) for _ in range(9))
    lhs_ref, sb_ref, ou_ref, eg_ref = (next(it) for _ in range(4))

    seq = n_chunks * CHUNK
    total = n_seq * n_chunks
    zero_rows = jnp.zeros((CONV_PAD, 3 * LANES), F32)
    for s in range(n_seq):
        pad_ref[s, 0:CONV_PAD, :] = zero_rows
        pad_ref[s, CONV_PAD + seq:2 * CONV_PAD + seq, :] = zero_rows
        for c in range(n_chunks):
            t0 = c * CHUNK
            r0 = s * seq + t0
            pad_ref[s, CONV_PAD + t0:CONV_PAD + t0 + CHUNK, 0:LANES] = q_ref[r0:r0 + CHUNK, :]
            pad_ref[s, CONV_PAD + t0:CONV_PAD + t0 + CHUNK, LANES:2 * LANES] = k_ref[r0:r0 + CHUNK, :]
            pad_ref[s, CONV_PAD + t0:CONV_PAD + t0 + CHUNK, 2 * LANES:3 * LANES] = v_ref[r0:r0 + CHUNK, :]

    a_neg = [-jnp.exp(par_ref[d:d + 1, :]) for d in range(2)]
    dt_bias = [par_ref[2 + d:3 + d, :] for d in range(2)]
    for s in range(n_seq):
        for c in range(n_chunks):
            t0 = c * CHUNK
            r0 = s * seq + t0
            g = s * n_chunks + c
            pad_s = pad_ref.at[s]
            qa = _silu(_conv_tap_sum(pad_s, t0, 0, LANES, wq_ref, bq_ref))
            ka = _silu(_conv_tap_sum(pad_s, t0, LANES, 2 * LANES, wk_ref, bk_ref))
            va = _silu(_conv_tap_sum(pad_s, t0, 2 * LANES, 3 * LANES, wv_ref, bv_ref))
            qn = qa * lax.rsqrt(jnp.sum(qa * qa, axis=-1, keepdims=True) + EPS) * (GDN_DK ** -0.5)
            kn = ka * lax.rsqrt(jnp.sum(ka * ka, axis=-1, keepdims=True) + EPS)
            qn_ref[g] = qn
            kn_ref[g] = kn
            vn_ref[g] = va
            kt_ref[g] = kn.T
            for d in range(2):
                braw = sm_ref[GDN_ROW_BETA + d:GDN_ROW_BETA + d + 1, r0:r0 + CHUNK]
                araw = sm_ref[GDN_ROW_A + d:GDN_ROW_A + d + 1, r0:r0 + CHUNK]
                gdec = a_neg[d] * _softplus(araw + dt_bias[d])
                sc_ref[g, d:d + 1, :] = _sigmoid(braw)
                sc_ref[g, 2 + d:3 + d, :] = _lane_scan(gdec, reverse=d == 1)

    def prepare(i, carry):
        _gdn_prepare([i * GDN_GROUP + j for j in range(GDN_GROUP)], qn_ref, kn_ref, vn_ref, kt_ref, sc_ref,
                     lhs_ref, sb_ref, ou_ref, eg_ref)
        return carry

    lax.fori_loop(0, total // GDN_GROUP, prepare, 0)

    if has_s0:
        s_ref[...] = s0_ref[...]
    else:
        s_ref[...] = jnp.zeros(s_ref.shape, F32)

    def advance(i, carry):
        steps = []
        for s in range(n_seq):
            steps.append((s * n_chunks + i, s, 0))
            steps.append((s * n_chunks + n_chunks - 1 - i, s, 1))
        _gdn_steps(steps, lhs_ref, sb_ref, ou_ref, eg_ref, sc_ref, s_ref, (of_ref, ob_ref))
        return carry

    lax.fori_loop(0, n_chunks, advance, 0)

    if want_state:
        sout_ref[...] = s_ref[...]
    for g in range(total):
        r0 = g * CHUNK
        o = of_ref[g] + ob_ref[g]
        o = o * lax.rsqrt(jnp.mean(o * o, axis=-1, keepdims=True) + EPS)
        og_ref[r0:r0 + CHUNK, :] = (o * nw_ref[...] * _silu(z_ref[r0:r0 + CHUNK, :])).astype(BF16)


def _gdn(proj, small_t, conv_w, conv_b, head_params, norm_w, s0, bsz, seq_len, want_state):
    n_chunks = seq_len // CHUNK
    n_tok = bsz * seq_len
    has_s0 = s0 is not None
    n_seq = max(1, GDN_GROUP * 2 // n_chunks)
    assert bsz % n_seq == 0 and (n_seq * n_chunks) % GDN_GROUP == 0
    total = n_seq * n_chunks
    rows = n_seq * seq_len
    col = lambda off: (lambda b, h: (b, off + h))
    cw = lambda off: (lambda b, h: (0, off + h))
    in_specs = [
        pl.BlockSpec((rows, LANES), col(COL_QKV // LANES)),
        pl.BlockSpec((rows, LANES), col(COL_QKV // LANES + GDN_HEADS)),
        pl.BlockSpec((rows, LANES), col(COL_QKV // LANES + 2 * GDN_HEADS)),
        pl.BlockSpec((rows, LANES), col(COL_ZG // LANES)),
        pl.BlockSpec((SUBLANES, rows), lambda b, h: (h, b)),
        pl.BlockSpec((CONV_K, LANES), cw(0)),
        pl.BlockSpec((CONV_K, LANES), cw(GDN_HEADS)),
        pl.BlockSpec((CONV_K, LANES), cw(2 * GDN_HEADS)),
        pl.BlockSpec((1, LANES), cw(0)),
        pl.BlockSpec((1, LANES), cw(GDN_HEADS)),
        pl.BlockSpec((1, LANES), cw(2 * GDN_HEADS)),
        pl.BlockSpec((None, SUBLANES, LANES), lambda b, h: (h, 0, 0)),
        pl.BlockSpec((1, LANES), lambda b, h: (0, 0)),
    ]
    args = [proj, proj, proj, proj, small_t, conv_w, conv_w, conv_w, conv_b, conv_b, conv_b,
            head_params, norm_w]
    state_spec = pl.BlockSpec((n_seq, None, 2, None, GDN_DK, GDN_DV), lambda b, h: (b, 0, 0, h, 0, 0))
    if has_s0:
        in_specs.append(state_spec)
        args.append(s0)
    out_specs = [pl.BlockSpec((rows, LANES), lambda b, h: (b, h))]
    out_shape = [jax.ShapeDtypeStruct((n_tok, GDN_HEADS * GDN_DV), BF16)]
    if want_state:
        out_specs.append(state_spec)
        out_shape.append(jax.ShapeDtypeStruct((bsz, 1, 2, GDN_HEADS, GDN_DK, GDN_DV), F32))
    chunked = pltpu.VMEM((total, CHUNK, LANES), F32)
    per_dir = pltpu.VMEM((total, 2, CHUNK, LANES), F32)
    return pl.pallas_call(
        functools.partial(_gdn_kernel, n_seq=n_seq, n_chunks=n_chunks, has_s0=has_s0, want_state=want_state),
        grid=(bsz // n_seq, GDN_HEADS),
        in_specs=in_specs,
        out_specs=out_specs,
        out_shape=out_shape,
        scratch_shapes=[
            pltpu.VMEM((n_seq, seq_len + 2 * CONV_PAD, 3 * LANES), F32),
            chunked, chunked, chunked, chunked, chunked, chunked,
            pltpu.VMEM((total, SUBLANES, CHUNK), F32),
            pltpu.VMEM((n_seq, 2, GDN_DK, GDN_DV), F32),
            pltpu.VMEM((total, 2, 3 * CHUNK, GDN_DK), BF16),
            per_dir, per_dir, per_dir,
        ],
        compiler_params=pltpu.CompilerParams(dimension_semantics=("arbitrary", "arbitrary"),
                                             vmem_limit_bytes=VMEM_LIMIT),
        name="gdn",
    )(*args)


def _ssd_steps(steps, xs_ref, cs_ref, bt_ref, dt_ref, ac_ref, ht_ref, y_ref):
    lane = lax.broadcasted_iota(jnp.int32, (CHUNK, LANES), 1)
    low_half = lane < SSD_P
    shared = [(_bdot(cs_ref[c], bt_ref[c]), _bdot(cs_ref[c], ht_ref[d])) for c, d in steps]
    pairs_per_stage = 2
    for p0 in range(0, HEADS_PER_GROUP // 2, pairs_per_stage):
        units = []
        for (c, d), (cb, y_off) in zip(steps, shared):
            reverse = d == 1
            incl, _ = _tri_masks(reverse)
            bt = bt_ref[c]
            for p in range(p0, p0 + pairs_per_stage):
                halves = []
                for r in (2 * p, 2 * p + 1):
                    row = d * HEADS_PER_GROUP + r
                    dt = dt_ref[c, row:row + 1, :]
                    ac = ac_ref[c, row:row + 1, :]
                    decay, acol = _decay_matrix(ac, incl)
                    last = ac[:, 0:1] if reverse else ac[:, CHUNK - 1:CHUNK]
                    halves.append(dict(
                        m=(cb * decay * jnp.broadcast_to(dt, (CHUNK, CHUNK))).astype(BF16),
                        bts=(bt * jnp.broadcast_to(dt * jnp.exp(last - ac), (SSD_N, CHUNK))).astype(BF16),
                        acol=acol, elast=jnp.exp(last)))
                units.append(dict(c=c, d=d, p=p, halves=halves, y_off=y_off[:, p * LANES:(p + 1) * LANES]))
        for u in units:
            x = xs_ref[u["c"], :, u["p"] * LANES:(u["p"] + 1) * LANES]
            zero = jnp.zeros_like(x)
            xh = (jnp.where(low_half, x, zero), jnp.where(low_half, zero, x))
            u["yd"] = [jnp.dot(hf["m"], xm, preferred_element_type=F32) for hf, xm in zip(u["halves"], xh)]
            u["st"] = [jnp.dot(hf["bts"], xm, preferred_element_type=F32) for hf, xm in zip(u["halves"], xh)]
        for u in units:
            c, d, p = u["c"], u["d"], u["p"]
            h0, h1 = u["halves"]
            eoff = jnp.exp(jnp.where(low_half, h0["acol"], h1["acol"]))
            elast = jnp.where(low_half, h0["elast"], h1["elast"])
            cols = slice(p * LANES, (p + 1) * LANES)
            y_ref[c, :, cols] = y_ref[c, :, cols] + (u["yd"][0] + u["yd"][1]) + eoff * u["y_off"]
            ht_ref[d, :, cols] = ht_ref[d, :, cols] * elast + (u["st"][0] + u["st"][1])


def _ssd_kernel(*refs, n_chunks, has_h0, want_state):
    it = iter(refs)
    x_ref, b_ref, c_ref, z_ref, dtf_ref, dtb_ref = (next(it) for _ in range(6))
    wx_ref, wb_ref, wc_ref, bx_ref, bb_ref, bc_ref = (next(it) for _ in range(6))
    alog_ref, dtbias_ref, dvec_ref, nw_ref = (next(it) for _ in range(4))
    h0_ref = next(it) if has_h0 else None
    yg_ref = next(it)
    hout_ref = next(it) if want_state else None
    pad_ref, xs_ref, cs_ref, bt_ref, dt_ref, ac_ref, y_ref, ht_ref = (next(it) for _ in range(8))

    seq = n_chunks * CHUNK
    width = GROUP_W + 2 * SSD_N
    zero_rows = jnp.zeros((CONV_PAD, width), F32)
    pad_ref[0:CONV_PAD, :] = zero_rows
    pad_ref[CONV_PAD + seq:2 * CONV_PAD + seq, :] = zero_rows
    for c in range(n_chunks):
        t0 = c * CHUNK
        pad_ref[CONV_PAD + t0:CONV_PAD + t0 + CHUNK, 0:GROUP_W] = x_ref[t0:t0 + CHUNK, :]
        pad_ref[CONV_PAD + t0:CONV_PAD + t0 + CHUNK, GROUP_W:GROUP_W + SSD_N] = b_ref[t0:t0 + CHUNK, :]
        pad_ref[CONV_PAD + t0:CONV_PAD + t0 + CHUNK, GROUP_W + SSD_N:width] = c_ref[t0:t0 + CHUNK, :]

    raw_dt = (dtf_ref, dtb_ref)
    for c in range(n_chunks):
        t0 = c * CHUNK
        for lo in range(0, GROUP_W, LANES):
            cols = slice(lo, lo + LANES)
            xa = _silu(_conv_tap_sum(pad_ref, t0, lo, lo + LANES, wx_ref.at[:, cols], bx_ref.at[:, cols]))
            xs_ref[c, :, cols] = xa.astype(BF16)
            y_ref[c, :, cols] = xa * dvec_ref[:, cols]
        ba = _silu(_conv_tap_sum(pad_ref, t0, GROUP_W, GROUP_W + SSD_N, wb_ref, bb_ref))
        ca = _silu(_conv_tap_sum(pad_ref, t0, GROUP_W + SSD_N, width, wc_ref, bc_ref))
        cs_ref[c] = ca
        bt_ref[c] = ba.T
        for d in range(2):
            dt = _softplus(raw_dt[d][:, t0:t0 + CHUNK] + dtbias_ref[d])
            lo, hi = d * HEADS_PER_GROUP, (d + 1) * HEADS_PER_GROUP
            dt_ref[c, lo:hi, :] = dt
            ac_ref[c, lo:hi, :] = _lane_scan(dt * (-jnp.exp(alog_ref[d])), reverse=d == 1)

    if has_h0:
        ht_ref[...] = h0_ref[...]
    else:
        ht_ref[...] = jnp.zeros(ht_ref.shape, F32)

    def body(i, carry):
        _ssd_steps([(i, 0), (n_chunks - 1 - i, 1)], xs_ref, cs_ref, bt_ref, dt_ref, ac_ref, ht_ref, y_ref)
        return carry

    lax.fori_loop(0, n_chunks, body, 0)

    if want_state:
        hout_ref[...] = ht_ref[...]
    for c in range(n_chunks):
        t0 = c * CHUNK
        ssq = jnp.zeros((CHUNK, 1), F32)
        for lo in range(0, GROUP_W, LANES):
            cols = slice(lo, lo + LANES)
            y = y_ref[c, :, cols] * _silu(z_ref[t0:t0 + CHUNK, cols])
            y_ref[c, :, cols] = y
            ssq = ssq + jnp.sum(y * y, axis=-1, keepdims=True)
        inv = lax.rsqrt(ssq * (1.0 / GROUP_W) + EPS)
        for lo in range(0, GROUP_W, LANES):
            cols = slice(lo, lo + LANES)
            yg_ref[t0:t0 + CHUNK, cols] = (y_ref[c, :, cols] * inv * nw_ref[:, cols]).astype(BF16)


def _ssd(proj, small_t, conv_w, conv_b, a_log_rep, dt_bias_rep, d_vec, norm_w, h0t, bsz, seq_len, want_state):
    n_chunks = seq_len // CHUNK
    n_tok = bsz * seq_len
    has_h0 = h0t is not None
    xbc_w = COL_XBC // GROUP_W
    bc_l = (COL_XBC + SSD_INNER) // LANES
    in_specs = [
        pl.BlockSpec((seq_len, GROUP_W), lambda b, g: (b, xbc_w + g)),
        pl.BlockSpec((seq_len, SSD_N), lambda b, g: (b, bc_l + g)),
        pl.BlockSpec((seq_len, SSD_N), lambda b, g: (b, bc_l + SSD_GROUPS + g)),
        pl.BlockSpec((seq_len, GROUP_W), lambda b, g: (b, COL_ZS // GROUP_W + g)),
        pl.BlockSpec((HEADS_PER_GROUP, seq_len), lambda b, g: (ROW_DT // HEADS_PER_GROUP + g, b)),
        pl.BlockSpec((HEADS_PER_GROUP, seq_len), lambda b, g: ((ROW_DT + SSD_HEADS) // HEADS_PER_GROUP + g, b)),
        pl.BlockSpec((CONV_K, GROUP_W), lambda b, g: (0, g)),
        pl.BlockSpec((CONV_K, SSD_N), lambda b, g: (0, SSD_INNER // SSD_N + g)),
        pl.BlockSpec((CONV_K, SSD_N), lambda b, g: (0, SSD_INNER // SSD_N + SSD_GROUPS + g)),
        pl.BlockSpec((1, GROUP_W), lambda b, g: (0, g)),
        pl.BlockSpec((1, SSD_N), lambda b, g: (0, SSD_INNER // SSD_N + g)),
        pl.BlockSpec((1, SSD_N), lambda b, g: (0, SSD_INNER // SSD_N + SSD_GROUPS + g)),
        pl.BlockSpec((2, HEADS_PER_GROUP, LANES), lambda b, g: (0, g, 0)),
        pl.BlockSpec((2, HEADS_PER_GROUP, LANES), lambda b, g: (0, g, 0)),
        pl.BlockSpec((1, GROUP_W), lambda b, g: (0, g)),
        pl.BlockSpec((1, GROUP_W), lambda b, g: (0, g)),
    ]
    args = [proj, proj, proj, proj, small_t, small_t, conv_w, conv_w, conv_w, conv_b, conv_b, conv_b,
            a_log_rep, dt_bias_rep, d_vec, norm_w]
    state_spec = pl.BlockSpec((None, 2, None, SSD_N, GROUP_W), lambda b, g: (b, 0, g, 0, 0))
    if has_h0:
        in_specs.append(state_spec)
        args.append(h0t)
    out_specs = [pl.BlockSpec((seq_len, GROUP_W), lambda b, g: (b, g))]
    out_shape = [jax.ShapeDtypeStruct((n_tok, SSD_INNER), BF16)]
    if want_state:
        out_specs.append(state_spec)
        out_shape.append(jax.ShapeDtypeStruct((bsz, 2, SSD_GROUPS, SSD_N, GROUP_W), F32))
    rows = pltpu.VMEM((n_chunks, 2 * HEADS_PER_GROUP, CHUNK), F32)
    return pl.pallas_call(
        functools.partial(_ssd_kernel, n_chunks=n_chunks, has_h0=has_h0, want_state=want_state),
        grid=(bsz, SSD_GROUPS),
        in_specs=in_specs,
        out_specs=out_specs,
        out_shape=out_shape,
        scratch_shapes=[
            pltpu.VMEM((seq_len + 2 * CONV_PAD, GROUP_W + 2 * SSD_N), F32),
            pltpu.VMEM((n_chunks, CHUNK, GROUP_W), BF16),
            pltpu.VMEM((n_chunks, CHUNK, SSD_N), F32),
            pltpu.VMEM((n_chunks, SSD_N, CHUNK), F32),
            rows, rows,
            pltpu.VMEM((n_chunks, CHUNK, GROUP_W), F32),
            pltpu.VMEM((2, SSD_N, GROUP_W), F32),
        ],
        compiler_params=pltpu.CompilerParams(dimension_semantics=("arbitrary", "arbitrary"),
                                             vmem_limit_bytes=VMEM_LIMIT),
        name="ssd",
    )(*args)


def _route(logits):
    lane = lax.broadcasted_iota(jnp.int32, logits.shape, 1)
    neg = -jnp.inf
    is_grp = (lane >= N_EXPERTS) & (lane < N_EXPERTS + N_GROUPS)
    gl = jnp.where(is_grp, logits, neg)
    gmax = jnp.max(gl, axis=-1, keepdims=True)
    ge = jnp.exp(gl - gmax)
    p_grp = ge / jnp.sum(ge, axis=-1, keepdims=True)
    p_top = jnp.max(p_grp, axis=-1, keepdims=True)
    g_idx = jnp.min(jnp.where(is_grp & (p_grp == p_top), lane, 2 * LANES), axis=-1, keepdims=True) - N_EXPERTS
    in_grp = (lane >= g_idx * EXPERTS_PER_GROUP) & (lane < (g_idx + 1) * EXPERTS_PER_GROUP)
    el = jnp.where(in_grp, logits, neg)
    emax = jnp.max(el, axis=-1, keepdims=True)
    ee = jnp.exp(el - emax)
    p_e = ee / jnp.sum(ee, axis=-1, keepdims=True)
    w1 = jnp.max(p_e, axis=-1, keepdims=True)
    i1 = jnp.min(jnp.where(in_grp & (p_e == w1), lane, 2 * LANES), axis=-1, keepdims=True)
    rest = jnp.where(in_grp & (lane != i1), p_e, -1.0)
    w2 = jnp.max(rest, axis=-1, keepdims=True)
    i2 = jnp.min(jnp.where(rest == w2, lane, 2 * LANES), axis=-1, keepdims=True)
    tot = w1 + w2
    gates = jnp.where(lane == i1, w1 / tot * p_top, 0.0) + jnp.where(lane == i2, w2 / tot * p_top, 0.0)
    return jnp.where(lane == GROUP_LANE, g_idx.astype(F32), gates)


def _post_kernel(*refs, has_pos):
    it = iter(refs)
    x_ref = next(it)
    pos_ref = next(it) if has_pos else None
    og_ref, yg_ref, gg_ref, gs_ref, ada_ref = (next(it) for _ in range(5))
    wg_ref, ws_ref, wo_ref, g1_ref, b1_ref, wr_ref, br_ref = (next(it) for _ in range(7))
    x1_ref, h2_ref, gates_ref = (next(it) for _ in range(3))

    x = x_ref[...]
    if has_pos:
        x = x + pos_ref[...]
    u_g = jnp.dot(og_ref[...], wg_ref[...], preferred_element_type=F32)
    u_s = jnp.dot(yg_ref[...], ws_ref[...], preferred_element_type=F32)
    m = _sigmoid(gg_ref[...]) * u_g + _sigmoid(gs_ref[...]) * u_s
    mix = jnp.dot(m.astype(BF16), wo_ref[...], preferred_element_type=F32)
    gate1 = ada_ref[:, 2 * D_MODEL:3 * D_MODEL]
    shift2 = ada_ref[:, 3 * D_MODEL:4 * D_MODEL]
    scale2 = ada_ref[:, 4 * D_MODEL:5 * D_MODEL]
    x1 = _ln(ALPHA * x + gate1 * mix) * g1_ref[...] + b1_ref[...]
    x1_ref[...] = x1
    h2 = _ln(x1) * (1.0 + scale2) + shift2
    h2b = h2.astype(BF16)
    h2_ref[...] = h2b
    logits = jnp.dot(h2b, wr_ref[...], preferred_element_type=F32) + br_ref[...]
    gates_ref[...] = _route(logits)


def _post(x2d, pos, og, yg, proj, ada3, w_gdn_out, w_ssd_out, w_o, ln_g, ln_b, w_router, b_router,
          seq_len, ada_row0):
    n_tok = x2d.shape[0]
    t = POST_BLOCK
    blocks_per_seq = seq_len // t
    has_pos = pos is not None

    def ada_map(i):
        return (ada_row0 + (i // blocks_per_seq if has_pos else 0), 0, 0)

    const = lambda i: (0, 0)
    in_specs = [pl.BlockSpec((t, D_MODEL), lambda i: (i, 0))]
    args = [x2d]
    if has_pos:
        in_specs.append(pl.BlockSpec((t, D_MODEL), lambda i: (i % blocks_per_seq, 0)))
        args.append(pos)
    in_specs += [
        pl.BlockSpec((t, GDN_HEADS * GDN_DV), lambda i: (i, 0)),
        pl.BlockSpec((t, SSD_INNER), lambda i: (i, 0)),
        pl.BlockSpec((t, D_MODEL), lambda i: (i, COL_GATE // D_MODEL)),
        pl.BlockSpec((t, D_MODEL), lambda i: (i, COL_GATE // D_MODEL + 1)),
        pl.BlockSpec((None, 1, 6 * D_MODEL), ada_map),
        pl.BlockSpec((GDN_HEADS * GDN_DV, D_MODEL), const),
        pl.BlockSpec((SSD_INNER, D_MODEL), const),
        pl.BlockSpec((D_MODEL, D_MODEL), const),
        pl.BlockSpec((1, D_MODEL), const),
        pl.BlockSpec((1, D_MODEL), const),
        pl.BlockSpec((D_MODEL, LANES), const),
        pl.BlockSpec((1, LANES), const),
    ]
    args += [og, yg, proj, proj, ada3, w_gdn_out, w_ssd_out, w_o, ln_g, ln_b, w_router, b_router]
    return pl.pallas_call(
        functools.partial(_post_kernel, has_pos=has_pos),
        grid=(n_tok // t,),
        in_specs=in_specs,
        out_specs=[
            pl.BlockSpec((t, D_MODEL), lambda i: (i, 0)),
            pl.BlockSpec((t, D_MODEL), lambda i: (i, 0)),
            pl.BlockSpec((t, LANES), lambda i: (i, 0)),
        ],
        out_shape=[
            jax.ShapeDtypeStruct((n_tok, D_MODEL), F32),
            jax.ShapeDtypeStruct((n_tok, D_MODEL), BF16),
            jax.ShapeDtypeStruct((n_tok, LANES), F32),
        ],
        compiler_params=pltpu.CompilerParams(dimension_semantics=("arbitrary",), vmem_limit_bytes=VMEM_LIMIT),
        name="post",
    )(*args)


def _moe_kernel(h_ref, gates_ref, x1_ref, ada_ref, wg_ref, wu_ref, wd_ref, g2_ref, b2_ref, out_ref,
                col_ref, row_ref, gx_ref, cnt_ref):
    g = pl.program_id(1)
    t = MOE_WINDOW
    lane = lax.broadcasted_iota(jnp.int32, (t, LANES), 1)

    @pl.when(g == 0)
    def _():
        gates = gates_ref[...]
        grp = jnp.sum(jnp.where(lane == GROUP_LANE, gates, 0.0), axis=-1, keepdims=True)
        onehot = jnp.where((lane < N_GROUPS) & (grp == lane.astype(F32)), 1.0, 0.0)
        tri = (lax.broadcasted_iota(jnp.int32, (t, t), 0) >= lax.broadcasted_iota(jnp.int32, (t, t), 1))
        cum = jnp.dot(jnp.where(tri, 1.0, 0.0).astype(BF16), onehot.astype(BF16), preferred_element_type=F32)
        rank = jnp.sum(onehot * cum, axis=-1, keepdims=True) - 1.0
        info = jnp.where(lane == 0, grp, jnp.where(lane == 1, rank, 0.0))
        col_ref[...] = info
        row_ref[...] = info.T[0:SUBLANES, :]
        totals = cum[t - 1:t, :]
        for k in range(N_GROUPS):
            cnt_ref[k] = totals[0, k].astype(jnp.int32)
        hi = gates.astype(BF16).astype(F32)
        mid = (gates - hi).astype(BF16).astype(F32)
        low = gates - hi - mid
        packed = jnp.where(lane < N_EXPERTS, hi,
                           jnp.where(lane < 2 * N_EXPERTS, pltpu.roll(mid, N_EXPERTS, axis=1),
                                     jnp.where(lane < 3 * N_EXPERTS, pltpu.roll(low, 2 * N_EXPERTS, axis=1), 0.0)))
        gx_ref[...] = packed.astype(BF16)
        out_ref[...] = jnp.zeros(out_ref.shape, F32)

    n_g = cnt_ref[g]
    gf = g.astype(F32)
    for j in range(MOE_TILES):
        @pl.when(j * MOE_TILE < n_g)
        def _(j=j):
            slot_r = (lax.broadcasted_iota(jnp.int32, (MOE_TILE, t), 0) + j * MOE_TILE).astype(F32)
            pick = jnp.where((row_ref[1:2, :] == slot_r) & (row_ref[0:1, :] == gf), 1.0, 0.0).astype(BF16)
            hs = jnp.dot(pick, h_ref[...], preferred_element_type=F32).astype(BF16)
            gsx = jnp.dot(pick, gx_ref[...], preferred_element_type=F32)
            gs = gsx + pltpu.roll(gsx, LANES - N_EXPERTS, axis=1) + pltpu.roll(gsx, LANES - 2 * N_EXPERTS, axis=1)
            lane_t = lax.broadcasted_iota(jnp.int32, (MOE_TILE, LANES), 1)
            y = jnp.zeros((MOE_TILE, D_MODEL), F32)
            for e in range(EXPERTS_PER_GROUP):
                cols = slice(e * D_EXPERT, (e + 1) * D_EXPERT)
                a = jnp.dot(hs, wg_ref[:, cols], preferred_element_type=F32)
                b = jnp.dot(hs, wu_ref[:, cols], preferred_element_type=F32)
                gate_e = jnp.sum(jnp.where(lane_t == g * EXPERTS_PER_GROUP + e, gs, 0.0), axis=-1, keepdims=True)
                y = y + jnp.dot((_silu(a) * b * gate_e).astype(BF16), wd_ref[cols, :], preferred_element_type=F32)
            slot_c = (lax.broadcasted_iota(jnp.int32, (t, MOE_TILE), 1) + j * MOE_TILE).astype(F32)
            place = jnp.where((col_ref[:, 1:2] == slot_c) & (col_ref[:, 0:1] == gf), 1.0, 0.0).astype(BF16)
            out_ref[...] += jnp.dot(place, y.astype(BF16), preferred_element_type=F32)

    @pl.when(g == N_GROUPS - 1)
    def _():
        gate2 = ada_ref[:, 5 * D_MODEL:6 * D_MODEL]
        out_ref[...] = _ln(ALPHA * x1_ref[...] + gate2 * out_ref[...]) * g2_ref[...] + b2_ref[...]


def _moe(h2, gates, x1, ada3, w_gate, w_up, w_down, ln_g, ln_b, seq_len, ada_row0, per_seq_ada):
    n_tok = h2.shape[0]
    t = MOE_WINDOW
    blocks_per_seq = seq_len // t
    assert not per_seq_ada or seq_len % t == 0
    group_w = EXPERTS_PER_GROUP * D_EXPERT

    def ada_map(i, g):
        return (ada_row0 + (i // blocks_per_seq if per_seq_ada else 0), 0, 0)

    return pl.pallas_call(
        _moe_kernel,
        grid=(n_tok // t, N_GROUPS),
        in_specs=[
            pl.BlockSpec((t, D_MODEL), lambda i, g: (i, 0)),
            pl.BlockSpec((t, LANES), lambda i, g: (i, 0)),
            pl.BlockSpec((t, D_MODEL), lambda i, g: (i, 0), pipeline_mode=pl.Buffered(1)),
            pl.BlockSpec((None, 1, 6 * D_MODEL), ada_map),
            pl.BlockSpec((None, D_MODEL, group_w), lambda i, g: (g, 0, 0)),
            pl.BlockSpec((None, D_MODEL, group_w), lambda i, g: (g, 0, 0)),
            pl.BlockSpec((None, group_w, D_MODEL), lambda i, g: (g, 0, 0)),
            pl.BlockSpec((1, D_MODEL), lambda i, g: (0, 0)),
            pl.BlockSpec((1, D_MODEL), lambda i, g: (0, 0)),
        ],
        out_specs=pl.BlockSpec((t, D_MODEL), lambda i, g: (i, 0)),
        out_shape=jax.ShapeDtypeStruct((n_tok, D_MODEL), F32),
        scratch_shapes=[
            pltpu.VMEM((t, LANES), F32),
            pltpu.VMEM((SUBLANES, t), F32),
            pltpu.VMEM((t, LANES), BF16),
            pltpu.SMEM((N_GROUPS,), jnp.int32),
        ],
        compiler_params=pltpu.CompilerParams(dimension_semantics=("arbitrary", "arbitrary"),
                                             vmem_limit_bytes=VMEM_LIMIT),
        name="moe",
    )(h2, gates, x1, ada3, w_gate, w_up, w_down, ln_g, ln_b)


def _grid_pos_embed(n_tokens, d):
    rows = n_tokens // GRID_W
    rr, cc = jnp.meshgrid(jnp.arange(rows, dtype=F32), jnp.arange(GRID_W, dtype=F32), indexing="ij")
    quarter = d // 4
    freqs = jnp.exp(-math.log(POS_BASE) * jnp.arange(quarter, dtype=F32) / quarter)
    ang_r = rr.reshape(-1, 1) * freqs
    ang_c = cc.reshape(-1, 1) * freqs
    return jnp.concatenate([jnp.sin(ang_r), jnp.cos(ang_r), jnp.sin(ang_c), jnp.cos(ang_c)], axis=-1)


def _group_columns(w):
    w = w.reshape(N_GROUPS, EXPERTS_PER_GROUP, D_MODEL, D_EXPERT)
    return jnp.transpose(w, (0, 2, 1, 3)).reshape(N_GROUPS, D_MODEL, EXPERTS_PER_GROUP * D_EXPERT).astype(BF16)


def _lane_rep(v):
    return jnp.broadcast_to(v[..., None], v.shape + (LANES,)).astype(F32)


def _stream(x3d, pos, ada3, ada_row0, s_gdn0, h_ssd0t, want_state, wts):
    bsz, seq_len, _ = x3d.shape
    x2d = x3d.reshape(bsz * seq_len, D_MODEL)
    proj, small_t = _in_proj(x2d, pos, ada3, wts["w_big"], wts["w_small_t"], seq_len, ada_row0)
    gdn_out = _gdn(proj, small_t, wts["gdn_conv_w"], wts["gdn_conv_b"], wts["gdn_params"],
                   wts["gdn_norm_w"], s_gdn0, bsz, seq_len, want_state)
    ssd_out = _ssd(proj, small_t, wts["ssd_conv_w"], wts["ssd_conv_b"], wts["ssd_a_log"], wts["ssd_dt_bias"],
                   wts["ssd_d"], wts["ssd_norm_w"], h_ssd0t, bsz, seq_len, want_state)
    x1, h2, gates = _post(x2d, pos, gdn_out[0], ssd_out[0], proj, ada3, wts["w_gdn_out"], wts["w_ssd_out"],
                          wts["w_o"], wts["ln1_g"], wts["ln1_b"], wts["w_router"], wts["b_router"],
                          seq_len, ada_row0)
    y = _moe(h2, gates, x1, ada3, wts["w_exp_gate"], wts["w_exp_up"], wts["w_exp_down"], wts["ln2_g"],
             wts["ln2_b"], seq_len, ada_row0, pos is not None)
    states = (gdn_out[1], ssd_out[1]) if want_state else None
    return y.reshape(bsz, seq_len, D_MODEL), states


def kernel(x_prompt, x_sample, state_gdn, state_ssd, c, c_ctx, w_ada, b_ada, w_in, gdn_conv_w, gdn_conv_b, gdn_a_log, gdn_dt_bias, gdn_norm_w, w_gdn_out, ssd_conv_w, ssd_conv_b, ssd_a_log, ssd_dt_bias, ssd_d, ssd_norm_w, w_ssd_out, w_o, ln1_g, ln1_b, w_router_group, b_router_group, w_router_expert, b_router_expert, w_exp_gate, w_exp_up, w_exp_down, ln2_g, ln2_b):
    assert w_in.shape[0] == DEPTH == 1
    l = 0
    bsz_c = x_prompt.shape[0]
    bsz_l, seq_l, _ = x_sample.shape

    cvec = jnp.zeros((SUBLANES, D_MODEL), F32).at[0].set(c_ctx).at[1:1 + bsz_l].set(c)
    ada3 = _ada(cvec, w_ada[l], b_ada[l]).reshape(SUBLANES, 1, 6 * D_MODEL)

    wi = w_in[l]
    o_zg = 3072
    o_beta = 4096
    o_a = 4112
    o_xbc = 4128
    o_zs = 7200
    o_dt = 9248
    o_gate = 9312
    w_big = jnp.concatenate([wi[:, 0:o_zg], wi[:, o_zg:o_beta], wi[:, o_xbc:o_zs], wi[:, o_zs:o_dt],
                             wi[:, o_gate:]], axis=1).astype(BF16)
    w_beta = wi[:, o_beta:o_a].T.reshape(2, GDN_HEADS, D_MODEL)
    w_a = wi[:, o_a:o_xbc].T.reshape(2, GDN_HEADS, D_MODEL)
    w_head = jnp.concatenate([w_beta, w_a, jnp.zeros((SUBLANES - 4, GDN_HEADS, D_MODEL), F32)], axis=0)
    w_head = jnp.transpose(w_head, (1, 0, 2)).reshape(ROW_DT, D_MODEL)
    w_small_t = jnp.concatenate([w_head, wi[:, o_dt:o_gate].T], axis=0).astype(BF16)
    assert w_small_t.shape == (SMALL_ROWS, D_MODEL)
    gdn_params = jnp.concatenate([gdn_a_log[l], gdn_dt_bias[l], jnp.zeros((SUBLANES - 4, GDN_HEADS), F32)], axis=0)
    gdn_params = _lane_rep(gdn_params.T)

    w_router = jnp.zeros((D_MODEL, LANES), F32)
    w_router = w_router.at[:, 0:N_EXPERTS].set(w_router_expert[l]).at[:, N_EXPERTS:N_EXPERTS + N_GROUPS].set(
        w_router_group[l])
    b_router = jnp.zeros((1, LANES), F32)
    b_router = b_router.at[0, 0:N_EXPERTS].set(b_router_expert[l]).at[0, N_EXPERTS:N_EXPERTS + N_GROUPS].set(
        b_router_group[l])

    wts = {
        "w_big": w_big, "w_small_t": w_small_t,
        "gdn_conv_w": gdn_conv_w[l], "gdn_conv_b": gdn_conv_b[l].reshape(1, -1),
        "gdn_params": gdn_params,
        "gdn_norm_w": gdn_norm_w[l].reshape(1, -1),
        "ssd_conv_w": ssd_conv_w[l], "ssd_conv_b": ssd_conv_b[l].reshape(1, -1),
        "ssd_a_log": _lane_rep(ssd_a_log[l]), "ssd_dt_bias": _lane_rep(ssd_dt_bias[l]),
        "ssd_d": jnp.repeat(ssd_d[l], SSD_P).reshape(1, -1), "ssd_norm_w": ssd_norm_w[l].reshape(1, -1),
        "w_gdn_out": w_gdn_out[l].astype(BF16), "w_ssd_out": w_ssd_out[l].astype(BF16),
        "w_o": w_o[l].astype(BF16),
        "ln1_g": ln1_g[l].reshape(1, -1), "ln1_b": ln1_b[l].reshape(1, -1),
        "w_router": w_router.astype(BF16), "b_router": b_router,
        "w_exp_gate": _group_columns(w_exp_gate[l]), "w_exp_up": _group_columns(w_exp_up[l]),
        "w_exp_down": w_exp_down[l].reshape(N_GROUPS, EXPERTS_PER_GROUP * D_EXPERT, D_MODEL).astype(BF16),
        "ln2_g": ln2_g[l].reshape(1, -1), "ln2_b": ln2_b[l].reshape(1, -1),
    }

    pos = _grid_pos_embed(seq_l, D_MODEL)
    h0 = state_ssd[:, l].reshape(bsz_l, 2, SSD_GROUPS, HEADS_PER_GROUP, SSD_P, SSD_N)
    h0t = jnp.transpose(h0, (0, 1, 2, 5, 3, 4)).reshape(bsz_l, 2, SSD_GROUPS, SSD_N, GROUP_W)

    y_ctx, (s_gdn, h_ssd_t) = _stream(x_prompt, None, ada3, 0, None, None, True, wts)
    y_lat, _ = _stream(x_sample, pos, ada3, 1, state_gdn, h0t, False, wts)

    h_ssd = h_ssd_t.reshape(bsz_c, 2, SSD_GROUPS, SSD_N, HEADS_PER_GROUP, SSD_P)
    new_ssd = jnp.transpose(h_ssd, (0, 1, 2, 4, 5, 3)).reshape(bsz_c, 1, 2, SSD_HEADS, SSD_P, SSD_N)
    return (y_ctx, y_lat, s_gdn, new_ssd)
```

```python
import functools
import math

import jax
import jax.numpy as jnp
import numpy as np
from jax import lax
from jax.experimental import pallas as pl
from jax.experimental.pallas import tpu as pltpu

F32 = jnp.float32
BF16 = jnp.bfloat16
HIGHEST = lax.Precision.HIGHEST

D_MODEL = 1024
GRID_W = 64
POS_BASE = 10000.0
CONV_K = 5
GDN_HEADS = 8
GDN_DK = 128
GDN_DV = 128
SSD_HEADS = 32
SSD_P = 64
SSD_INNER = SSD_HEADS * SSD_P
SSD_GROUPS = 4
SSD_N = 128
HEADS_PER_GROUP = SSD_HEADS // SSD_GROUPS
GROUP_W = HEADS_PER_GROUP * SSD_P
N_GROUPS = 4
EXPERTS_PER_GROUP = 8
N_EXPERTS = N_GROUPS * EXPERTS_PER_GROUP
D_EXPERT = 256
EPS = 1e-6
DEPTH = 1
ALPHA = (2.0 * DEPTH) ** 0.25

LANES = 128
SUBLANES = 8
CHUNK = 128
CONV_PAD = SUBLANES
VMEM_LIMIT = 56 * 1024 * 1024

COL_QKV = 0
COL_ZG = 3072
COL_XBC = 4096
COL_ZS = 7168
COL_GATE = 9216
PROJ_COLS = 11264
PROJ_TN = 1024
GDN_ROW_BETA = 0
GDN_ROW_A = 2
ROW_DT = GDN_HEADS * SUBLANES
SMALL_ROWS = 128

TOK_BLOCK = 1024
POST_BLOCK = 256
MOE_WINDOW = 1024
MOE_TILE = 288
MOE_TILES = -(-MOE_WINDOW // MOE_TILE)
GROUP_LANE = N_EXPERTS
SSD_CHUNKS_PER_STEP = 8


def _bdot(a, b):
    return jnp.dot(a.astype(BF16), b.astype(BF16), preferred_element_type=F32)


def _sigmoid(x):
    return 0.5 * jnp.tanh(0.5 * x) + 0.5


def _silu(x):
    return x * _sigmoid(x)


def _softplus(x):
    return jnp.maximum(x, 0.0) + jnp.log1p(jnp.exp(-jnp.abs(x)))


def _ln(x):
    mu = jnp.mean(x, axis=-1, keepdims=True)
    xc = x - mu
    var = jnp.mean(xc * xc, axis=-1, keepdims=True)
    return xc * lax.rsqrt(var + EPS)


def _lane_scan(x, reverse):
    lane = lax.broadcasted_iota(jnp.int32, x.shape, 1)
    s = 1
    while s < CHUNK:
        if reverse:
            shifted = pltpu.roll(x, CHUNK - s, axis=1)
            x = x + jnp.where(lane < CHUNK - s, shifted, 0.0)
        else:
            shifted = pltpu.roll(x, s, axis=1)
            x = x + jnp.where(lane >= s, shifted, 0.0)
        s *= 2
    return x


def _tri_masks(reverse):
    row = lax.broadcasted_iota(jnp.int32, (CHUNK, CHUNK), 0)
    col = lax.broadcasted_iota(jnp.int32, (CHUNK, CHUNK), 1)
    if reverse:
        return row <= col, row < col
    return row >= col, row > col


def _decay_matrix(acc_row, incl):
    acc_rb = jnp.broadcast_to(acc_row, (CHUNK, CHUNK))
    acc_col = acc_rb.T
    decay = jnp.exp(jnp.where(incl, acc_col - acc_rb, -jnp.inf))
    return decay, acc_col


def _conv_tap_sum(pad_ref, t0, lo, hi, w_ref, b_ref):
    acc = jnp.broadcast_to(b_ref[...], (CHUNK, hi - lo))
    for j in range(CONV_K):
        start = t0 + CONV_PAD - CONV_K // 2 + j
        acc = acc + pad_ref[start:start + CHUNK, lo:hi] * w_ref[j:j + 1, :]
    return acc


def _ada_kernel(c_ref, w_ref, b_ref, o_ref):
    s = _silu(c_ref[...])
    o_ref[...] = jnp.dot(s, w_ref[...], precision=HIGHEST, preferred_element_type=F32) + b_ref[...]


def _ada(cvec, w_ada, b_ada):
    n_out = w_ada.shape[1]
    tn = 1024
    return pl.pallas_call(
        _ada_kernel,
        grid=(n_out // tn,),
        in_specs=[
            pl.BlockSpec((SUBLANES, D_MODEL), lambda j: (0, 0)),
            pl.BlockSpec((D_MODEL, tn), lambda j: (0, j)),
            pl.BlockSpec((1, tn), lambda j: (0, j)),
        ],
        out_specs=pl.BlockSpec((SUBLANES, tn), lambda j: (0, j)),
        out_shape=jax.ShapeDtypeStruct((SUBLANES, n_out), F32),
        compiler_params=pltpu.CompilerParams(dimension_semantics=("arbitrary",), vmem_limit_bytes=VMEM_LIMIT),
        name="ada",
    )(cvec, w_ada, b_ada.reshape(1, n_out))


def _inproj_kernel(*refs, has_pos):
    if has_pos:
        x_ref, pos_ref, ada_ref, w_ref, wst_ref, proj_ref, small_ref, h_ref = refs
    else:
        x_ref, ada_ref, w_ref, wst_ref, proj_ref, small_ref, h_ref = refs
        pos_ref = None

    @pl.when(pl.program_id(1) == 0)
    def _():
        x = x_ref[...]
        if pos_ref is not None:
            x = x + pos_ref[...]
        shift = ada_ref[:, 0:D_MODEL]
        scale = ada_ref[:, D_MODEL:2 * D_MODEL]
        h = (_ln(x) * (1.0 + scale) + shift).astype(BF16)
        h_ref[...] = h
        small_ref[...] = lax.dot_general(wst_ref[...], h, (((1,), (1,)), ((), ())), preferred_element_type=F32)

    proj_ref[...] = jnp.dot(h_ref[...], w_ref[...], preferred_element_type=F32)


def _in_proj(x2d, pos, ada3, w_big, w_small_t, seq_len, ada_row0):
    n_tok = x2d.shape[0]
    t = TOK_BLOCK
    blocks_per_seq = seq_len // t
    has_pos = pos is not None

    def ada_map(i, j):
        return (ada_row0 + (i // blocks_per_seq if has_pos else 0), 0, 0)

    in_specs = [pl.BlockSpec((t, D_MODEL), lambda i, j: (i, 0))]
    args = [x2d]
    if has_pos:
        in_specs.append(pl.BlockSpec((t, D_MODEL), lambda i, j: (i % blocks_per_seq, 0)))
        args.append(pos)
    in_specs += [
        pl.BlockSpec((None, 1, 6 * D_MODEL), ada_map),
        pl.BlockSpec((D_MODEL, PROJ_TN), lambda i, j: (0, j)),
        pl.BlockSpec((SMALL_ROWS, D_MODEL), lambda i, j: (0, 0)),
    ]
    args += [ada3, w_big, w_small_t]
    return pl.pallas_call(
        functools.partial(_inproj_kernel, has_pos=has_pos),
        grid=(n_tok // t, PROJ_COLS // PROJ_TN),
        in_specs=in_specs,
        out_specs=[
            pl.BlockSpec((t, PROJ_TN), lambda i, j: (i, j)),
            pl.BlockSpec((SMALL_ROWS, t), lambda i, j: (0, i)),
        ],
        out_shape=[
            jax.ShapeDtypeStruct((n_tok, PROJ_COLS), F32),
            jax.ShapeDtypeStruct((SMALL_ROWS, n_tok), F32),
        ],
        scratch_shapes=[pltpu.VMEM((t, D_MODEL), BF16)],
        compiler_params=pltpu.CompilerParams(dimension_semantics=("arbitrary", "arbitrary"),
                                             vmem_limit_bytes=VMEM_LIMIT),
        name="in_proj",
    )(*args)


INV_BASE = 8
GDN_GROUP = 8


def _take_blocks(m, size, odd):
    parts = [m[k * size:(k + 1) * size] for k in range(CHUNK // size) if (k % 2 == 1) == odd]
    return parts[0] if len(parts) == 1 else jnp.concatenate(parts, axis=0)


def _interleave_blocks(even_rows, odd_rows, size):
    parts = []
    for k in range(CHUNK // (2 * size)):
        parts.append(even_rows[k * size:(k + 1) * size])
        parts.append(odd_rows[k * size:(k + 1) * size])
    return jnp.concatenate(parts, axis=0)


def _unit_tri_inverses(nmats, uppers):
    row = lax.broadcasted_iota(jnp.int32, (CHUNK, CHUNK), 0)
    col = lax.broadcasted_iota(jnp.int32, (CHUNK, CHUNK), 1)

    def same_block(size):
        shift = int(math.log2(size))
        return (row >> shift) == (col >> shift)

    eye = (row == col).astype(F32)
    base = same_block(INV_BASE)
    nds = [jnp.where(base, n, 0.0) for n in nmats]
    xs = [eye - nd for nd in nds]
    pws = [_bdot(nd, nd) for nd in nds]
    size = 2
    while True:
        xs = [x + _bdot(x, pw) for x, pw in zip(xs, pws)]
        size *= 2
        if size >= INV_BASE:
            break
        pws = [_bdot(pw, pw) for pw in pws]
    size = INV_BASE
    while size < CHUNK:
        coupling = same_block(2 * size) & jnp.logical_not(same_block(size))
        offs = [_take_blocks(jnp.where(coupling, n, 0.0), size, odd=not up) for n, up in zip(nmats, uppers)]
        tmps = [_bdot(off, x) for off, x in zip(offs, xs)]
        zeros = jnp.zeros((CHUNK // 2, CHUNK), F32)
        fulls = [_interleave_blocks(t, zeros, size) if up else _interleave_blocks(zeros, t, size)
                 for t, up in zip(tmps, uppers)]
        moved = [_take_blocks(x, size, odd=not up) for x, up in zip(xs, uppers)]
        kept = [_take_blocks(x, size, odd=up) for x, up in zip(xs, uppers)]
        news = [m - _bdot(m, f) for m, f in zip(moved, fulls)]
        xs = [_interleave_blocks(nw, kp, size) if up else _interleave_blocks(kp, nw, size)
              for nw, kp, up in zip(news, kept, uppers)]
        size *= 2
    return xs


def _gdn_prepare(chunks, qn_ref, kn_ref, vn_ref, kt_ref, sc_ref, lhs_ref, sb_ref, ob_ref, eg_ref):
    grams = [_bdot(jnp.concatenate([kn_ref[g], qn_ref[g]], axis=0), kt_ref[g]) for g in chunks]
    units = []
    for g, gram in zip(chunks, grams):
        for d in range(2):
            reverse = d == 1
            incl, strict = _tri_masks(reverse)
            beta = sc_ref[g, d:d + 1, :]
            gc = sc_ref[g, 2 + d:3 + d, :]
            decay, gcol = _decay_matrix(gc, incl)
            beta_rb = jnp.broadcast_to(beta, (CHUNK, CHUNK))
            glast = gc[:, 0:1] if reverse else gc[:, CHUNK - 1:CHUNK]
            sc_ref[g, 4 + d:5 + d, :] = jnp.broadcast_to(jnp.exp(glast), (1, CHUNK))
            eg_ref[g, d] = jnp.exp(gcol)
            units.append(dict(
                g=g, d=d,
                nmat=jnp.where(strict, gram[0:CHUNK] * decay, 0.0) * beta_rb,
                att=gram[CHUNK:2 * CHUNK] * decay * beta_rb,
                kd_scale=beta * jnp.exp(glast - gc)))
    xinvs = _unit_tri_inverses([u["nmat"] for u in units], [u["d"] == 1 for u in units])
    uws = [_bdot(x, jnp.concatenate([vn_ref[u["g"]], kn_ref[u["g"]] * eg_ref[u["g"], u["d"]]], axis=1))
           for u, x in zip(units, xinvs)]
    mixeds = []
    for u, uw in zip(units, uws):
        kd = kt_ref[u["g"]] * jnp.broadcast_to(u["kd_scale"], (CHUNK, CHUNK))
        mixeds.append(_bdot(jnp.concatenate([kd, u["att"]], axis=0), uw))
    for u, mixed in zip(units, mixeds):
        g, d = u["g"], u["d"]
        lhs_ref[g, d, 0:CHUNK, :] = mixed[0:CHUNK, GDN_DV:].astype(BF16)
        lhs_ref[g, d, CHUNK:2 * CHUNK, :] = qn_ref[g].astype(BF16)
        lhs_ref[g, d, 2 * CHUNK:3 * CHUNK, :] = mixed[CHUNK:, GDN_DV:].astype(BF16)
        sb_ref[g, d] = mixed[0:CHUNK, 0:GDN_DV]
        ob_ref[g, d] = mixed[CHUNK:, 0:GDN_DV]


def _gdn_steps(steps, lhs_ref, sb_ref, ob_ref, eg_ref, sc_ref, s_ref, o_refs):
    states = [s_ref[s, d] for _, s, d in steps]
    rs = [jnp.dot(lhs_ref[g, d], st.astype(BF16), preferred_element_type=F32)
          for (g, _, d), st in zip(steps, states)]
    for (g, s, d), st, r in zip(steps, states, rs):
        s_ref[s, d] = st * sc_ref[g, 4 + d:5 + d, :] - r[0:CHUNK] + sb_ref[g, d]
        o_refs[d][g] = eg_ref[g, d] * r[CHUNK:2 * CHUNK] - r[2 * CHUNK:3 * CHUNK] + ob_ref[g, d]


def _gdn_kernel(*refs, n_seq, n_chunks, has_s0, want_state):
    it = iter(refs)
    q_ref, k_ref, v_ref, z_ref, sm_ref = (next(it) for _ in range(5))
    wq_ref, wk_ref, wv_ref, bq_ref, bk_ref, bv_ref = (next(it) for _ in range(6))
    par_ref, nw_ref = (next(it) for _ in range(2))
    s0_ref = next(it) if has_s0 else None
    og_ref = next(it)
    sout_ref = next(it) if want_state else None
    pad_ref, qn_ref, kn_ref, vn_ref, kt_ref, of_ref, ob_ref, sc_ref, s_ref = (next(it) for _ in range(9))
    lhs_ref, sb_ref, ou_ref, eg_ref, st_ref = (next(it) for _ in range(5))

    seq = n_chunks * CHUNK
    total = n_seq * n_chunks
    zero_rows = jnp.zeros((CONV_PAD, 3 * LANES), F32)
    for s in range(n_seq):
        pad_ref[s, 0:CONV_PAD, :] = zero_rows
        pad_ref[s, CONV_PAD + seq:2 * CONV_PAD + seq, :] = zero_rows
        for c in range(n_chunks):
            t0 = c * CHUNK
            r0 = s * seq + t0
            pad_ref[s, CONV_PAD + t0:CONV_PAD + t0 + CHUNK, 0:LANES] = q_ref[r0:r0 + CHUNK, :]
            pad_ref[s, CONV_PAD + t0:CONV_PAD + t0 + CHUNK, LANES:2 * LANES] = k_ref[r0:r0 + CHUNK, :]
            pad_ref[s, CONV_PAD + t0:CONV_PAD + t0 + CHUNK, 2 * LANES:3 * LANES] = v_ref[r0:r0 + CHUNK, :]

    a_neg = [-jnp.exp(par_ref[d:d + 1, :]) for d in range(2)]
    dt_bias = [par_ref[2 + d:3 + d, :] for d in range(2)]
    for s in range(n_seq):
        for c in range(n_chunks):
            t0 = c * CHUNK
            r0 = s * seq + t0
            g = s * n_chunks + c
            pad_s = pad_ref.at[s]
            qa = _silu(_conv_tap_sum(pad_s, t0, 0, LANES, wq_ref, bq_ref))
            ka = _silu(_conv_tap_sum(pad_s, t0, LANES, 2 * LANES, wk_ref, bk_ref))
            va = _silu(_conv_tap_sum(pad_s, t0, 2 * LANES, 3 * LANES, wv_ref, bv_ref))
            qn = qa * lax.rsqrt(jnp.sum(qa * qa, axis=-1, keepdims=True) + EPS) * (GDN_DK ** -0.5)
            kn = ka * lax.rsqrt(jnp.sum(ka * ka, axis=-1, keepdims=True) + EPS)
            qn_ref[g] = qn
            kn_ref[g] = kn
            vn_ref[g] = va
            kt_ref[g] = kn.T

    for g in range(total):
        r0 = g * CHUNK
        for d in range(2):
            st_ref[d, g:g + 1, :] = sm_ref[GDN_ROW_A + d:GDN_ROW_A + d + 1, r0:r0 + CHUNK]
            st_ref[2 + d, g:g + 1, :] = sm_ref[GDN_ROW_BETA + d:GDN_ROW_BETA + d + 1, r0:r0 + CHUNK]
    for d in range(2):
        gcum = _lane_scan(a_neg[d] * _softplus(st_ref[d] + dt_bias[d]), reverse=d == 1)
        beta = _sigmoid(st_ref[2 + d])
        for g in range(total):
            sc_ref[g, d:d + 1, :] = beta[g:g + 1, :]
            sc_ref[g, 2 + d:3 + d, :] = gcum[g:g + 1, :]

    def prepare(i, carry):
        _gdn_prepare([i * GDN_GROUP + j for j in range(GDN_GROUP)], qn_ref, kn_ref, vn_ref, kt_ref, sc_ref,
                     lhs_ref, sb_ref, ou_ref, eg_ref)
        return carry

    lax.fori_loop(0, total // GDN_GROUP, prepare, 0)

    if has_s0:
        s_ref[...] = s0_ref[...]
    else:
        s_ref[...] = jnp.zeros(s_ref.shape, F32)

    def advance(i, carry):
        steps = []
        for s in range(n_seq):
            steps.append((s * n_chunks + i, s, 0))
            steps.append((s * n_chunks + n_chunks - 1 - i, s, 1))
        _gdn_steps(steps, lhs_ref, sb_ref, ou_ref, eg_ref, sc_ref, s_ref, (of_ref, ob_ref))
        return carry

    lax.fori_loop(0, n_chunks, advance, 0)

    if want_state:
        sout_ref[...] = s_ref[...]
    for g in range(total):
        r0 = g * CHUNK
        o = of_ref[g] + ob_ref[g]
        o = o * lax.rsqrt(jnp.mean(o * o, axis=-1, keepdims=True) + EPS)
        og_ref[r0:r0 + CHUNK, :] = (o * nw_ref[...] * _silu(z_ref[r0:r0 + CHUNK, :])).astype(BF16)


def _gdn(proj, small_t, conv_w, conv_b, head_params, norm_w, s0, bsz, seq_len, want_state):
    n_chunks = seq_len // CHUNK
    n_tok = bsz * seq_len
    has_s0 = s0 is not None
    n_seq = max(1, GDN_GROUP * 2 // n_chunks)
    assert bsz % n_seq == 0 and (n_seq * n_chunks) % GDN_GROUP == 0
    total = n_seq * n_chunks
    rows = n_seq * seq_len
    col = lambda off: (lambda b, h: (b, off + h))
    cw = lambda off: (lambda b, h: (0, off + h))
    in_specs = [
        pl.BlockSpec((rows, LANES), col(COL_QKV // LANES)),
        pl.BlockSpec((rows, LANES), col(COL_QKV // LANES + GDN_HEADS)),
        pl.BlockSpec((rows, LANES), col(COL_QKV // LANES + 2 * GDN_HEADS)),
        pl.BlockSpec((rows, LANES), col(COL_ZG // LANES)),
        pl.BlockSpec((SUBLANES, rows), lambda b, h: (h, b)),
        pl.BlockSpec((CONV_K, LANES), cw(0)),
        pl.BlockSpec((CONV_K, LANES), cw(GDN_HEADS)),
        pl.BlockSpec((CONV_K, LANES), cw(2 * GDN_HEADS)),
        pl.BlockSpec((1, LANES), cw(0)),
        pl.BlockSpec((1, LANES), cw(GDN_HEADS)),
        pl.BlockSpec((1, LANES), cw(2 * GDN_HEADS)),
        pl.BlockSpec((None, SUBLANES, LANES), lambda b, h: (h, 0, 0)),
        pl.BlockSpec((1, LANES), lambda b, h: (0, 0)),
    ]
    args = [proj, proj, proj, proj, small_t, conv_w, conv_w, conv_w, conv_b, conv_b, conv_b,
            head_params, norm_w]
    state_spec = pl.BlockSpec((n_seq, None, 2, None, GDN_DK, GDN_DV), lambda b, h: (b, 0, 0, h, 0, 0))
    if has_s0:
        in_specs.append(state_spec)
        args.append(s0)
    out_specs = [pl.BlockSpec((rows, LANES), lambda b, h: (b, h))]
    out_shape = [jax.ShapeDtypeStruct((n_tok, GDN_HEADS * GDN_DV), BF16)]
    if want_state:
        out_specs.append(state_spec)
        out_shape.append(jax.ShapeDtypeStruct((bsz, 1, 2, GDN_HEADS, GDN_DK, GDN_DV), F32))
    chunked = pltpu.VMEM((total, CHUNK, LANES), F32)
    per_dir = pltpu.VMEM((total, 2, CHUNK, LANES), F32)
    return pl.pallas_call(
        functools.partial(_gdn_kernel, n_seq=n_seq, n_chunks=n_chunks, has_s0=has_s0, want_state=want_state),
        grid=(bsz // n_seq, GDN_HEADS),
        in_specs=in_specs,
        out_specs=out_specs,
        out_shape=out_shape,
        scratch_shapes=[
            pltpu.VMEM((n_seq, seq_len + 2 * CONV_PAD, 3 * LANES), F32),
            chunked, chunked, chunked, chunked, chunked, chunked,
            pltpu.VMEM((total, SUBLANES, CHUNK), F32),
            pltpu.VMEM((n_seq, 2, GDN_DK, GDN_DV), F32),
            pltpu.VMEM((total, 2, 3 * CHUNK, GDN_DK), BF16),
            per_dir, per_dir, per_dir,
            pltpu.VMEM((4, total, CHUNK), F32),
        ],
        compiler_params=pltpu.CompilerParams(dimension_semantics=("arbitrary", "arbitrary"),
                                             vmem_limit_bytes=VMEM_LIMIT),
        name="gdn",
    )(*args)


def _ssd_steps(steps, xs_ref, cs_ref, bt_ref, dt_ref, ac_ref, ht_ref, y_ref):
    lane = lax.broadcasted_iota(jnp.int32, (CHUNK, LANES), 1)
    low_half = lane < SSD_P
    shared = [(_bdot(cs_ref[c], bt_ref[c]), _bdot(cs_ref[c], ht_ref[s, d])) for c, s, d in steps]
    pairs_per_stage = 2
    for p0 in range(0, HEADS_PER_GROUP // 2, pairs_per_stage):
        units = []
        for (c, s, d), (cb, y_off) in zip(steps, shared):
            reverse = d == 1
            incl, _ = _tri_masks(reverse)
            bt = bt_ref[c]
            for p in range(p0, p0 + pairs_per_stage):
                halves = []
                for r in (2 * p, 2 * p + 1):
                    row = d * HEADS_PER_GROUP + r
                    dt = dt_ref[c, row:row + 1, :]
                    ac = ac_ref[c, row:row + 1, :]
                    decay, acol = _decay_matrix(ac, incl)
                    last = ac[:, 0:1] if reverse else ac[:, CHUNK - 1:CHUNK]
                    halves.append(dict(
                        m=(cb * decay * jnp.broadcast_to(dt, (CHUNK, CHUNK))).astype(BF16),
                        bts=(bt * jnp.broadcast_to(dt * jnp.exp(last - ac), (SSD_N, CHUNK))).astype(BF16),
                        acol=acol, elast=jnp.exp(last)))
                units.append(dict(c=c, s=s, d=d, p=p, halves=halves, y_off=y_off[:, p * LANES:(p + 1) * LANES]))
        for u in units:
            x = xs_ref[u["c"], :, u["p"] * LANES:(u["p"] + 1) * LANES]
            zero = jnp.zeros_like(x)
            xh = (jnp.where(low_half, x, zero), jnp.where(low_half, zero, x))
            u["yd"] = [jnp.dot(hf["m"], xm, preferred_element_type=F32) for hf, xm in zip(u["halves"], xh)]
            u["st"] = [jnp.dot(hf["bts"], xm, preferred_element_type=F32) for hf, xm in zip(u["halves"], xh)]
        for u in units:
            c, s, d, p = u["c"], u["s"], u["d"], u["p"]
            h0, h1 = u["halves"]
            eoff = jnp.exp(jnp.where(low_half, h0["acol"], h1["acol"]))
            elast = jnp.where(low_half, h0["elast"], h1["elast"])
            cols = slice(p * LANES, (p + 1) * LANES)
            y_ref[c, :, cols] = y_ref[c, :, cols] + (u["yd"][0] + u["yd"][1]) + eoff * u["y_off"]
            ht_ref[s, d, :, cols] = ht_ref[s, d, :, cols] * elast + (u["st"][0] + u["st"][1])


def _ssd_kernel(*refs, n_seq, n_chunks, has_h0, want_state):
    it = iter(refs)
    x_ref, b_ref, c_ref, z_ref, dtf_ref, dtb_ref = (next(it) for _ in range(6))
    wx_ref, wb_ref, wc_ref, bx_ref, bb_ref, bc_ref = (next(it) for _ in range(6))
    alog_ref, dtbias_ref, dvec_ref, nw_ref = (next(it) for _ in range(4))
    h0_ref = next(it) if has_h0 else None
    yg_ref = next(it)
    hout_ref = next(it) if want_state else None
    pad_ref, xs_ref, cs_ref, bt_ref, dt_ref, ac_ref, y_ref, ht_ref = (next(it) for _ in range(8))

    seq = n_chunks * CHUNK
    total = n_seq * n_chunks
    width = GROUP_W + 2 * SSD_N
    zero_rows = jnp.zeros((CONV_PAD, width), F32)
    for s in range(n_seq):
        pad_ref[s, 0:CONV_PAD, :] = zero_rows
        pad_ref[s, CONV_PAD + seq:2 * CONV_PAD + seq, :] = zero_rows
        for c in range(n_chunks):
            t0 = c * CHUNK
            r0 = s * seq + t0
            pad_ref[s, CONV_PAD + t0:CONV_PAD + t0 + CHUNK, 0:GROUP_W] = x_ref[r0:r0 + CHUNK, :]
            pad_ref[s, CONV_PAD + t0:CONV_PAD + t0 + CHUNK, GROUP_W:GROUP_W + SSD_N] = b_ref[r0:r0 + CHUNK, :]
            pad_ref[s, CONV_PAD + t0:CONV_PAD + t0 + CHUNK, GROUP_W + SSD_N:width] = c_ref[r0:r0 + CHUNK, :]

    raw_dt = (dtf_ref, dtb_ref)
    for s in range(n_seq):
        pad_s = pad_ref.at[s]
        for c in range(n_chunks):
            t0 = c * CHUNK
            g = s * n_chunks + c
            for lo in range(0, GROUP_W, LANES):
                cols = slice(lo, lo + LANES)
                xa = _silu(_conv_tap_sum(pad_s, t0, lo, lo + LANES, wx_ref.at[:, cols], bx_ref.at[:, cols]))
                xs_ref[g, :, cols] = xa.astype(BF16)
                y_ref[g, :, cols] = xa * dvec_ref[:, cols]
            ba = _silu(_conv_tap_sum(pad_s, t0, GROUP_W, GROUP_W + SSD_N, wb_ref, bb_ref))
            ca = _silu(_conv_tap_sum(pad_s, t0, GROUP_W + SSD_N, width, wc_ref, bc_ref))
            cs_ref[g] = ca
            bt_ref[g] = ba.T
            for d in range(2):
                dt_ref[g, d * HEADS_PER_GROUP:(d + 1) * HEADS_PER_GROUP, :] = raw_dt[d][:, g * CHUNK:(g + 1) * CHUNK]

    for d in range(2):
        lo, hi = d * HEADS_PER_GROUP, (d + 1) * HEADS_PER_GROUP
        dt = _softplus(dt_ref[:, lo:hi, :] + dtbias_ref[d][None])
        dt_ref[:, lo:hi, :] = dt
        scaled = (dt * (-jnp.exp(alog_ref[d]))[None]).reshape(total * HEADS_PER_GROUP, CHUNK)
        ac_ref[:, lo:hi, :] = _lane_scan(scaled, reverse=d == 1).reshape(total, HEADS_PER_GROUP, CHUNK)

    if has_h0:
        ht_ref[...] = h0_ref[...]
    else:
        ht_ref[...] = jnp.zeros(ht_ref.shape, F32)

    def body(i, carry):
        for s in range(n_seq):
            _ssd_steps([(s * n_chunks + i, s, 0), (s * n_chunks + n_chunks - 1 - i, s, 1)],
                       xs_ref, cs_ref, bt_ref, dt_ref, ac_ref, ht_ref, y_ref)
        return carry

    lax.fori_loop(0, n_chunks, body, 0)

    if want_state:
        hout_ref[...] = ht_ref[...]
    for g in range(total):
        r0 = g * CHUNK
        ssq = jnp.zeros((CHUNK, 1), F32)
        for lo in range(0, GROUP_W, LANES):
            cols = slice(lo, lo + LANES)
            y = y_ref[g, :, cols] * _silu(z_ref[r0:r0 + CHUNK, cols])
            y_ref[g, :, cols] = y
            ssq = ssq + jnp.sum(y * y, axis=-1, keepdims=True)
        inv = lax.rsqrt(ssq * (1.0 / GROUP_W) + EPS)
        for lo in range(0, GROUP_W, LANES):
            cols = slice(lo, lo + LANES)
            yg_ref[r0:r0 + CHUNK, cols] = (y_ref[g, :, cols] * inv * nw_ref[:, cols]).astype(BF16)


def _ssd(proj, small_t, conv_w, conv_b, a_log_rep, dt_bias_rep, d_vec, norm_w, h0t, bsz, seq_len, want_state):
    n_chunks = seq_len // CHUNK
    n_tok = bsz * seq_len
    has_h0 = h0t is not None
    n_seq = max(1, SSD_CHUNKS_PER_STEP // n_chunks)
    assert bsz % n_seq == 0
    total = n_seq * n_chunks
    rows_per_step = n_seq * seq_len
    xbc_w = COL_XBC // GROUP_W
    bc_l = (COL_XBC + SSD_INNER) // LANES
    in_specs = [
        pl.BlockSpec((rows_per_step, GROUP_W), lambda b, g: (b, xbc_w + g)),
        pl.BlockSpec((rows_per_step, SSD_N), lambda b, g: (b, bc_l + g)),
        pl.BlockSpec((rows_per_step, SSD_N), lambda b, g: (b, bc_l + SSD_GROUPS + g)),
        pl.BlockSpec((rows_per_step, GROUP_W), lambda b, g: (b, COL_ZS // GROUP_W + g)),
        pl.BlockSpec((HEADS_PER_GROUP, rows_per_step), lambda b, g: (ROW_DT // HEADS_PER_GROUP + g, b)),
        pl.BlockSpec((HEADS_PER_GROUP, rows_per_step),
                     lambda b, g: ((ROW_DT + SSD_HEADS) // HEADS_PER_GROUP + g, b)),
        pl.BlockSpec((CONV_K, GROUP_W), lambda b, g: (0, g)),
        pl.BlockSpec((CONV_K, SSD_N), lambda b, g: (0, SSD_INNER // SSD_N + g)),
        pl.BlockSpec((CONV_K, SSD_N), lambda b, g: (0, SSD_INNER // SSD_N + SSD_GROUPS + g)),
        pl.BlockSpec((1, GROUP_W), lambda b, g: (0, g)),
        pl.BlockSpec((1, SSD_N), lambda b, g: (0, SSD_INNER // SSD_N + g)),
        pl.BlockSpec((1, SSD_N), lambda b, g: (0, SSD_INNER // SSD_N + SSD_GROUPS + g)),
        pl.BlockSpec((2, HEADS_PER_GROUP, LANES), lambda b, g: (0, g, 0)),
        pl.BlockSpec((2, HEADS_PER_GROUP, LANES), lambda b, g: (0, g, 0)),
        pl.BlockSpec((1, GROUP_W), lambda b, g: (0, g)),
        pl.BlockSpec((1, GROUP_W), lambda b, g: (0, g)),
    ]
    args = [proj, proj, proj, proj, small_t, small_t, conv_w, conv_w, conv_w, conv_b, conv_b, conv_b,
            a_log_rep, dt_bias_rep, d_vec, norm_w]
    state_spec = pl.BlockSpec((n_seq, 2, None, SSD_N, GROUP_W), lambda b, g: (b, 0, g, 0, 0))
    if has_h0:
        in_specs.append(state_spec)
        args.append(h0t)
    out_specs = [pl.BlockSpec((rows_per_step, GROUP_W), lambda b, g: (b, g))]
    out_shape = [jax.ShapeDtypeStruct((n_tok, SSD_INNER), BF16)]
    if want_state:
        out_specs.append(state_spec)
        out_shape.append(jax.ShapeDtypeStruct((bsz, 2, SSD_GROUPS, SSD_N, GROUP_W), F32))
    rows = pltpu.VMEM((total, 2 * HEADS_PER_GROUP, CHUNK), F32)
    return pl.pallas_call(
        functools.partial(_ssd_kernel, n_seq=n_seq, n_chunks=n_chunks, has_h0=has_h0, want_state=want_state),
        grid=(bsz // n_seq, SSD_GROUPS),
        in_specs=in_specs,
        out_specs=out_specs,
        out_shape=out_shape,
        scratch_shapes=[
            pltpu.VMEM((n_seq, seq_len + 2 * CONV_PAD, GROUP_W + 2 * SSD_N), F32),
            pltpu.VMEM((total, CHUNK, GROUP_W), BF16),
            pltpu.VMEM((total, CHUNK, SSD_N), F32),
            pltpu.VMEM((total, SSD_N, CHUNK), F32),
            rows, rows,
            pltpu.VMEM((total, CHUNK, GROUP_W), F32),
            pltpu.VMEM((n_seq, 2, SSD_N, GROUP_W), F32),
        ],
        compiler_params=pltpu.CompilerParams(dimension_semantics=("arbitrary", "arbitrary"),
                                             vmem_limit_bytes=VMEM_LIMIT),
        name="ssd",
    )(*args)


def _route(logits):
    lane = lax.broadcasted_iota(jnp.int32, logits.shape, 1)
    neg = -jnp.inf
    is_grp = (lane >= N_EXPERTS) & (lane < N_EXPERTS + N_GROUPS)
    gl = jnp.where(is_grp, logits, neg)
    gmax = jnp.max(gl, axis=-1, keepdims=True)
    ge = jnp.exp(gl - gmax)
    p_grp = ge / jnp.sum(ge, axis=-1, keepdims=True)
    p_top = jnp.max(p_grp, axis=-1, keepdims=True)
    g_idx = jnp.min(jnp.where(is_grp & (p_grp == p_top), lane, 2 * LANES), axis=-1, keepdims=True) - N_EXPERTS
    in_grp = (lane >= g_idx * EXPERTS_PER_GROUP) & (lane < (g_idx + 1) * EXPERTS_PER_GROUP)
    el = jnp.where(in_grp, logits, neg)
    emax = jnp.max(el, axis=-1, keepdims=True)
    ee = jnp.exp(el - emax)
    p_e = ee / jnp.sum(ee, axis=-1, keepdims=True)
    w1 = jnp.max(p_e, axis=-1, keepdims=True)
    i1 = jnp.min(jnp.where(in_grp & (p_e == w1), lane, 2 * LANES), axis=-1, keepdims=True)
    rest = jnp.where(in_grp & (lane != i1), p_e, -1.0)
    w2 = jnp.max(rest, axis=-1, keepdims=True)
    i2 = jnp.min(jnp.where(rest == w2, lane, 2 * LANES), axis=-1, keepdims=True)
    tot = w1 + w2
    gates = jnp.where(lane == i1, w1 / tot * p_top, 0.0) + jnp.where(lane == i2, w2 / tot * p_top, 0.0)
    return jnp.where(lane == GROUP_LANE, g_idx.astype(F32), gates)


def _post_kernel(*refs, has_pos):
    it = iter(refs)
    x_ref = next(it)
    pos_ref = next(it) if has_pos else None
    og_ref, yg_ref, gg_ref, gs_ref, ada_ref = (next(it) for _ in range(5))
    wg_ref, ws_ref, wo_ref, g1_ref, b1_ref, wr_ref, br_ref = (next(it) for _ in range(7))
    x1_ref, h2_ref, gates_ref = (next(it) for _ in range(3))

    x = x_ref[...]
    if has_pos:
        x = x + pos_ref[...]
    u_g = jnp.dot(og_ref[...], wg_ref[...], preferred_element_type=F32)
    u_s = jnp.dot(yg_ref[...], ws_ref[...], preferred_element_type=F32)
    m = _sigmoid(gg_ref[...]) * u_g + _sigmoid(gs_ref[...]) * u_s
    mix = jnp.dot(m.astype(BF16), wo_ref[...], preferred_element_type=F32)
    gate1 = ada_ref[:, 2 * D_MODEL:3 * D_MODEL]
    shift2 = ada_ref[:, 3 * D_MODEL:4 * D_MODEL]
    scale2 = ada_ref[:, 4 * D_MODEL:5 * D_MODEL]
    x1 = _ln(ALPHA * x + gate1 * mix) * g1_ref[...] + b1_ref[...]
    x1_ref[...] = x1
    h2 = _ln(x1) * (1.0 + scale2) + shift2
    h2b = h2.astype(BF16)
    h2_ref[...] = h2b
    logits = jnp.dot(h2b, wr_ref[...], preferred_element_type=F32) + br_ref[...]
    gates_ref[...] = _route(logits)


def _post(x2d, pos, og, yg, proj, ada3, w_gdn_out, w_ssd_out, w_o, ln_g, ln_b, w_router, b_router,
          seq_len, ada_row0):
    n_tok = x2d.shape[0]
    t = POST_BLOCK
    blocks_per_seq = seq_len // t
    has_pos = pos is not None

    def ada_map(i):
        return (ada_row0 + (i // blocks_per_seq if has_pos else 0), 0, 0)

    const = lambda i: (0, 0)
    in_specs = [pl.BlockSpec((t, D_MODEL), lambda i: (i, 0))]
    args = [x2d]
    if has_pos:
        in_specs.append(pl.BlockSpec((t, D_MODEL), lambda i: (i % blocks_per_seq, 0)))
        args.append(pos)
    in_specs += [
        pl.BlockSpec((t, GDN_HEADS * GDN_DV), lambda i: (i, 0)),
        pl.BlockSpec((t, SSD_INNER), lambda i: (i, 0)),
        pl.BlockSpec((t, D_MODEL), lambda i: (i, COL_GATE // D_MODEL)),
        pl.BlockSpec((t, D_MODEL), lambda i: (i, COL_GATE // D_MODEL + 1)),
        pl.BlockSpec((None, 1, 6 * D_MODEL), ada_map),
        pl.BlockSpec((GDN_HEADS * GDN_DV, D_MODEL), const),
        pl.BlockSpec((SSD_INNER, D_MODEL), const),
        pl.BlockSpec((D_MODEL, D_MODEL), const),
        pl.BlockSpec((1, D_MODEL), const),
        pl.BlockSpec((1, D_MODEL), const),
        pl.BlockSpec((D_MODEL, LANES), const),
        pl.BlockSpec((1, LANES), const),
    ]
    args += [og, yg, proj, proj, ada3, w_gdn_out, w_ssd_out, w_o, ln_g, ln_b, w_router, b_router]
    return pl.pallas_call(
        functools.partial(_post_kernel, has_pos=has_pos),
        grid=(n_tok // t,),
        in_specs=in_specs,
        out_specs=[
            pl.BlockSpec((t, D_MODEL), lambda i: (i, 0)),
            pl.BlockSpec((t, D_MODEL), lambda i: (i, 0)),
            pl.BlockSpec((t, LANES), lambda i: (i, 0)),
        ],
        out_shape=[
            jax.ShapeDtypeStruct((n_tok, D_MODEL), F32),
            jax.ShapeDtypeStruct((n_tok, D_MODEL), BF16),
            jax.ShapeDtypeStruct((n_tok, LANES), F32),
        ],
        compiler_params=pltpu.CompilerParams(dimension_semantics=("arbitrary",), vmem_limit_bytes=VMEM_LIMIT),
        name="post",
    )(*args)


def _moe_kernel(h_ref, gates_ref, x1_ref, ada_ref, wg_ref, wu_ref, wd_ref, g2_ref, b2_ref, out_ref,
                col_ref, row_ref, gx_ref, cnt_ref):
    g = pl.program_id(1)
    t = MOE_WINDOW
    lane = lax.broadcasted_iota(jnp.int32, (t, LANES), 1)

    @pl.when(g == 0)
    def _():
        gates = gates_ref[...]
        grp = jnp.sum(jnp.where(lane == GROUP_LANE, gates, 0.0), axis=-1, keepdims=True)
        onehot = jnp.where((lane < N_GROUPS) & (grp == lane.astype(F32)), 1.0, 0.0)
        tri = (lax.broadcasted_iota(jnp.int32, (t, t), 0) >= lax.broadcasted_iota(jnp.int32, (t, t), 1))
        cum = jnp.dot(jnp.where(tri, 1.0, 0.0).astype(BF16), onehot.astype(BF16), preferred_element_type=F32)
        rank = jnp.sum(onehot * cum, axis=-1, keepdims=True) - 1.0
        info = jnp.where(lane == 0, grp, jnp.where(lane == 1, rank, 0.0))
        col_ref[...] = info
        row_ref[...] = info.T[0:SUBLANES, :]
        totals = cum[t - 1:t, :]
        for k in range(N_GROUPS):
            cnt_ref[k] = totals[0, k].astype(jnp.int32)
        hi = gates.astype(BF16).astype(F32)
        mid = (gates - hi).astype(BF16).astype(F32)
        low = gates - hi - mid
        packed = jnp.where(lane < N_EXPERTS, hi,
                           jnp.where(lane < 2 * N_EXPERTS, pltpu.roll(mid, N_EXPERTS, axis=1),
                                     jnp.where(lane < 3 * N_EXPERTS, pltpu.roll(low, 2 * N_EXPERTS, axis=1), 0.0)))
        gx_ref[...] = packed.astype(BF16)
        out_ref[...] = jnp.zeros(out_ref.shape, F32)

    n_g = cnt_ref[g]
    gf = g.astype(F32)
    for j in range(MOE_TILES):
        @pl.when(j * MOE_TILE < n_g)
        def _(j=j):
            slot_r = (lax.broadcasted_iota(jnp.int32, (MOE_TILE, t), 0) + j * MOE_TILE).astype(F32)
            pick = jnp.where((row_ref[1:2, :] == slot_r) & (row_ref[0:1, :] == gf), 1.0, 0.0).astype(BF16)
            hs = jnp.dot(pick, h_ref[...], preferred_element_type=F32).astype(BF16)
            gsx = jnp.dot(pick, gx_ref[...], preferred_element_type=F32)
            gs = gsx + pltpu.roll(gsx, LANES - N_EXPERTS, axis=1) + pltpu.roll(gsx, LANES - 2 * N_EXPERTS, axis=1)
            lane_t = lax.broadcasted_iota(jnp.int32, (MOE_TILE, LANES), 1)
            y = jnp.zeros((MOE_TILE, D_MODEL), F32)
            for e in range(EXPERTS_PER_GROUP):
                a = jnp.dot(hs, wg_ref[e], preferred_element_type=F32)
                b = jnp.dot(hs, wu_ref[e], preferred_element_type=F32)
                gate_e = jnp.sum(jnp.where(lane_t == g * EXPERTS_PER_GROUP + e, gs, 0.0), axis=-1, keepdims=True)
                y = y + jnp.dot((_silu(a) * b * gate_e).astype(BF16), wd_ref[e], preferred_element_type=F32)
            slot_c = (lax.broadcasted_iota(jnp.int32, (t, MOE_TILE), 1) + j * MOE_TILE).astype(F32)
            place = jnp.where((col_ref[:, 1:2] == slot_c) & (col_ref[:, 0:1] == gf), 1.0, 0.0).astype(BF16)
            out_ref[...] += jnp.dot(place, y.astype(BF16), preferred_element_type=F32)

    @pl.when(g == N_GROUPS - 1)
    def _():
        gate2 = ada_ref[:, 5 * D_MODEL:6 * D_MODEL]
        out_ref[...] = _ln(ALPHA * x1_ref[...] + gate2 * out_ref[...]) * g2_ref[...] + b2_ref[...]


def _moe(h2, gates, x1, ada3, w_gate, w_up, w_down, ln_g, ln_b, seq_len, ada_row0, per_seq_ada):
    n_tok = h2.shape[0]
    t = MOE_WINDOW
    blocks_per_seq = seq_len // t
    assert not per_seq_ada or seq_len % t == 0
    group_w = EXPERTS_PER_GROUP * D_EXPERT

    def ada_map(i, g):
        return (ada_row0 + (i // blocks_per_seq if per_seq_ada else 0), 0, 0)

    return pl.pallas_call(
        _moe_kernel,
        grid=(n_tok // t, N_GROUPS),
        in_specs=[
            pl.BlockSpec((t, D_MODEL), lambda i, g: (i, 0)),
            pl.BlockSpec((t, LANES), lambda i, g: (i, 0)),
            pl.BlockSpec((t, D_MODEL), lambda i, g: (i, 0), pipeline_mode=pl.Buffered(1)),
            pl.BlockSpec((None, 1, 6 * D_MODEL), ada_map),
            pl.BlockSpec((EXPERTS_PER_GROUP, D_MODEL, D_EXPERT), lambda i, g: (g, 0, 0)),
            pl.BlockSpec((EXPERTS_PER_GROUP, D_MODEL, D_EXPERT), lambda i, g: (g, 0, 0)),
            pl.BlockSpec((EXPERTS_PER_GROUP, D_EXPERT, D_MODEL), lambda i, g: (g, 0, 0)),
            pl.BlockSpec((1, D_MODEL), lambda i, g: (0, 0)),
            pl.BlockSpec((1, D_MODEL), lambda i, g: (0, 0)),
        ],
        out_specs=pl.BlockSpec((t, D_MODEL), lambda i, g: (i, 0)),
        out_shape=jax.ShapeDtypeStruct((n_tok, D_MODEL), F32),
        scratch_shapes=[
            pltpu.VMEM((t, LANES), F32),
            pltpu.VMEM((SUBLANES, t), F32),
            pltpu.VMEM((t, LANES), BF16),
            pltpu.SMEM((N_GROUPS,), jnp.int32),
        ],
        compiler_params=pltpu.CompilerParams(dimension_semantics=("arbitrary", "arbitrary"),
                                             vmem_limit_bytes=VMEM_LIMIT),
        name="moe",
    )(h2, gates, x1, ada3, w_gate, w_up, w_down, ln_g, ln_b)


def _grid_pos_embed(n_tokens, d):
    rows = n_tokens // GRID_W
    rr, cc = np.meshgrid(np.arange(rows, dtype=np.float32), np.arange(GRID_W, dtype=np.float32), indexing="ij")
    quarter = d // 4
    freqs = np.exp(-math.log(POS_BASE) * np.arange(quarter, dtype=np.float32) / quarter).astype(np.float32)
    ang_r = rr.reshape(-1, 1) * freqs
    ang_c = cc.reshape(-1, 1) * freqs
    table = np.concatenate([np.sin(ang_r), np.cos(ang_r), np.sin(ang_c), np.cos(ang_c)], axis=-1)
    return jnp.asarray(table, dtype=F32)


def _lane_rep(v):
    return jnp.broadcast_to(v[..., None], v.shape + (LANES,)).astype(F32)


def _stream(x3d, pos, ada3, ada_row0, s_gdn0, h_ssd0t, want_state, wts):
    bsz, seq_len, _ = x3d.shape
    x2d = x3d.reshape(bsz * seq_len, D_MODEL)
    proj, small_t = _in_proj(x2d, pos, ada3, wts["w_big"], wts["w_small_t"], seq_len, ada_row0)
    gdn_out = _gdn(proj, small_t, wts["gdn_conv_w"], wts["gdn_conv_b"], wts["gdn_params"],
                   wts["gdn_norm_w"], s_gdn0, bsz, seq_len, want_state)
    ssd_out = _ssd(proj, small_t, wts["ssd_conv_w"], wts["ssd_conv_b"], wts["ssd_a_log"], wts["ssd_dt_bias"],
                   wts["ssd_d"], wts["ssd_norm_w"], h_ssd0t, bsz, seq_len, want_state)
    x1, h2, gates = _post(x2d, pos, gdn_out[0], ssd_out[0], proj, ada3, wts["w_gdn_out"], wts["w_ssd_out"],
                          wts["w_o"], wts["ln1_g"], wts["ln1_b"], wts["w_router"], wts["b_router"],
                          seq_len, ada_row0)
    y = _moe(h2, gates, x1, ada3, wts["w_exp_gate"], wts["w_exp_up"], wts["w_exp_down"], wts["ln2_g"],
             wts["ln2_b"], seq_len, ada_row0, pos is not None)
    states = (gdn_out[1], ssd_out[1]) if want_state else None
    return y.reshape(bsz, seq_len, D_MODEL), states


def kernel(x_prompt, x_sample, state_gdn, state_ssd, c, c_ctx, w_ada, b_ada, w_in, gdn_conv_w, gdn_conv_b, gdn_a_log, gdn_dt_bias, gdn_norm_w, w_gdn_out, ssd_conv_w, ssd_conv_b, ssd_a_log, ssd_dt_bias, ssd_d, ssd_norm_w, w_ssd_out, w_o, ln1_g, ln1_b, w_router_group, b_router_group, w_router_expert, b_router_expert, w_exp_gate, w_exp_up, w_exp_down, ln2_g, ln2_b):
    assert w_in.shape[0] == DEPTH == 1
    l = 0
    bsz_c = x_prompt.shape[0]
    bsz_l, seq_l, _ = x_sample.shape

    cvec = jnp.zeros((SUBLANES, D_MODEL), F32).at[0].set(c_ctx).at[1:1 + bsz_l].set(c)
    ada3 = _ada(cvec, w_ada[l], b_ada[l]).reshape(SUBLANES, 1, 6 * D_MODEL)

    wi = w_in[l]
    o_zg = 3072
    o_beta = 4096
    o_a = 4112
    o_xbc = 4128
    o_zs = 7200
    o_dt = 9248
    o_gate = 9312
    w_big = jnp.concatenate([wi[:, 0:o_zg], wi[:, o_zg:o_beta], wi[:, o_xbc:o_zs], wi[:, o_zs:o_dt],
                             wi[:, o_gate:]], axis=1).astype(BF16)
    w_beta = wi[:, o_beta:o_a].T.reshape(2, GDN_HEADS, D_MODEL)
    w_a = wi[:, o_a:o_xbc].T.reshape(2, GDN_HEADS, D_MODEL)
    w_head = jnp.concatenate([w_beta, w_a, jnp.zeros((SUBLANES - 4, GDN_HEADS, D_MODEL), F32)], axis=0)
    w_head = jnp.transpose(w_head, (1, 0, 2)).reshape(ROW_DT, D_MODEL)
    w_small_t = jnp.concatenate([w_head, wi[:, o_dt:o_gate].T], axis=0).astype(BF16)
    assert w_small_t.shape == (SMALL_ROWS, D_MODEL)
    gdn_params = jnp.concatenate([gdn_a_log[l], gdn_dt_bias[l], jnp.zeros((SUBLANES - 4, GDN_HEADS), F32)], axis=0)
    gdn_params = _lane_rep(gdn_params.T)

    w_router = jnp.zeros((D_MODEL, LANES), F32)
    w_router = w_router.at[:, 0:N_EXPERTS].set(w_router_expert[l]).at[:, N_EXPERTS:N_EXPERTS + N_GROUPS].set(
        w_router_group[l])
    b_router = jnp.zeros((1, LANES), F32)
    b_router = b_router.at[0, 0:N_EXPERTS].set(b_router_expert[l]).at[0, N_EXPERTS:N_EXPERTS + N_GROUPS].set(
        b_router_group[l])

    wts = {
        "w_big": w_big, "w_small_t": w_small_t,
        "gdn_conv_w": gdn_conv_w[l], "gdn_conv_b": gdn_conv_b[l].reshape(1, -1),
        "gdn_params": gdn_params,
        "gdn_norm_w": gdn_norm_w[l].reshape(1, -1),
        "ssd_conv_w": ssd_conv_w[l], "ssd_conv_b": ssd_conv_b[l].reshape(1, -1),
        "ssd_a_log": _lane_rep(ssd_a_log[l]), "ssd_dt_bias": _lane_rep(ssd_dt_bias[l]),
        "ssd_d": jnp.repeat(ssd_d[l], SSD_P).reshape(1, -1), "ssd_norm_w": ssd_norm_w[l].reshape(1, -1),
        "w_gdn_out": w_gdn_out[l].astype(BF16), "w_ssd_out": w_ssd_out[l].astype(BF16),
        "w_o": w_o[l].astype(BF16),
        "ln1_g": ln1_g[l].reshape(1, -1), "ln1_b": ln1_b[l].reshape(1, -1),
        "w_router": w_router.astype(BF16), "b_router": b_router,
        "w_exp_gate": w_exp_gate[l].astype(BF16), "w_exp_up": w_exp_up[l].astype(BF16),
        "w_exp_down": w_exp_down[l].astype(BF16),
        "ln2_g": ln2_g[l].reshape(1, -1), "ln2_b": ln2_b[l].reshape(1, -1),
    }

    pos = _grid_pos_embed(seq_l, D_MODEL)
    h0 = state_ssd[:, l].reshape(bsz_l, 2, SSD_GROUPS, HEADS_PER_GROUP, SSD_P, SSD_N)
    h0t = jnp.transpose(h0, (0, 1, 2, 5, 3, 4)).reshape(bsz_l, 2, SSD_GROUPS, SSD_N, GROUP_W)

    y_ctx, (s_gdn, h_ssd_t) = _stream(x_prompt, None, ada3, 0, None, None, True, wts)
    y_lat, _ = _stream(x_sample, pos, ada3, 1, state_gdn, h0t, False, wts)

    h_ssd = h_ssd_t.reshape(bsz_c, 2, SSD_GROUPS, SSD_N, HEADS_PER_GROUP, SSD_P)
    new_ssd = jnp.transpose(h_ssd, (0, 1, 2, 4, 5, 3)).reshape(bsz_c, 1, 2, SSD_HEADS, SSD_P, SSD_N)
    return (y_ctx, y_lat, s_gdn, new_ssd)
```

```python
import functools
import math

import jax
import jax.numpy as jnp
import numpy as np
from jax import lax
from jax.experimental import pallas as pl
from jax.experimental.pallas import tpu as pltpu

F32 = jnp.float32
BF16 = jnp.bfloat16
HIGHEST = lax.Precision.HIGHEST

D_MODEL = 1024
GRID_W = 64
POS_BASE = 10000.0
CONV_K = 5
GDN_HEADS = 8
GDN_DK = 128
GDN_DV = 128
SSD_HEADS = 32
SSD_P = 64
SSD_INNER = SSD_HEADS * SSD_P
SSD_GROUPS = 4
SSD_N = 128
HEADS_PER_GROUP = SSD_HEADS // SSD_GROUPS
GROUP_W = HEADS_PER_GROUP * SSD_P
N_GROUPS = 4
EXPERTS_PER_GROUP = 8
N_EXPERTS = N_GROUPS * EXPERTS_PER_GROUP
D_EXPERT = 256
EPS = 1e-6
DEPTH = 1
ALPHA = (2.0 * DEPTH) ** 0.25

LANES = 128
SUBLANES = 8
CHUNK = 128
CONV_PAD = SUBLANES
VMEM_LIMIT = 56 * 1024 * 1024

COL_QKV = 0
COL_ZG = 3072
COL_XBC = 4096
COL_ZS = 7168
COL_GATE = 9216
PROJ_COLS = 11264
PROJ_TN = 1024
GDN_ROW_BETA = 0
GDN_ROW_A = 2
ROW_DT = GDN_HEADS * SUBLANES
SMALL_ROWS = 128

TOK_BLOCK = 2048
POST_BLOCK = 256
MOE_WINDOW = 1024
MOE_TILE = 288
MOE_TILES = -(-MOE_WINDOW // MOE_TILE)
GROUP_LANE = N_EXPERTS
SSD_CHUNKS_PER_STEP = 8


def _bdot(a, b):
    return jnp.dot(a.astype(BF16), b.astype(BF16), preferred_element_type=F32)


def _sigmoid(x):
    return 0.5 * jnp.tanh(0.5 * x) + 0.5


def _silu(x):
    return x * _sigmoid(x)


def _softplus(x):
    return jnp.maximum(x, 0.0) + jnp.log1p(jnp.exp(-jnp.abs(x)))


def _ln(x):
    mu = jnp.mean(x, axis=-1, keepdims=True)
    xc = x - mu
    var = jnp.mean(xc * xc, axis=-1, keepdims=True)
    return xc * lax.rsqrt(var + EPS)


def _lane_scan(x, reverse):
    lane = lax.broadcasted_iota(jnp.int32, x.shape, 1)
    s = 1
    while s < CHUNK:
        if reverse:
            shifted = pltpu.roll(x, CHUNK - s, axis=1)
            x = x + jnp.where(lane < CHUNK - s, shifted, 0.0)
        else:
            shifted = pltpu.roll(x, s, axis=1)
            x = x + jnp.where(lane >= s, shifted, 0.0)
        s *= 2
    return x


def _tri_masks(reverse):
    row = lax.broadcasted_iota(jnp.int32, (CHUNK, CHUNK), 0)
    col = lax.broadcasted_iota(jnp.int32, (CHUNK, CHUNK), 1)
    if reverse:
        return row <= col, row < col
    return row >= col, row > col


def _decay_matrix(acc_row, incl):
    acc_rb = jnp.broadcast_to(acc_row, (CHUNK, CHUNK))
    acc_col = acc_rb.T
    decay = jnp.exp(jnp.where(incl, acc_col - acc_rb, -jnp.inf))
    return decay, acc_col


def _conv_tap_sum(pad_ref, t0, lo, hi, w_ref, b_ref):
    acc = jnp.broadcast_to(b_ref[...], (CHUNK, hi - lo))
    for j in range(CONV_K):
        start = t0 + CONV_PAD - CONV_K // 2 + j
        acc = acc + pad_ref[start:start + CHUNK, lo:hi] * w_ref[j:j + 1, :]
    return acc


def _ada_kernel(c_ref, w_ref, b_ref, o_ref):
    s = _silu(c_ref[...])
    o_ref[...] = jnp.dot(s, w_ref[...], precision=HIGHEST, preferred_element_type=F32) + b_ref[...]


def _ada(cvec, w_ada, b_ada):
    n_out = w_ada.shape[1]
    tn = 1024
    return pl.pallas_call(
        _ada_kernel,
        grid=(n_out // tn,),
        in_specs=[
            pl.BlockSpec((SUBLANES, D_MODEL), lambda j: (0, 0)),
            pl.BlockSpec((D_MODEL, tn), lambda j: (0, j)),
            pl.BlockSpec((1, tn), lambda j: (0, j)),
        ],
        out_specs=pl.BlockSpec((SUBLANES, tn), lambda j: (0, j)),
        out_shape=jax.ShapeDtypeStruct((SUBLANES, n_out), F32),
        compiler_params=pltpu.CompilerParams(dimension_semantics=("arbitrary",), vmem_limit_bytes=VMEM_LIMIT),
        name="ada",
    )(cvec, w_ada, b_ada.reshape(1, n_out))


def _inproj_kernel(*refs, has_pos):
    if has_pos:
        x_ref, pos_ref, ada_ref, w_ref, wst_ref, proj_ref, small_ref, h_ref = refs
    else:
        x_ref, ada_ref, w_ref, wst_ref, proj_ref, small_ref, h_ref = refs
        pos_ref = None

    @pl.when(pl.program_id(1) == 0)
    def _():
        x = x_ref[...]
        if pos_ref is not None:
            x = x + pos_ref[...]
        shift = ada_ref[:, 0:D_MODEL]
        scale = ada_ref[:, D_MODEL:2 * D_MODEL]
        h = (_ln(x) * (1.0 + scale) + shift).astype(BF16)
        h_ref[...] = h
        small_ref[...] = lax.dot_general(wst_ref[...], h, (((1,), (1,)), ((), ())), preferred_element_type=F32)

    proj_ref[...] = jnp.dot(h_ref[...], w_ref[...], preferred_element_type=F32).astype(BF16)


def _in_proj(x2d, pos, ada3, w_big, w_small_t, seq_len, ada_row0):
    n_tok = x2d.shape[0]
    t = TOK_BLOCK
    blocks_per_seq = seq_len // t
    has_pos = pos is not None

    def ada_map(i, j):
        return (ada_row0 + (i // blocks_per_seq if has_pos else 0), 0, 0)

    in_specs = [pl.BlockSpec((t, D_MODEL), lambda i, j: (i, 0))]
    args = [x2d]
    if has_pos:
        pos_mode = dict(pipeline_mode=pl.Buffered(1)) if blocks_per_seq == 1 else {}
        in_specs.append(pl.BlockSpec((t, D_MODEL), lambda i, j: (i % blocks_per_seq, 0), **pos_mode))
        args.append(pos)
    in_specs += [
        pl.BlockSpec((None, 1, 6 * D_MODEL), ada_map),
        pl.BlockSpec((D_MODEL, PROJ_TN), lambda i, j: (0, j)),
        pl.BlockSpec((SMALL_ROWS, D_MODEL), lambda i, j: (0, 0)),
    ]
    args += [ada3, w_big, w_small_t]
    return pl.pallas_call(
        functools.partial(_inproj_kernel, has_pos=has_pos),
        grid=(n_tok // t, PROJ_COLS // PROJ_TN),
        in_specs=in_specs,
        out_specs=[
            pl.BlockSpec((t, PROJ_TN), lambda i, j: (i, j)),
            pl.BlockSpec((SMALL_ROWS, t), lambda i, j: (0, i)),
        ],
        out_shape=[
            jax.ShapeDtypeStruct((n_tok, PROJ_COLS), BF16),
            jax.ShapeDtypeStruct((SMALL_ROWS, n_tok), F32),
        ],
        scratch_shapes=[pltpu.VMEM((t, D_MODEL), BF16)],
        compiler_params=pltpu.CompilerParams(dimension_semantics=("arbitrary", "arbitrary"),
                                             vmem_limit_bytes=VMEM_LIMIT),
        name="in_proj",
    )(*args)


INV_BASE = 8
GDN_GROUP = 8


def _take_blocks(m, size, odd):
    parts = [m[k * size:(k + 1) * size] for k in range(CHUNK // size) if (k % 2 == 1) == odd]
    return parts[0] if len(parts) == 1 else jnp.concatenate(parts, axis=0)


def _interleave_blocks(even_rows, odd_rows, size):
    parts = []
    for k in range(CHUNK // (2 * size)):
        parts.append(even_rows[k * size:(k + 1) * size])
        parts.append(odd_rows[k * size:(k + 1) * size])
    return jnp.concatenate(parts, axis=0)


def _unit_tri_inverses(nmats, uppers):
    row = lax.broadcasted_iota(jnp.int32, (CHUNK, CHUNK), 0)
    col = lax.broadcasted_iota(jnp.int32, (CHUNK, CHUNK), 1)

    def same_block(size):
        shift = int(math.log2(size))
        return (row >> shift) == (col >> shift)

    eye = (row == col).astype(F32)
    base = same_block(INV_BASE)
    nds = [jnp.where(base, n, 0.0) for n in nmats]
    xs = [eye - nd for nd in nds]
    pws = [_bdot(nd, nd) for nd in nds]
    size = 2
    while True:
        xs = [x + _bdot(x, pw) for x, pw in zip(xs, pws)]
        size *= 2
        if size >= INV_BASE:
            break
        pws = [_bdot(pw, pw) for pw in pws]
    size = INV_BASE
    while size < CHUNK:
        coupling = same_block(2 * size) & jnp.logical_not(same_block(size))
        offs = [_take_blocks(jnp.where(coupling, n, 0.0), size, odd=not up) for n, up in zip(nmats, uppers)]
        tmps = [_bdot(off, x) for off, x in zip(offs, xs)]
        zeros = jnp.zeros((CHUNK // 2, CHUNK), F32)
        fulls = [_interleave_blocks(t, zeros, size) if up else _interleave_blocks(zeros, t, size)
                 for t, up in zip(tmps, uppers)]
        moved = [_take_blocks(x, size, odd=not up) for x, up in zip(xs, uppers)]
        kept = [_take_blocks(x, size, odd=up) for x, up in zip(xs, uppers)]
        news = [m - _bdot(m, f) for m, f in zip(moved, fulls)]
        xs = [_interleave_blocks(nw, kp, size) if up else _interleave_blocks(kp, nw, size)
              for nw, kp, up in zip(news, kept, uppers)]
        size *= 2
    return xs


def _gdn_prepare(chunks, qn_ref, kn_ref, vn_ref, kt_ref, sc_ref, lhs_ref, sb_ref, ob_ref, eg_ref):
    grams = [_bdot(jnp.concatenate([kn_ref[g], qn_ref[g]], axis=0), kt_ref[g]) for g in chunks]
    units = []
    for g, gram in zip(chunks, grams):
        for d in range(2):
            reverse = d == 1
            incl, strict = _tri_masks(reverse)
            beta = sc_ref[g, d:d + 1, :]
            gc = sc_ref[g, 2 + d:3 + d, :]
            decay, gcol = _decay_matrix(gc, incl)
            beta_rb = jnp.broadcast_to(beta, (CHUNK, CHUNK))
            glast = gc[:, 0:1] if reverse else gc[:, CHUNK - 1:CHUNK]
            sc_ref[g, 4 + d:5 + d, :] = jnp.broadcast_to(jnp.exp(glast), (1, CHUNK))
            eg_ref[g, d] = jnp.exp(gcol)
            units.append(dict(
                g=g, d=d,
                nmat=jnp.where(strict, gram[0:CHUNK] * decay, 0.0) * beta_rb,
                att=gram[CHUNK:2 * CHUNK] * decay * beta_rb,
                kd_scale=beta * jnp.exp(glast - gc)))
    xinvs = _unit_tri_inverses([u["nmat"] for u in units], [u["d"] == 1 for u in units])
    uws = [_bdot(x, jnp.concatenate([vn_ref[u["g"]], kn_ref[u["g"]] * eg_ref[u["g"], u["d"]]], axis=1))
           for u, x in zip(units, xinvs)]
    mixeds = []
    for u, uw in zip(units, uws):
        kd = kt_ref[u["g"]] * jnp.broadcast_to(u["kd_scale"], (CHUNK, CHUNK))
        mixeds.append(_bdot(jnp.concatenate([kd, u["att"]], axis=0), uw))
    for u, mixed in zip(units, mixeds):
        g, d = u["g"], u["d"]
        lhs_ref[g, d, 0:CHUNK, :] = mixed[0:CHUNK, GDN_DV:].astype(BF16)
        lhs_ref[g, d, CHUNK:2 * CHUNK, :] = qn_ref[g].astype(BF16)
        lhs_ref[g, d, 2 * CHUNK:3 * CHUNK, :] = mixed[CHUNK:, GDN_DV:].astype(BF16)
        sb_ref[g, d] = mixed[0:CHUNK, 0:GDN_DV]
        ob_ref[g, d] = mixed[CHUNK:, 0:GDN_DV]


def _gdn_steps(steps, lhs_ref, sb_ref, ob_ref, eg_ref, sc_ref, s_ref, o_refs):
    states = [s_ref[s, d] for _, s, d in steps]
    rs = [jnp.dot(lhs_ref[g, d], st.astype(BF16), preferred_element_type=F32)
          for (g, _, d), st in zip(steps, states)]
    for (g, s, d), st, r in zip(steps, states, rs):
        s_ref[s, d] = st * sc_ref[g, 4 + d:5 + d, :] - r[0:CHUNK] + sb_ref[g, d]
        o_refs[d][g] = eg_ref[g, d] * r[CHUNK:2 * CHUNK] - r[2 * CHUNK:3 * CHUNK] + ob_ref[g, d]


def _gdn_kernel(*refs, n_seq, n_chunks, has_s0, want_state):
    it = iter(refs)
    q_ref, k_ref, v_ref, z_ref, sm_ref = (next(it) for _ in range(5))
    wq_ref, wk_ref, wv_ref, bq_ref, bk_ref, bv_ref = (next(it) for _ in range(6))
    par_ref, nw_ref = (next(it) for _ in range(2))
    s0_ref = next(it) if has_s0 else None
    og_ref = next(it)
    sout_ref = next(it) if want_state else None
    pad_ref, qn_ref, kn_ref, vn_ref, kt_ref, of_ref, ob_ref, sc_ref, s_ref = (next(it) for _ in range(9))
    lhs_ref, sb_ref, ou_ref, eg_ref, st_ref = (next(it) for _ in range(5))

    seq = n_chunks * CHUNK
    total = n_seq * n_chunks
    zero_rows = jnp.zeros((CONV_PAD, 3 * LANES), F32)
    for s in range(n_seq):
        pad_ref[s, 0:CONV_PAD, :] = zero_rows
        pad_ref[s, CONV_PAD + seq:2 * CONV_PAD + seq, :] = zero_rows
        for c in range(n_chunks):
            t0 = c * CHUNK
            r0 = s * seq + t0
            pad_ref[s, CONV_PAD + t0:CONV_PAD + t0 + CHUNK, 0:LANES] = q_ref[r0:r0 + CHUNK, :].astype(F32)
            pad_ref[s, CONV_PAD + t0:CONV_PAD + t0 + CHUNK, LANES:2 * LANES] = k_ref[r0:r0 + CHUNK, :].astype(F32)
            pad_ref[s, CONV_PAD + t0:CONV_PAD + t0 + CHUNK, 2 * LANES:3 * LANES] = v_ref[r0:r0 + CHUNK, :].astype(F32)

    a_neg = [-jnp.exp(par_ref[d:d + 1, :]) for d in range(2)]
    dt_bias = [par_ref[2 + d:3 + d, :] for d in range(2)]
    for s in range(n_seq):
        for c in range(n_chunks):
            t0 = c * CHUNK
            r0 = s * seq + t0
            g = s * n_chunks + c
            pad_s = pad_ref.at[s]
            qa = _silu(_conv_tap_sum(pad_s, t0, 0, LANES, wq_ref, bq_ref))
            ka = _silu(_conv_tap_sum(pad_s, t0, LANES, 2 * LANES, wk_ref, bk_ref))
            va = _silu(_conv_tap_sum(pad_s, t0, 2 * LANES, 3 * LANES, wv_ref, bv_ref))
            qn = qa * lax.rsqrt(jnp.sum(qa * qa, axis=-1, keepdims=True) + EPS) * (GDN_DK ** -0.5)
            kn = ka * lax.rsqrt(jnp.sum(ka * ka, axis=-1, keepdims=True) + EPS)
            qn_ref[g] = qn
            kn_ref[g] = kn
            vn_ref[g] = va
            kt_ref[g] = kn.T

    for g in range(total):
        r0 = g * CHUNK
        for d in range(2):
            st_ref[d, g:g + 1, :] = sm_ref[GDN_ROW_A + d:GDN_ROW_A + d + 1, r0:r0 + CHUNK]
            st_ref[2 + d, g:g + 1, :] = sm_ref[GDN_ROW_BETA + d:GDN_ROW_BETA + d + 1, r0:r0 + CHUNK]
    for d in range(2):
        gcum = _lane_scan(a_neg[d] * _softplus(st_ref[d] + dt_bias[d]), reverse=d == 1)
        beta = _sigmoid(st_ref[2 + d])
        for g in range(total):
            sc_ref[g, d:d + 1, :] = beta[g:g + 1, :]
            sc_ref[g, 2 + d:3 + d, :] = gcum[g:g + 1, :]

    def prepare(i, carry):
        _gdn_prepare([i * GDN_GROUP + j for j in range(GDN_GROUP)], qn_ref, kn_ref, vn_ref, kt_ref, sc_ref,
                     lhs_ref, sb_ref, ou_ref, eg_ref)
        return carry

    lax.fori_loop(0, total // GDN_GROUP, prepare, 0)

    if has_s0:
        s_ref[...] = s0_ref[...]
    else:
        s_ref[...] = jnp.zeros(s_ref.shape, F32)

    def advance(i, carry):
        steps = []
        for s in range(n_seq):
            steps.append((s * n_chunks + i, s, 0))
            steps.append((s * n_chunks + n_chunks - 1 - i, s, 1))
        _gdn_steps(steps, lhs_ref, sb_ref, ou_ref, eg_ref, sc_ref, s_ref, (of_ref, ob_ref))
        return carry

    lax.fori_loop(0, n_chunks, advance, 0)

    if want_state:
        sout_ref[...] = s_ref[...]
    for g in range(total):
        r0 = g * CHUNK
        o = of_ref[g] + ob_ref[g]
        o = o * lax.rsqrt(jnp.mean(o * o, axis=-1, keepdims=True) + EPS)
        og_ref[r0:r0 + CHUNK, :] = (o * nw_ref[...] * _silu(z_ref[r0:r0 + CHUNK, :].astype(F32))).astype(BF16)


def _gdn(proj, small_t, conv_w, conv_b, head_params, norm_w, s0, bsz, seq_len, want_state):
    n_chunks = seq_len // CHUNK
    n_tok = bsz * seq_len
    has_s0 = s0 is not None
    n_seq = max(1, GDN_GROUP * 2 // n_chunks)
    assert bsz % n_seq == 0 and (n_seq * n_chunks) % GDN_GROUP == 0
    total = n_seq * n_chunks
    rows = n_seq * seq_len
    col = lambda off: (lambda b, h: (b, off + h))
    cw = lambda off: (lambda b, h: (0, off + h))
    in_specs = [
        pl.BlockSpec((rows, LANES), col(COL_QKV // LANES)),
        pl.BlockSpec((rows, LANES), col(COL_QKV // LANES + GDN_HEADS)),
        pl.BlockSpec((rows, LANES), col(COL_QKV // LANES + 2 * GDN_HEADS)),
        pl.BlockSpec((rows, LANES), col(COL_ZG // LANES)),
        pl.BlockSpec((SUBLANES, rows), lambda b, h: (h, b)),
        pl.BlockSpec((CONV_K, LANES), cw(0)),
        pl.BlockSpec((CONV_K, LANES), cw(GDN_HEADS)),
        pl.BlockSpec((CONV_K, LANES), cw(2 * GDN_HEADS)),
        pl.BlockSpec((1, LANES), cw(0)),
        pl.BlockSpec((1, LANES), cw(GDN_HEADS)),
        pl.BlockSpec((1, LANES), cw(2 * GDN_HEADS)),
        pl.BlockSpec((None, SUBLANES, LANES), lambda b, h: (h, 0, 0)),
        pl.BlockSpec((1, LANES), lambda b, h: (0, 0)),
    ]
    args = [proj, proj, proj, proj, small_t, conv_w, conv_w, conv_w, conv_b, conv_b, conv_b,
            head_params, norm_w]
    state_spec = pl.BlockSpec((n_seq, None, 2, None, GDN_DK, GDN_DV), lambda b, h: (b, 0, 0, h, 0, 0))
    if has_s0:
        in_specs.append(state_spec)
        args.append(s0)
    out_specs = [pl.BlockSpec((rows, LANES), lambda b, h: (b, h))]
    out_shape = [jax.ShapeDtypeStruct((n_tok, GDN_HEADS * GDN_DV), BF16)]
    if want_state:
        out_specs.append(state_spec)
        out_shape.append(jax.ShapeDtypeStruct((bsz, 1, 2, GDN_HEADS, GDN_DK, GDN_DV), F32))
    chunked = pltpu.VMEM((total, CHUNK, LANES), F32)
    per_dir = pltpu.VMEM((total, 2, CHUNK, LANES), F32)
    return pl.pallas_call(
        functools.partial(_gdn_kernel, n_seq=n_seq, n_chunks=n_chunks, has_s0=has_s0, want_state=want_state),
        grid=(bsz // n_seq, GDN_HEADS),
        in_specs=in_specs,
        out_specs=out_specs,
        out_shape=out_shape,
        scratch_shapes=[
            pltpu.VMEM((n_seq, seq_len + 2 * CONV_PAD, 3 * LANES), F32),
            chunked, chunked, chunked, chunked, chunked, chunked,
            pltpu.VMEM((total, SUBLANES, CHUNK), F32),
            pltpu.VMEM((n_seq, 2, GDN_DK, GDN_DV), F32),
            pltpu.VMEM((total, 2, 3 * CHUNK, GDN_DK), BF16),
            per_dir, per_dir, per_dir,
            pltpu.VMEM((4, total, CHUNK), F32),
        ],
        compiler_params=pltpu.CompilerParams(dimension_semantics=("arbitrary", "arbitrary"),
                                             vmem_limit_bytes=VMEM_LIMIT),
        name="gdn",
    )(*args)


def _ssd_steps(steps, xs_ref, cs_ref, bt_ref, dt_ref, ac_ref, ht_ref, y_ref):
    lane = lax.broadcasted_iota(jnp.int32, (CHUNK, LANES), 1)
    low_half = lane < SSD_P
    shared = [(_bdot(cs_ref[c], bt_ref[c]), _bdot(cs_ref[c], ht_ref[s, d])) for c, s, d in steps]
    pairs_per_stage = 2
    for p0 in range(0, HEADS_PER_GROUP // 2, pairs_per_stage):
        units = []
        for (c, s, d), (cb, y_off) in zip(steps, shared):
            reverse = d == 1
            incl, _ = _tri_masks(reverse)
            bt = bt_ref[c]
            for p in range(p0, p0 + pairs_per_stage):
                halves = []
                for r in (2 * p, 2 * p + 1):
                    row = d * HEADS_PER_GROUP + r
                    dt = dt_ref[c, row:row + 1, :]
                    ac = ac_ref[c, row:row + 1, :]
                    decay, acol = _decay_matrix(ac, incl)
                    last = ac[:, 0:1] if reverse else ac[:, CHUNK - 1:CHUNK]
                    halves.append(dict(
                        m=(cb * decay * jnp.broadcast_to(dt, (CHUNK, CHUNK))).astype(BF16),
                        bts=(bt * jnp.broadcast_to(dt * jnp.exp(last - ac), (SSD_N, CHUNK))).astype(BF16),
                        acol=acol, elast=jnp.exp(last)))
                units.append(dict(c=c, s=s, d=d, p=p, halves=halves, y_off=y_off[:, p * LANES:(p + 1) * LANES]))
        for u in units:
            x = xs_ref[u["c"], :, u["p"] * LANES:(u["p"] + 1) * LANES]
            zero = jnp.zeros_like(x)
            xh = (jnp.where(low_half, x, zero), jnp.where(low_half, zero, x))
            u["yd"] = [jnp.dot(hf["m"], xm, preferred_element_type=F32) for hf, xm in zip(u["halves"], xh)]
            u["st"] = [jnp.dot(hf["bts"], xm, preferred_element_type=F32) for hf, xm in zip(u["halves"], xh)]
        for u in units:
            c, s, d, p = u["c"], u["s"], u["d"], u["p"]
            h0, h1 = u["halves"]
            eoff = jnp.exp(jnp.where(low_half, h0["acol"], h1["acol"]))
            elast = jnp.where(low_half, h0["elast"], h1["elast"])
            cols = slice(p * LANES, (p + 1) * LANES)
            y_ref[c, :, cols] = y_ref[c, :, cols] + (u["yd"][0] + u["yd"][1]) + eoff * u["y_off"]
            ht_ref[s, d, :, cols] = ht_ref[s, d, :, cols] * elast + (u["st"][0] + u["st"][1])


def _ssd_kernel(*refs, n_seq, n_chunks, has_h0, want_state):
    it = iter(refs)
    x_ref, b_ref, c_ref, z_ref, dtf_ref, dtb_ref = (next(it) for _ in range(6))
    wx_ref, wb_ref, wc_ref, bx_ref, bb_ref, bc_ref = (next(it) for _ in range(6))
    alog_ref, dtbias_ref, dvec_ref, nw_ref = (next(it) for _ in range(4))
    h0_ref = next(it) if has_h0 else None
    yg_ref = next(it)
    hout_ref = next(it) if want_state else None
    pad_ref, xs_ref, cs_ref, bt_ref, dt_ref, ac_ref, y_ref, ht_ref = (next(it) for _ in range(8))

    seq = n_chunks * CHUNK
    total = n_seq * n_chunks
    width = GROUP_W + 2 * SSD_N
    zero_rows = jnp.zeros((CONV_PAD, width), F32)
    for s in range(n_seq):
        pad_ref[s, 0:CONV_PAD, :] = zero_rows
        pad_ref[s, CONV_PAD + seq:2 * CONV_PAD + seq, :] = zero_rows
        for c in range(n_chunks):
            t0 = c * CHUNK
            r0 = s * seq + t0
            pad_ref[s, CONV_PAD + t0:CONV_PAD + t0 + CHUNK, 0:GROUP_W] = x_ref[r0:r0 + CHUNK, :].astype(F32)
            pad_ref[s, CONV_PAD + t0:CONV_PAD + t0 + CHUNK, GROUP_W:GROUP_W + SSD_N] = b_ref[r0:r0 + CHUNK, :].astype(F32)
            pad_ref[s, CONV_PAD + t0:CONV_PAD + t0 + CHUNK, GROUP_W + SSD_N:width] = c_ref[r0:r0 + CHUNK, :].astype(F32)

    raw_dt = (dtf_ref, dtb_ref)
    for s in range(n_seq):
        pad_s = pad_ref.at[s]
        for c in range(n_chunks):
            t0 = c * CHUNK
            g = s * n_chunks + c
            for lo in range(0, GROUP_W, LANES):
                cols = slice(lo, lo + LANES)
                xa = _silu(_conv_tap_sum(pad_s, t0, lo, lo + LANES, wx_ref.at[:, cols], bx_ref.at[:, cols]))
                xs_ref[g, :, cols] = xa.astype(BF16)
                y_ref[g, :, cols] = xa * dvec_ref[:, cols]
            ba = _silu(_conv_tap_sum(pad_s, t0, GROUP_W, GROUP_W + SSD_N, wb_ref, bb_ref))
            ca = _silu(_conv_tap_sum(pad_s, t0, GROUP_W + SSD_N, width, wc_ref, bc_ref))
            cs_ref[g] = ca
            bt_ref[g] = ba.T
            for d in range(2):
                dt_ref[g, d * HEADS_PER_GROUP:(d + 1) * HEADS_PER_GROUP, :] = raw_dt[d][:, g * CHUNK:(g + 1) * CHUNK]

    for d in range(2):
        lo, hi = d * HEADS_PER_GROUP, (d + 1) * HEADS_PER_GROUP
        dt = _softplus(dt_ref[:, lo:hi, :] + dtbias_ref[d][None])
        dt_ref[:, lo:hi, :] = dt
        scaled = (dt * (-jnp.exp(alog_ref[d]))[None]).reshape(total * HEADS_PER_GROUP, CHUNK)
        ac_ref[:, lo:hi, :] = _lane_scan(scaled, reverse=d == 1).reshape(total, HEADS_PER_GROUP, CHUNK)

    if has_h0:
        ht_ref[...] = h0_ref[...]
    else:
        ht_ref[...] = jnp.zeros(ht_ref.shape, F32)

    def body(i, carry):
        for s in range(n_seq):
            _ssd_steps([(s * n_chunks + i, s, 0), (s * n_chunks + n_chunks - 1 - i, s, 1)],
                       xs_ref, cs_ref, bt_ref, dt_ref, ac_ref, ht_ref, y_ref)
        return carry

    lax.fori_loop(0, n_chunks, body, 0)

    if want_state:
        hout_ref[...] = ht_ref[...]
    for g in range(total):
        r0 = g * CHUNK
        ssq = jnp.zeros((CHUNK, 1), F32)
        for lo in range(0, GROUP_W, LANES):
            cols = slice(lo, lo + LANES)
            y = y_ref[g, :, cols] * _silu(z_ref[r0:r0 + CHUNK, cols].astype(F32))
            y_ref[g, :, cols] = y
            ssq = ssq + jnp.sum(y * y, axis=-1, keepdims=True)
        inv = lax.rsqrt(ssq * (1.0 / GROUP_W) + EPS)
        for lo in range(0, GROUP_W, LANES):
            cols = slice(lo, lo + LANES)
            yg_ref[r0:r0 + CHUNK, cols] = (y_ref[g, :, cols] * inv * nw_ref[:, cols]).astype(BF16)


def _ssd(proj, small_t, conv_w, conv_b, a_log_rep, dt_bias_rep, d_vec, norm_w, h0t, bsz, seq_len, want_state):
    n_chunks = seq_len // CHUNK
    n_tok = bsz * seq_len
    has_h0 = h0t is not None
    n_seq = max(1, SSD_CHUNKS_PER_STEP // n_chunks)
    assert bsz % n_seq == 0
    total = n_seq * n_chunks
    rows_per_step = n_seq * seq_len
    xbc_w = COL_XBC // GROUP_W
    bc_l = (COL_XBC + SSD_INNER) // LANES
    in_specs = [
        pl.BlockSpec((rows_per_step, GROUP_W), lambda b, g: (b, xbc_w + g)),
        pl.BlockSpec((rows_per_step, SSD_N), lambda b, g: (b, bc_l + g)),
        pl.BlockSpec((rows_per_step, SSD_N), lambda b, g: (b, bc_l + SSD_GROUPS + g)),
        pl.BlockSpec((rows_per_step, GROUP_W), lambda b, g: (b, COL_ZS // GROUP_W + g)),
        pl.BlockSpec((HEADS_PER_GROUP, rows_per_step), lambda b, g: (ROW_DT // HEADS_PER_GROUP + g, b)),
        pl.BlockSpec((HEADS_PER_GROUP, rows_per_step),
                     lambda b, g: ((ROW_DT + SSD_HEADS) // HEADS_PER_GROUP + g, b)),
        pl.BlockSpec((CONV_K, GROUP_W), lambda b, g: (0, g)),
        pl.BlockSpec((CONV_K, SSD_N), lambda b, g: (0, SSD_INNER // SSD_N + g)),
        pl.BlockSpec((CONV_K, SSD_N), lambda b, g: (0, SSD_INNER // SSD_N + SSD_GROUPS + g)),
        pl.BlockSpec((1, GROUP_W), lambda b, g: (0, g)),
        pl.BlockSpec((1, SSD_N), lambda b, g: (0, SSD_INNER // SSD_N + g)),
        pl.BlockSpec((1, SSD_N), lambda b, g: (0, SSD_INNER // SSD_N + SSD_GROUPS + g)),
        pl.BlockSpec((2, HEADS_PER_GROUP, LANES), lambda b, g: (0, g, 0)),
        pl.BlockSpec((2, HEADS_PER_GROUP, LANES), lambda b, g: (0, g, 0)),
        pl.BlockSpec((1, GROUP_W), lambda b, g: (0, g)),
        pl.BlockSpec((1, GROUP_W), lambda b, g: (0, g)),
    ]
    args = [proj, proj, proj, proj, small_t, small_t, conv_w, conv_w, conv_w, conv_b, conv_b, conv_b,
            a_log_rep, dt_bias_rep, d_vec, norm_w]
    state_spec = pl.BlockSpec((n_seq, 2, None, SSD_N, GROUP_W), lambda b, g: (b, 0, g, 0, 0))
    if has_h0:
        in_specs.append(state_spec)
        args.append(h0t)
    out_specs = [pl.BlockSpec((rows_per_step, GROUP_W), lambda b, g: (b, g))]
    out_shape = [jax.ShapeDtypeStruct((n_tok, SSD_INNER), BF16)]
    if want_state:
        out_specs.append(state_spec)
        out_shape.append(jax.ShapeDtypeStruct((bsz, 2, SSD_GROUPS, SSD_N, GROUP_W), F32))
    rows = pltpu.VMEM((total, 2 * HEADS_PER_GROUP, CHUNK), F32)
    return pl.pallas_call(
        functools.partial(_ssd_kernel, n_seq=n_seq, n_chunks=n_chunks, has_h0=has_h0, want_state=want_state),
        grid=(bsz // n_seq, SSD_GROUPS),
        in_specs=in_specs,
        out_specs=out_specs,
        out_shape=out_shape,
        scratch_shapes=[
            pltpu.VMEM((n_seq, seq_len + 2 * CONV_PAD, GROUP_W + 2 * SSD_N), F32),
            pltpu.VMEM((total, CHUNK, GROUP_W), BF16),
            pltpu.VMEM((total, CHUNK, SSD_N), F32),
            pltpu.VMEM((total, SSD_N, CHUNK), F32),
            rows, rows,
            pltpu.VMEM((total, CHUNK, GROUP_W), F32),
            pltpu.VMEM((n_seq, 2, SSD_N, GROUP_W), F32),
        ],
        compiler_params=pltpu.CompilerParams(dimension_semantics=("arbitrary", "arbitrary"),
                                             vmem_limit_bytes=VMEM_LIMIT),
        name="ssd",
    )(*args)


def _route(logits):
    lane = lax.broadcasted_iota(jnp.int32, logits.shape, 1)
    neg = -jnp.inf
    is_grp = (lane >= N_EXPERTS) & (lane < N_EXPERTS + N_GROUPS)
    gl = jnp.where(is_grp, logits, neg)
    gmax = jnp.max(gl, axis=-1, keepdims=True)
    ge = jnp.exp(gl - gmax)
    p_grp = ge / jnp.sum(ge, axis=-1, keepdims=True)
    p_top = jnp.max(p_grp, axis=-1, keepdims=True)
    g_idx = jnp.min(jnp.where(is_grp & (p_grp == p_top), lane, 2 * LANES), axis=-1, keepdims=True) - N_EXPERTS
    in_grp = (lane >= g_idx * EXPERTS_PER_GROUP) & (lane < (g_idx + 1) * EXPERTS_PER_GROUP)
    el = jnp.where(in_grp, logits, neg)
    emax = jnp.max(el, axis=-1, keepdims=True)
    ee = jnp.exp(el - emax)
    p_e = ee / jnp.sum(ee, axis=-1, keepdims=True)
    w1 = jnp.max(p_e, axis=-1, keepdims=True)
    i1 = jnp.min(jnp.where(in_grp & (p_e == w1), lane, 2 * LANES), axis=-1, keepdims=True)
    rest = jnp.where(in_grp & (lane != i1), p_e, -1.0)
    w2 = jnp.max(rest, axis=-1, keepdims=True)
    i2 = jnp.min(jnp.where(rest == w2, lane, 2 * LANES), axis=-1, keepdims=True)
    tot = w1 + w2
    gates = jnp.where(lane == i1, w1 / tot * p_top, 0.0) + jnp.where(lane == i2, w2 / tot * p_top, 0.0)
    return jnp.where(lane == GROUP_LANE, g_idx.astype(F32), gates)


def _post_kernel(*refs, has_pos):
    it = iter(refs)
    x_ref = next(it)
    pos_ref = next(it) if has_pos else None
    og_ref, yg_ref, gg_ref, gs_ref, ada_ref = (next(it) for _ in range(5))
    wg_ref, ws_ref, wo_ref, g1_ref, b1_ref, wr_ref, br_ref = (next(it) for _ in range(7))
    x1_ref, h2_ref, gates_ref = (next(it) for _ in range(3))

    x = x_ref[...]
    if has_pos:
        x = x + pos_ref[...]
    u_g = jnp.dot(og_ref[...], wg_ref[...], preferred_element_type=F32)
    u_s = jnp.dot(yg_ref[...], ws_ref[...], preferred_element_type=F32)
    m = _sigmoid(gg_ref[...].astype(F32)) * u_g + _sigmoid(gs_ref[...].astype(F32)) * u_s
    mix = jnp.dot(m.astype(BF16), wo_ref[...], preferred_element_type=F32)
    gate1 = ada_ref[:, 2 * D_MODEL:3 * D_MODEL]
    shift2 = ada_ref[:, 3 * D_MODEL:4 * D_MODEL]
    scale2 = ada_ref[:, 4 * D_MODEL:5 * D_MODEL]
    x1 = _ln(ALPHA * x + gate1 * mix) * g1_ref[...] + b1_ref[...]
    x1_ref[...] = x1
    h2 = _ln(x1) * (1.0 + scale2) + shift2
    h2b = h2.astype(BF16)
    h2_ref[...] = h2b
    logits = jnp.dot(h2b, wr_ref[...], preferred_element_type=F32) + br_ref[...]
    gates_ref[...] = _route(logits)


def _post(x2d, pos, og, yg, proj, ada3, w_gdn_out, w_ssd_out, w_o, ln_g, ln_b, w_router, b_router,
          seq_len, ada_row0):
    n_tok = x2d.shape[0]
    t = POST_BLOCK
    blocks_per_seq = seq_len // t
    has_pos = pos is not None

    def ada_map(i):
        return (ada_row0 + (i // blocks_per_seq if has_pos else 0), 0, 0)

    const = lambda i: (0, 0)
    in_specs = [pl.BlockSpec((t, D_MODEL), lambda i: (i, 0))]
    args = [x2d]
    if has_pos:
        in_specs.append(pl.BlockSpec((t, D_MODEL), lambda i: (i % blocks_per_seq, 0)))
        args.append(pos)
    in_specs += [
        pl.BlockSpec((t, GDN_HEADS * GDN_DV), lambda i: (i, 0)),
        pl.BlockSpec((t, SSD_INNER), lambda i: (i, 0)),
        pl.BlockSpec((t, D_MODEL), lambda i: (i, COL_GATE // D_MODEL)),
        pl.BlockSpec((t, D_MODEL), lambda i: (i, COL_GATE // D_MODEL + 1)),
        pl.BlockSpec((None, 1, 6 * D_MODEL), ada_map),
        pl.BlockSpec((GDN_HEADS * GDN_DV, D_MODEL), const),
        pl.BlockSpec((SSD_INNER, D_MODEL), const),
        pl.BlockSpec((D_MODEL, D_MODEL), const),
        pl.BlockSpec((1, D_MODEL), const),
        pl.BlockSpec((1, D_MODEL), const),
        pl.BlockSpec((D_MODEL, LANES), const),
        pl.BlockSpec((1, LANES), const),
    ]
    args += [og, yg, proj, proj, ada3, w_gdn_out, w_ssd_out, w_o, ln_g, ln_b, w_router, b_router]
    return pl.pallas_call(
        functools.partial(_post_kernel, has_pos=has_pos),
        grid=(n_tok // t,),
        in_specs=in_specs,
        out_specs=[
            pl.BlockSpec((t, D_MODEL), lambda i: (i, 0)),
            pl.BlockSpec((t, D_MODEL), lambda i: (i, 0)),
            pl.BlockSpec((t, LANES), lambda i: (i, 0)),
        ],
        out_shape=[
            jax.ShapeDtypeStruct((n_tok, D_MODEL), F32),
            jax.ShapeDtypeStruct((n_tok, D_MODEL), BF16),
            jax.ShapeDtypeStruct((n_tok, LANES), F32),
        ],
        compiler_params=pltpu.CompilerParams(dimension_semantics=("arbitrary",), vmem_limit_bytes=VMEM_LIMIT),
        name="post",
    )(*args)


def _moe_kernel(h_ref, gates_ref, x1_ref, ada_ref, wg_ref, wu_ref, wd_ref, g2_ref, b2_ref, out_ref,
                col_ref, row_ref, gx_ref, cnt_ref):
    g = pl.program_id(1)
    t = MOE_WINDOW
    lane = lax.broadcasted_iota(jnp.int32, (t, LANES), 1)

    @pl.when(g == 0)
    def _():
        gates = gates_ref[...]
        grp = jnp.sum(jnp.where(lane == GROUP_LANE, gates, 0.0), axis=-1, keepdims=True)
        onehot = jnp.where((lane < N_GROUPS) & (grp == lane.astype(F32)), 1.0, 0.0)
        tri = (lax.broadcasted_iota(jnp.int32, (t, t), 0) >= lax.broadcasted_iota(jnp.int32, (t, t), 1))
        cum = jnp.dot(jnp.where(tri, 1.0, 0.0).astype(BF16), onehot.astype(BF16), preferred_element_type=F32)
        rank = jnp.sum(onehot * cum, axis=-1, keepdims=True) - 1.0
        info = jnp.where(lane == 0, grp, jnp.where(lane == 1, rank, 0.0))
        col_ref[...] = info
        row_ref[...] = info.T[0:SUBLANES, :]
        totals = cum[t - 1:t, :]
        for k in range(N_GROUPS):
            cnt_ref[k] = totals[0, k].astype(jnp.int32)
        hi = gates.astype(BF16).astype(F32)
        mid = (gates - hi).astype(BF16).astype(F32)
        low = gates - hi - mid
        packed = jnp.where(lane < N_EXPERTS, hi,
                           jnp.where(lane < 2 * N_EXPERTS, pltpu.roll(mid, N_EXPERTS, axis=1),
                                     jnp.where(lane < 3 * N_EXPERTS, pltpu.roll(low, 2 * N_EXPERTS, axis=1), 0.0)))
        gx_ref[...] = packed.astype(BF16)
        out_ref[...] = jnp.zeros(out_ref.shape, F32)

    n_g = cnt_ref[g]
    gf = g.astype(F32)
    for j in range(MOE_TILES):
        @pl.when(j * MOE_TILE < n_g)
        def _(j=j):
            slot_r = (lax.broadcasted_iota(jnp.int32, (MOE_TILE, t), 0) + j * MOE_TILE).astype(F32)
            pick = jnp.where((row_ref[1:2, :] == slot_r) & (row_ref[0:1, :] == gf), 1.0, 0.0).astype(BF16)
            hs = jnp.dot(pick, h_ref[...], preferred_element_type=F32).astype(BF16)
            gsx = jnp.dot(pick, gx_ref[...], preferred_element_type=F32)
            gs = gsx + pltpu.roll(gsx, LANES - N_EXPERTS, axis=1) + pltpu.roll(gsx, LANES - 2 * N_EXPERTS, axis=1)
            lane_t = lax.broadcasted_iota(jnp.int32, (MOE_TILE, LANES), 1)
            y = jnp.zeros((MOE_TILE, D_MODEL), F32)
            for e in range(EXPERTS_PER_GROUP):
                a = jnp.dot(hs, wg_ref[e], preferred_element_type=F32)
                b = jnp.dot(hs, wu_ref[e], preferred_element_type=F32)
                gate_e = jnp.sum(jnp.where(lane_t == g * EXPERTS_PER_GROUP + e, gs, 0.0), axis=-1, keepdims=True)
                y = y + jnp.dot((_silu(a) * b * gate_e).astype(BF16), wd_ref[e], preferred_element_type=F32)
            slot_c = (lax.broadcasted_iota(jnp.int32, (t, MOE_TILE), 1) + j * MOE_TILE).astype(F32)
            place = jnp.where((col_ref[:, 1:2] == slot_c) & (col_ref[:, 0:1] == gf), 1.0, 0.0).astype(BF16)
            out_ref[...] += jnp.dot(place, y.astype(BF16), preferred_element_type=F32)

    @pl.when(g == N_GROUPS - 1)
    def _():
        gate2 = ada_ref[:, 5 * D_MODEL:6 * D_MODEL]
        out_ref[...] = _ln(ALPHA * x1_ref[...] + gate2 * out_ref[...]) * g2_ref[...] + b2_ref[...]


def _moe(h2, gates, x1, ada3, w_gate, w_up, w_down, ln_g, ln_b, seq_len, ada_row0, per_seq_ada):
    n_tok = h2.shape[0]
    t = MOE_WINDOW
    blocks_per_seq = seq_len // t
    assert not per_seq_ada or seq_len % t == 0
    group_w = EXPERTS_PER_GROUP * D_EXPERT

    def ada_map(i, g):
        return (ada_row0 + (i // blocks_per_seq if per_seq_ada else 0), 0, 0)

    return pl.pallas_call(
        _moe_kernel,
        grid=(n_tok // t, N_GROUPS),
        in_specs=[
            pl.BlockSpec((t, D_MODEL), lambda i, g: (i, 0)),
            pl.BlockSpec((t, LANES), lambda i, g: (i, 0)),
            pl.BlockSpec((t, D_MODEL), lambda i, g: (i, 0), pipeline_mode=pl.Buffered(1)),
            pl.BlockSpec((None, 1, 6 * D_MODEL), ada_map),
            pl.BlockSpec((EXPERTS_PER_GROUP, D_MODEL, D_EXPERT), lambda i, g: (g, 0, 0)),
            pl.BlockSpec((EXPERTS_PER_GROUP, D_MODEL, D_EXPERT), lambda i, g: (g, 0, 0)),
            pl.BlockSpec((EXPERTS_PER_GROUP, D_EXPERT, D_MODEL), lambda i, g: (g, 0, 0)),
            pl.BlockSpec((1, D_MODEL), lambda i, g: (0, 0)),
            pl.BlockSpec((1, D_MODEL), lambda i, g: (0, 0)),
        ],
        out_specs=pl.BlockSpec((t, D_MODEL), lambda i, g: (i, 0)),
        out_shape=jax.ShapeDtypeStruct((n_tok, D_MODEL), F32),
        scratch_shapes=[
            pltpu.VMEM((t, LANES), F32),
            pltpu.VMEM((SUBLANES, t), F32),
            pltpu.VMEM((t, LANES), BF16),
            pltpu.SMEM((N_GROUPS,), jnp.int32),
        ],
        compiler_params=pltpu.CompilerParams(dimension_semantics=("arbitrary", "arbitrary"),
                                             vmem_limit_bytes=VMEM_LIMIT),
        name="moe",
    )(h2, gates, x1, ada3, w_gate, w_up, w_down, ln_g, ln_b)


def _grid_pos_embed(n_tokens, d):
    rows = n_tokens // GRID_W
    rr, cc = np.meshgrid(np.arange(rows, dtype=np.float32), np.arange(GRID_W, dtype=np.float32), indexing="ij")
    quarter = d // 4
    freqs = np.exp(-math.log(POS_BASE) * np.arange(quarter, dtype=np.float32) / quarter).astype(np.float32)
    ang_r = rr.reshape(-1, 1) * freqs
    ang_c = cc.reshape(-1, 1) * freqs
    table = np.concatenate([np.sin(ang_r), np.cos(ang_r), np.sin(ang_c), np.cos(ang_c)], axis=-1)
    return jnp.asarray(table, dtype=F32)


def _lane_rep(v):
    return jnp.broadcast_to(v[..., None], v.shape + (LANES,)).astype(F32)


def _stream(x3d, pos, ada3, ada_row0, s_gdn0, h_ssd0t, want_state, wts):
    bsz, seq_len, _ = x3d.shape
    x2d = x3d.reshape(bsz * seq_len, D_MODEL)
    proj, small_t = _in_proj(x2d, pos, ada3, wts["w_big"], wts["w_small_t"], seq_len, ada_row0)
    gdn_out = _gdn(proj, small_t, wts["gdn_conv_w"], wts["gdn_conv_b"], wts["gdn_params"],
                   wts["gdn_norm_w"], s_gdn0, bsz, seq_len, want_state)
    ssd_out = _ssd(proj, small_t, wts["ssd_conv_w"], wts["ssd_conv_b"], wts["ssd_a_log"], wts["ssd_dt_bias"],
                   wts["ssd_d"], wts["ssd_norm_w"], h_ssd0t, bsz, seq_len, want_state)
    x1, h2, gates = _post(x2d, pos, gdn_out[0], ssd_out[0], proj, ada3, wts["w_gdn_out"], wts["w_ssd_out"],
                          wts["w_o"], wts["ln1_g"], wts["ln1_b"], wts["w_router"], wts["b_router"],
                          seq_len, ada_row0)
    y = _moe(h2, gates, x1, ada3, wts["w_exp_gate"], wts["w_exp_up"], wts["w_exp_down"], wts["ln2_g"],
             wts["ln2_b"], seq_len, ada_row0, pos is not None)
    states = (gdn_out[1], ssd_out[1]) if want_state else None
    return y.reshape(bsz, seq_len, D_MODEL), states


def kernel(x_prompt, x_sample, state_gdn, state_ssd, c, c_ctx, w_ada, b_ada, w_in, gdn_conv_w, gdn_conv_b, gdn_a_log, gdn_dt_bias, gdn_norm_w, w_gdn_out, ssd_conv_w, ssd_conv_b, ssd_a_log, ssd_dt_bias, ssd_d, ssd_norm_w, w_ssd_out, w_o, ln1_g, ln1_b, w_router_group, b_router_group, w_router_expert, b_router_expert, w_exp_gate, w_exp_up, w_exp_down, ln2_g, ln2_b):
    assert w_in.shape[0] == DEPTH == 1
    l = 0
    bsz_c = x_prompt.shape[0]
    bsz_l, seq_l, _ = x_sample.shape

    cvec = jnp.zeros((SUBLANES, D_MODEL), F32).at[0].set(c_ctx).at[1:1 + bsz_l].set(c)
    ada3 = _ada(cvec, w_ada[l], b_ada[l]).reshape(SUBLANES, 1, 6 * D_MODEL)

    wi = w_in[l]
    o_zg = 3072
    o_beta = 4096
    o_a = 4112
    o_xbc = 4128
    o_zs = 7200
    o_dt = 9248
    o_gate = 9312
    w_big = jnp.concatenate([wi[:, 0:o_zg], wi[:, o_zg:o_beta], wi[:, o_xbc:o_zs], wi[:, o_zs:o_dt],
                             wi[:, o_gate:]], axis=1).astype(BF16)
    w_beta = wi[:, o_beta:o_a].T.reshape(2, GDN_HEADS, D_MODEL)
    w_a = wi[:, o_a:o_xbc].T.reshape(2, GDN_HEADS, D_MODEL)
    w_head = jnp.concatenate([w_beta, w_a, jnp.zeros((SUBLANES - 4, GDN_HEADS, D_MODEL), F32)], axis=0)
    w_head = jnp.transpose(w_head, (1, 0, 2)).reshape(ROW_DT, D_MODEL)
    w_small_t = jnp.concatenate([w_head, wi[:, o_dt:o_gate].T], axis=0).astype(BF16)
    assert w_small_t.shape == (SMALL_ROWS, D_MODEL)
    gdn_params = jnp.concatenate([gdn_a_log[l], gdn_dt_bias[l], jnp.zeros((SUBLANES - 4, GDN_HEADS), F32)], axis=0)
    gdn_params = _lane_rep(gdn_params.T)

    w_router = jnp.zeros((D_MODEL, LANES), F32)
    w_router = w_router.at[:, 0:N_EXPERTS].set(w_router_expert[l]).at[:, N_EXPERTS:N_EXPERTS + N_GROUPS].set(
        w_router_group[l])
    b_router = jnp.zeros((1, LANES), F32)
    b_router = b_router.at[0, 0:N_EXPERTS].set(b_router_expert[l]).at[0, N_EXPERTS:N_EXPERTS + N_GROUPS].set(
        b_router_group[l])

    wts = {
        "w_big": w_big, "w_small_t": w_small_t,
        "gdn_conv_w": gdn_conv_w[l], "gdn_conv_b": gdn_conv_b[l].reshape(1, -1),
        "gdn_params": gdn_params,
        "gdn_norm_w": gdn_norm_w[l].reshape(1, -1),
        "ssd_conv_w": ssd_conv_w[l], "ssd_conv_b": ssd_conv_b[l].reshape(1, -1),
        "ssd_a_log": _lane_rep(ssd_a_log[l]), "ssd_dt_bias": _lane_rep(ssd_dt_bias[l]),
        "ssd_d": jnp.repeat(ssd_d[l], SSD_P).reshape(1, -1), "ssd_norm_w": ssd_norm_w[l].reshape(1, -1),
        "w_gdn_out": w_gdn_out[l].astype(BF16), "w_ssd_out": w_ssd_out[l].astype(BF16),
        "w_o": w_o[l].astype(BF16),
        "ln1_g": ln1_g[l].reshape(1, -1), "ln1_b": ln1_b[l].reshape(1, -1),
        "w_router": w_router.astype(BF16), "b_router": b_router,
        "w_exp_gate": w_exp_gate[l].astype(BF16), "w_exp_up": w_exp_up[l].astype(BF16),
        "w_exp_down": w_exp_down[l].astype(BF16),
        "ln2_g": ln2_g[l].reshape(1, -1), "ln2_b": ln2_b[l].reshape(1, -1),
    }

    pos = _grid_pos_embed(seq_l, D_MODEL)
    h0 = state_ssd[:, l].reshape(bsz_l, 2, SSD_GROUPS, HEADS_PER_GROUP, SSD_P, SSD_N)
    h0t = jnp.transpose(h0, (0, 1, 2, 5, 3, 4)).reshape(bsz_l, 2, SSD_GROUPS, SSD_N, GROUP_W)

    y_ctx, (s_gdn, h_ssd_t) = _stream(x_prompt, None, ada3, 0, None, None, True, wts)
    y_lat, _ = _stream(x_sample, pos, ada3, 1, state_gdn, h0t, False, wts)

    h_ssd = h_ssd_t.reshape(bsz_c, 2, SSD_GROUPS, SSD_N, HEADS_PER_GROUP, SSD_P)
    new_ssd = jnp.transpose(h_ssd, (0, 1, 2, 4, 5, 3)).reshape(bsz_c, 1, 2, SSD_HEADS, SSD_P, SSD_N)
    return (y_ctx, y_lat, s_gdn, new_ssd)
```

```python
import functools
import math

import jax
import jax.numpy as jnp
import numpy as np
from jax import lax
from jax.experimental import pallas as pl
from jax.experimental.pallas import tpu as pltpu

F32 = jnp.float32
BF16 = jnp.bfloat16
HIGHEST = lax.Precision.HIGHEST

D_MODEL = 1024
GRID_W = 64
POS_BASE = 10000.0
CONV_K = 5
GDN_HEADS = 8
GDN_DK = 128
GDN_DV = 128
SSD_HEADS = 32
SSD_P = 64
SSD_INNER = SSD_HEADS * SSD_P
SSD_GROUPS = 4
SSD_N = 128
HEADS_PER_GROUP = SSD_HEADS // SSD_GROUPS
GROUP_W = HEADS_PER_GROUP * SSD_P
N_GROUPS = 4
EXPERTS_PER_GROUP = 8
N_EXPERTS = N_GROUPS * EXPERTS_PER_GROUP
D_EXPERT = 256
EPS = 1e-6
DEPTH = 1
ALPHA = (2.0 * DEPTH) ** 0.25

LANES = 128
SUBLANES = 8
CHUNK = 128
CONV_PAD = SUBLANES
VMEM_LIMIT = 56 * 1024 * 1024

COL_QKV = 0
COL_ZG = 3072
COL_XBC = 4096
COL_ZS = 7168
COL_GATE = 9216
PROJ_COLS = 11264
PROJ_TN = 1024
GDN_ROW_BETA = 0
GDN_ROW_A = 2
ROW_DT = GDN_HEADS * SUBLANES
SMALL_ROWS = 128

TOK_BLOCK = 2048
POST_BLOCK = 512
MOE_WINDOW = 1024
MOE_TILE = 288
MOE_TILES = -(-MOE_WINDOW // MOE_TILE)
GROUP_LANE = N_EXPERTS
SSD_CHUNKS_PER_STEP = 8


def _bdot(a, b):
    return jnp.dot(a.astype(BF16), b.astype(BF16), preferred_element_type=F32)


def _sigmoid(x):
    return 0.5 * jnp.tanh(0.5 * x) + 0.5


def _silu(x):
    return x * _sigmoid(x)


def _softplus(x):
    return jnp.maximum(x, 0.0) + jnp.log1p(jnp.exp(-jnp.abs(x)))


def _ln(x):
    mu = jnp.mean(x, axis=-1, keepdims=True)
    xc = x - mu
    var = jnp.mean(xc * xc, axis=-1, keepdims=True)
    return xc * lax.rsqrt(var + EPS)


def _lane_scan(x, reverse):
    lane = lax.broadcasted_iota(jnp.int32, x.shape, 1)
    s = 1
    while s < CHUNK:
        if reverse:
            shifted = pltpu.roll(x, CHUNK - s, axis=1)
            x = x + jnp.where(lane < CHUNK - s, shifted, 0.0)
        else:
            shifted = pltpu.roll(x, s, axis=1)
            x = x + jnp.where(lane >= s, shifted, 0.0)
        s *= 2
    return x


def _tri_masks(reverse):
    row = lax.broadcasted_iota(jnp.int32, (CHUNK, CHUNK), 0)
    col = lax.broadcasted_iota(jnp.int32, (CHUNK, CHUNK), 1)
    if reverse:
        return row <= col, row < col
    return row >= col, row > col


def _decay_matrix(acc_row, incl):
    acc_rb = jnp.broadcast_to(acc_row, (CHUNK, CHUNK))
    acc_col = acc_rb.T
    decay = jnp.exp(jnp.where(incl, acc_col - acc_rb, -jnp.inf))
    return decay, acc_col


def _conv_tap_sum(pad_ref, t0, lo, hi, w_ref, b_ref):
    acc = jnp.broadcast_to(b_ref[...], (CHUNK, hi - lo))
    for j in range(CONV_K):
        start = t0 + CONV_PAD - CONV_K // 2 + j
        acc = acc + pad_ref[start:start + CHUNK, lo:hi] * w_ref[j:j + 1, :]
    return acc


def _ada_kernel(c_ref, w_ref, b_ref, o_ref):
    s = _silu(c_ref[...])
    o_ref[...] = jnp.dot(s, w_ref[...], precision=HIGHEST, preferred_element_type=F32) + b_ref[...]


def _ada(cvec, w_ada, b_ada):
    n_out = w_ada.shape[1]
    tn = 1024
    return pl.pallas_call(
        _ada_kernel,
        grid=(n_out // tn,),
        in_specs=[
            pl.BlockSpec((SUBLANES, D_MODEL), lambda j: (0, 0)),
            pl.BlockSpec((D_MODEL, tn), lambda j: (0, j)),
            pl.BlockSpec((1, tn), lambda j: (0, j)),
        ],
        out_specs=pl.BlockSpec((SUBLANES, tn), lambda j: (0, j)),
        out_shape=jax.ShapeDtypeStruct((SUBLANES, n_out), F32),
        compiler_params=pltpu.CompilerParams(dimension_semantics=("arbitrary",), vmem_limit_bytes=VMEM_LIMIT),
        name="ada",
    )(cvec, w_ada, b_ada.reshape(1, n_out))


def _inproj_kernel(*refs, has_pos):
    if has_pos:
        x_ref, pos_ref, ada_ref, w_ref, wst_ref, proj_ref, small_ref, h_ref = refs
    else:
        x_ref, ada_ref, w_ref, wst_ref, proj_ref, small_ref, h_ref = refs
        pos_ref = None

    @pl.when(pl.program_id(1) == 0)
    def _():
        x = x_ref[...]
        if pos_ref is not None:
            x = x + pos_ref[...]
        shift = ada_ref[:, 0:D_MODEL]
        scale = ada_ref[:, D_MODEL:2 * D_MODEL]
        h = (_ln(x) * (1.0 + scale) + shift).astype(BF16)
        h_ref[...] = h
        small_ref[...] = lax.dot_general(wst_ref[...], h, (((1,), (1,)), ((), ())), preferred_element_type=F32)

    proj_ref[...] = jnp.dot(h_ref[...], w_ref[...], preferred_element_type=F32).astype(BF16)


def _in_proj(x2d, pos, ada3, w_big, w_small_t, seq_len, ada_row0):
    n_tok = x2d.shape[0]
    t = TOK_BLOCK
    blocks_per_seq = seq_len // t
    has_pos = pos is not None

    def ada_map(i, j):
        return (ada_row0 + (i // blocks_per_seq if has_pos else 0), 0, 0)

    in_specs = [pl.BlockSpec((t, D_MODEL), lambda i, j: (i, 0))]
    args = [x2d]
    if has_pos:
        pos_mode = dict(pipeline_mode=pl.Buffered(1)) if blocks_per_seq == 1 else {}
        in_specs.append(pl.BlockSpec((t, D_MODEL), lambda i, j: (i % blocks_per_seq, 0), **pos_mode))
        args.append(pos)
    in_specs += [
        pl.BlockSpec((None, 1, 6 * D_MODEL), ada_map),
        pl.BlockSpec((D_MODEL, PROJ_TN), lambda i, j: (0, j)),
        pl.BlockSpec((SMALL_ROWS, D_MODEL), lambda i, j: (0, 0)),
    ]
    args += [ada3, w_big, w_small_t]
    return pl.pallas_call(
        functools.partial(_inproj_kernel, has_pos=has_pos),
        grid=(n_tok // t, PROJ_COLS // PROJ_TN),
        in_specs=in_specs,
        out_specs=[
            pl.BlockSpec((t, PROJ_TN), lambda i, j: (i, j)),
            pl.BlockSpec((SMALL_ROWS, t), lambda i, j: (0, i)),
        ],
        out_shape=[
            jax.ShapeDtypeStruct((n_tok, PROJ_COLS), BF16),
            jax.ShapeDtypeStruct((SMALL_ROWS, n_tok), F32),
        ],
        scratch_shapes=[pltpu.VMEM((t, D_MODEL), BF16)],
        compiler_params=pltpu.CompilerParams(dimension_semantics=("arbitrary", "arbitrary"),
                                             vmem_limit_bytes=VMEM_LIMIT),
        name="in_proj",
    )(*args)


INV_BASE = 8
GDN_GROUP = 8


def _take_blocks(m, size, odd):
    parts = [m[k * size:(k + 1) * size] for k in range(CHUNK // size) if (k % 2 == 1) == odd]
    return parts[0] if len(parts) == 1 else jnp.concatenate(parts, axis=0)


def _interleave_blocks(even_rows, odd_rows, size):
    parts = []
    for k in range(CHUNK // (2 * size)):
        parts.append(even_rows[k * size:(k + 1) * size])
        parts.append(odd_rows[k * size:(k + 1) * size])
    return jnp.concatenate(parts, axis=0)


def _unit_tri_inverses(nmats, uppers):
    row = lax.broadcasted_iota(jnp.int32, (CHUNK, CHUNK), 0)
    col = lax.broadcasted_iota(jnp.int32, (CHUNK, CHUNK), 1)

    def same_block(size):
        shift = int(math.log2(size))
        return (row >> shift) == (col >> shift)

    eye = (row == col).astype(F32)
    base = same_block(INV_BASE)
    nds = [jnp.where(base, n, 0.0) for n in nmats]
    xs = [eye - nd for nd in nds]
    pws = [_bdot(nd, nd) for nd in nds]
    size = 2
    while True:
        xs = [x + _bdot(x, pw) for x, pw in zip(xs, pws)]
        size *= 2
        if size >= INV_BASE:
            break
        pws = [_bdot(pw, pw) for pw in pws]
    size = INV_BASE
    while size < CHUNK:
        coupling = same_block(2 * size) & jnp.logical_not(same_block(size))
        offs = [_take_blocks(jnp.where(coupling, n, 0.0), size, odd=not up) for n, up in zip(nmats, uppers)]
        tmps = [_bdot(off, x) for off, x in zip(offs, xs)]
        zeros = jnp.zeros((CHUNK // 2, CHUNK), F32)
        fulls = [_interleave_blocks(t, zeros, size) if up else _interleave_blocks(zeros, t, size)
                 for t, up in zip(tmps, uppers)]
        moved = [_take_blocks(x, size, odd=not up) for x, up in zip(xs, uppers)]
        kept = [_take_blocks(x, size, odd=up) for x, up in zip(xs, uppers)]
        news = [m - _bdot(m, f) for m, f in zip(moved, fulls)]
        xs = [_interleave_blocks(nw, kp, size) if up else _interleave_blocks(kp, nw, size)
              for nw, kp, up in zip(news, kept, uppers)]
        size *= 2
    return xs


def _gdn_prepare(chunks, qn_ref, kn_ref, vn_ref, kt_ref, sc_ref, lhs_ref, sb_ref, ob_ref, eg_ref):
    grams = [_bdot(jnp.concatenate([kn_ref[g], qn_ref[g]], axis=0), kt_ref[g]) for g in chunks]
    units = []
    for g, gram in zip(chunks, grams):
        for d in range(2):
            reverse = d == 1
            incl, strict = _tri_masks(reverse)
            beta = sc_ref[g, d:d + 1, :]
            gc = sc_ref[g, 2 + d:3 + d, :]
            decay, gcol = _decay_matrix(gc, incl)
            beta_rb = jnp.broadcast_to(beta, (CHUNK, CHUNK))
            glast = gc[:, 0:1] if reverse else gc[:, CHUNK - 1:CHUNK]
            sc_ref[g, 4 + d:5 + d, :] = jnp.broadcast_to(jnp.exp(glast), (1, CHUNK))
            eg_ref[g, d] = jnp.exp(gcol)
            units.append(dict(
                g=g, d=d,
                nmat=jnp.where(strict, gram[0:CHUNK] * decay, 0.0) * beta_rb,
                att=gram[CHUNK:2 * CHUNK] * decay * beta_rb,
                kd_scale=beta * jnp.exp(glast - gc)))
    xinvs = _unit_tri_inverses([u["nmat"] for u in units], [u["d"] == 1 for u in units])
    uws = [_bdot(x, jnp.concatenate([vn_ref[u["g"]], kn_ref[u["g"]] * eg_ref[u["g"], u["d"]]], axis=1))
           for u, x in zip(units, xinvs)]
    mixeds = []
    for u, uw in zip(units, uws):
        kd = kt_ref[u["g"]] * jnp.broadcast_to(u["kd_scale"], (CHUNK, CHUNK))
        mixeds.append(_bdot(jnp.concatenate([kd, u["att"]], axis=0), uw))
    for u, mixed in zip(units, mixeds):
        g, d = u["g"], u["d"]
        lhs_ref[g, d, 0:CHUNK, :] = mixed[0:CHUNK, GDN_DV:].astype(BF16)
        lhs_ref[g, d, CHUNK:2 * CHUNK, :] = qn_ref[g].astype(BF16)
        lhs_ref[g, d, 2 * CHUNK:3 * CHUNK, :] = mixed[CHUNK:, GDN_DV:].astype(BF16)
        sb_ref[g, d] = mixed[0:CHUNK, 0:GDN_DV]
        ob_ref[g, d] = mixed[CHUNK:, 0:GDN_DV]


def _gdn_steps(steps, lhs_ref, sb_ref, ob_ref, eg_ref, sc_ref, s_ref, o_refs):
    states = [s_ref[s, d] for _, s, d in steps]
    rs = [jnp.dot(lhs_ref[g, d], st.astype(BF16), preferred_element_type=F32)
          for (g, _, d), st in zip(steps, states)]
    for (g, s, d), st, r in zip(steps, states, rs):
        s_ref[s, d] = st * sc_ref[g, 4 + d:5 + d, :] - r[0:CHUNK] + sb_ref[g, d]
        o_refs[d][g] = eg_ref[g, d] * r[CHUNK:2 * CHUNK] - r[2 * CHUNK:3 * CHUNK] + ob_ref[g, d]


def _gdn_kernel(*refs, n_seq, n_chunks, has_s0, want_state):
    it = iter(refs)
    q_ref, k_ref, v_ref, z_ref, sm_ref = (next(it) for _ in range(5))
    wq_ref, wk_ref, wv_ref, bq_ref, bk_ref, bv_ref = (next(it) for _ in range(6))
    par_ref, nw_ref = (next(it) for _ in range(2))
    s0_ref = next(it) if has_s0 else None
    og_ref = next(it)
    sout_ref = next(it) if want_state else None
    pad_ref, qn_ref, kn_ref, vn_ref, kt_ref, of_ref, ob_ref, sc_ref, s_ref = (next(it) for _ in range(9))
    lhs_ref, sb_ref, ou_ref, eg_ref, st_ref = (next(it) for _ in range(5))

    seq = n_chunks * CHUNK
    total = n_seq * n_chunks
    zero_rows = jnp.zeros((CONV_PAD, 3 * LANES), F32)
    for s in range(n_seq):
        pad_ref[s, 0:CONV_PAD, :] = zero_rows
        pad_ref[s, CONV_PAD + seq:2 * CONV_PAD + seq, :] = zero_rows
        for c in range(n_chunks):
            t0 = c * CHUNK
            r0 = s * seq + t0
            pad_ref[s, CONV_PAD + t0:CONV_PAD + t0 + CHUNK, 0:LANES] = q_ref[r0:r0 + CHUNK, :].astype(F32)
            pad_ref[s, CONV_PAD + t0:CONV_PAD + t0 + CHUNK, LANES:2 * LANES] = k_ref[r0:r0 + CHUNK, :].astype(F32)
            pad_ref[s, CONV_PAD + t0:CONV_PAD + t0 + CHUNK, 2 * LANES:3 * LANES] = v_ref[r0:r0 + CHUNK, :].astype(F32)

    a_neg = [-jnp.exp(par_ref[d:d + 1, :]) for d in range(2)]
    dt_bias = [par_ref[2 + d:3 + d, :] for d in range(2)]
    for s in range(n_seq):
        for c in range(n_chunks):
            t0 = c * CHUNK
            r0 = s * seq + t0
            g = s * n_chunks + c
            pad_s = pad_ref.at[s]
            qa = _silu(_conv_tap_sum(pad_s, t0, 0, LANES, wq_ref, bq_ref))
            ka = _silu(_conv_tap_sum(pad_s, t0, LANES, 2 * LANES, wk_ref, bk_ref))
            va = _silu(_conv_tap_sum(pad_s, t0, 2 * LANES, 3 * LANES, wv_ref, bv_ref))
            qn = qa * lax.rsqrt(jnp.sum(qa * qa, axis=-1, keepdims=True) + EPS) * (GDN_DK ** -0.5)
            kn = ka * lax.rsqrt(jnp.sum(ka * ka, axis=-1, keepdims=True) + EPS)
            qn_ref[g] = qn
            kn_ref[g] = kn
            vn_ref[g] = va
            kt_ref[g] = kn.T

    for g in range(total):
        r0 = g * CHUNK
        for d in range(2):
            st_ref[d, g:g + 1, :] = sm_ref[GDN_ROW_A + d:GDN_ROW_A + d + 1, r0:r0 + CHUNK]
            st_ref[2 + d, g:g + 1, :] = sm_ref[GDN_ROW_BETA + d:GDN_ROW_BETA + d + 1, r0:r0 + CHUNK]
    for d in range(2):
        gcum = _lane_scan(a_neg[d] * _softplus(st_ref[d] + dt_bias[d]), reverse=d == 1)
        beta = _sigmoid(st_ref[2 + d])
        for g in range(total):
            sc_ref[g, d:d + 1, :] = beta[g:g + 1, :]
            sc_ref[g, 2 + d:3 + d, :] = gcum[g:g + 1, :]

    def prepare(i, carry):
        _gdn_prepare([i * GDN_GROUP + j for j in range(GDN_GROUP)], qn_ref, kn_ref, vn_ref, kt_ref, sc_ref,
                     lhs_ref, sb_ref, ou_ref, eg_ref)
        return carry

    lax.fori_loop(0, total // GDN_GROUP, prepare, 0)

    if has_s0:
        s_ref[...] = s0_ref[...]
    else:
        s_ref[...] = jnp.zeros(s_ref.shape, F32)

    def advance(i, carry):
        steps = []
        for s in range(n_seq):
            steps.append((s * n_chunks + i, s, 0))
            steps.append((s * n_chunks + n_chunks - 1 - i, s, 1))
        _gdn_steps(steps, lhs_ref, sb_ref, ou_ref, eg_ref, sc_ref, s_ref, (of_ref, ob_ref))
        return carry

    lax.fori_loop(0, n_chunks, advance, 0)

    if want_state:
        sout_ref[...] = s_ref[...]
    for g in range(total):
        r0 = g * CHUNK
        o = of_ref[g] + ob_ref[g]
        o = o * lax.rsqrt(jnp.mean(o * o, axis=-1, keepdims=True) + EPS)
        og_ref[r0:r0 + CHUNK, :] = (o * nw_ref[...] * _silu(z_ref[r0:r0 + CHUNK, :].astype(F32))).astype(BF16)


def _gdn(proj, small_t, conv_w, conv_b, head_params, norm_w, s0, bsz, seq_len, want_state):
    n_chunks = seq_len // CHUNK
    n_tok = bsz * seq_len
    has_s0 = s0 is not None
    n_seq = max(1, GDN_GROUP * 2 // n_chunks)
    assert bsz % n_seq == 0 and (n_seq * n_chunks) % GDN_GROUP == 0
    total = n_seq * n_chunks
    rows = n_seq * seq_len
    col = lambda off: (lambda b, h: (b, off + h))
    cw = lambda off: (lambda b, h: (0, off + h))
    in_specs = [
        pl.BlockSpec((rows, LANES), col(COL_QKV // LANES)),
        pl.BlockSpec((rows, LANES), col(COL_QKV // LANES + GDN_HEADS)),
        pl.BlockSpec((rows, LANES), col(COL_QKV // LANES + 2 * GDN_HEADS)),
        pl.BlockSpec((rows, LANES), col(COL_ZG // LANES)),
        pl.BlockSpec((SUBLANES, rows), lambda b, h: (h, b)),
        pl.BlockSpec((CONV_K, LANES), cw(0)),
        pl.BlockSpec((CONV_K, LANES), cw(GDN_HEADS)),
        pl.BlockSpec((CONV_K, LANES), cw(2 * GDN_HEADS)),
        pl.BlockSpec((1, LANES), cw(0)),
        pl.BlockSpec((1, LANES), cw(GDN_HEADS)),
        pl.BlockSpec((1, LANES), cw(2 * GDN_HEADS)),
        pl.BlockSpec((None, SUBLANES, LANES), lambda b, h: (h, 0, 0)),
        pl.BlockSpec((1, LANES), lambda b, h: (0, 0)),
    ]
    args = [proj, proj, proj, proj, small_t, conv_w, conv_w, conv_w, conv_b, conv_b, conv_b,
            head_params, norm_w]
    state_spec = pl.BlockSpec((n_seq, None, 2, None, GDN_DK, GDN_DV), lambda b, h: (b, 0, 0, h, 0, 0))
    if has_s0:
        in_specs.append(state_spec)
        args.append(s0)
    out_specs = [pl.BlockSpec((rows, LANES), lambda b, h: (b, h))]
    out_shape = [jax.ShapeDtypeStruct((n_tok, GDN_HEADS * GDN_DV), BF16)]
    if want_state:
        out_specs.append(state_spec)
        out_shape.append(jax.ShapeDtypeStruct((bsz, 1, 2, GDN_HEADS, GDN_DK, GDN_DV), F32))
    chunked = pltpu.VMEM((total, CHUNK, LANES), F32)
    per_dir = pltpu.VMEM((total, 2, CHUNK, LANES), F32)
    return pl.pallas_call(
        functools.partial(_gdn_kernel, n_seq=n_seq, n_chunks=n_chunks, has_s0=has_s0, want_state=want_state),
        grid=(bsz // n_seq, GDN_HEADS),
        in_specs=in_specs,
        out_specs=out_specs,
        out_shape=out_shape,
        scratch_shapes=[
            pltpu.VMEM((n_seq, seq_len + 2 * CONV_PAD, 3 * LANES), F32),
            chunked, chunked, chunked, chunked, chunked, chunked,
            pltpu.VMEM((total, SUBLANES, CHUNK), F32),
            pltpu.VMEM((n_seq, 2, GDN_DK, GDN_DV), F32),
            pltpu.VMEM((total, 2, 3 * CHUNK, GDN_DK), BF16),
            per_dir, per_dir, per_dir,
            pltpu.VMEM((4, total, CHUNK), F32),
        ],
        compiler_params=pltpu.CompilerParams(dimension_semantics=("arbitrary", "arbitrary"),
                                             vmem_limit_bytes=VMEM_LIMIT),
        name="gdn",
    )(*args)


def _ssd_steps(steps, xs_ref, cs_ref, bt_ref, dt_ref, ac_ref, ht_ref, y_ref):
    lane = lax.broadcasted_iota(jnp.int32, (CHUNK, LANES), 1)
    low_half = lane < SSD_P
    shared = [(_bdot(cs_ref[c], bt_ref[c]), _bdot(cs_ref[c], ht_ref[s, d])) for c, s, d in steps]
    pairs_per_stage = 2
    for p0 in range(0, HEADS_PER_GROUP // 2, pairs_per_stage):
        units = []
        for (c, s, d), (cb, y_off) in zip(steps, shared):
            reverse = d == 1
            incl, _ = _tri_masks(reverse)
            bt = bt_ref[c]
            for p in range(p0, p0 + pairs_per_stage):
                halves = []
                for r in (2 * p, 2 * p + 1):
                    row = d * HEADS_PER_GROUP + r
                    dt = dt_ref[c, row:row + 1, :]
                    ac = ac_ref[c, row:row + 1, :]
                    decay, acol = _decay_matrix(ac, incl)
                    last = ac[:, 0:1] if reverse else ac[:, CHUNK - 1:CHUNK]
                    halves.append(dict(
                        m=(cb * decay * jnp.broadcast_to(dt, (CHUNK, CHUNK))).astype(BF16),
                        bts=(bt * jnp.broadcast_to(dt * jnp.exp(last - ac), (SSD_N, CHUNK))).astype(BF16),
                        acol=acol, elast=jnp.exp(last)))
                units.append(dict(c=c, s=s, d=d, p=p, halves=halves, y_off=y_off[:, p * LANES:(p + 1) * LANES]))
        for u in units:
            x = xs_ref[u["c"], :, u["p"] * LANES:(u["p"] + 1) * LANES]
            zero = jnp.zeros_like(x)
            xh = (jnp.where(low_half, x, zero), jnp.where(low_half, zero, x))
            u["yd"] = [jnp.dot(hf["m"], xm, preferred_element_type=F32) for hf, xm in zip(u["halves"], xh)]
            u["st"] = [jnp.dot(hf["bts"], xm, preferred_element_type=F32) for hf, xm in zip(u["halves"], xh)]
        for u in units:
            c, s, d, p = u["c"], u["s"], u["d"], u["p"]
            h0, h1 = u["halves"]
            eoff = jnp.exp(jnp.where(low_half, h0["acol"], h1["acol"]))
            elast = jnp.where(low_half, h0["elast"], h1["elast"])
            cols = slice(p * LANES, (p + 1) * LANES)
            y_ref[c, :, cols] = y_ref[c, :, cols] + (u["yd"][0] + u["yd"][1]) + eoff * u["y_off"]
            ht_ref[s, d, :, cols] = ht_ref[s, d, :, cols] * elast + (u["st"][0] + u["st"][1])


def _ssd_kernel(*refs, n_seq, n_chunks, has_h0, want_state):
    it = iter(refs)
    x_ref, b_ref, c_ref, z_ref, dtf_ref, dtb_ref = (next(it) for _ in range(6))
    wx_ref, wb_ref, wc_ref, bx_ref, bb_ref, bc_ref = (next(it) for _ in range(6))
    alog_ref, dtbias_ref, dvec_ref, nw_ref = (next(it) for _ in range(4))
    h0_ref = next(it) if has_h0 else None
    yg_ref = next(it)
    hout_ref = next(it) if want_state else None
    pad_ref, xs_ref, cs_ref, bt_ref, dt_ref, ac_ref, y_ref, ht_ref = (next(it) for _ in range(8))

    seq = n_chunks * CHUNK
    total = n_seq * n_chunks
    width = GROUP_W + 2 * SSD_N
    zero_rows = jnp.zeros((CONV_PAD, width), F32)
    for s in range(n_seq):
        pad_ref[s, 0:CONV_PAD, :] = zero_rows
        pad_ref[s, CONV_PAD + seq:2 * CONV_PAD + seq, :] = zero_rows
        for c in range(n_chunks):
            t0 = c * CHUNK
            r0 = s * seq + t0
            pad_ref[s, CONV_PAD + t0:CONV_PAD + t0 + CHUNK, 0:GROUP_W] = x_ref[r0:r0 + CHUNK, :].astype(F32)
            pad_ref[s, CONV_PAD + t0:CONV_PAD + t0 + CHUNK, GROUP_W:GROUP_W + SSD_N] = b_ref[r0:r0 + CHUNK, :].astype(F32)
            pad_ref[s, CONV_PAD + t0:CONV_PAD + t0 + CHUNK, GROUP_W + SSD_N:width] = c_ref[r0:r0 + CHUNK, :].astype(F32)

    raw_dt = (dtf_ref, dtb_ref)
    for s in range(n_seq):
        pad_s = pad_ref.at[s]
        for c in range(n_chunks):
            t0 = c * CHUNK
            g = s * n_chunks + c
            for lo in range(0, GROUP_W, LANES):
                cols = slice(lo, lo + LANES)
                xa = _silu(_conv_tap_sum(pad_s, t0, lo, lo + LANES, wx_ref.at[:, cols], bx_ref.at[:, cols]))
                xs_ref[g, :, cols] = xa.astype(BF16)
                y_ref[g, :, cols] = xa * dvec_ref[:, cols]
            ba = _silu(_conv_tap_sum(pad_s, t0, GROUP_W, GROUP_W + SSD_N, wb_ref, bb_ref))
            ca = _silu(_conv_tap_sum(pad_s, t0, GROUP_W + SSD_N, width, wc_ref, bc_ref))
            cs_ref[g] = ca
            bt_ref[g] = ba.T
            for d in range(2):
                dt_ref[g, d * HEADS_PER_GROUP:(d + 1) * HEADS_PER_GROUP, :] = raw_dt[d][:, g * CHUNK:(g + 1) * CHUNK]

    for d in range(2):
        lo, hi = d * HEADS_PER_GROUP, (d + 1) * HEADS_PER_GROUP
        dt = _softplus(dt_ref[:, lo:hi, :] + dtbias_ref[d][None])
        dt_ref[:, lo:hi, :] = dt
        scaled = (dt * (-jnp.exp(alog_ref[d]))[None]).reshape(total * HEADS_PER_GROUP, CHUNK)
        ac_ref[:, lo:hi, :] = _lane_scan(scaled, reverse=d == 1).reshape(total, HEADS_PER_GROUP, CHUNK)

    if has_h0:
        for s in range(n_seq):
            for d in range(2):
                for lo in range(0, GROUP_W, LANES):
                    ht_ref[s, d, :, lo:lo + LANES] = h0_ref[s, d, lo:lo + LANES, :].T
    else:
        ht_ref[...] = jnp.zeros(ht_ref.shape, F32)

    def body(i, carry):
        for s in range(n_seq):
            _ssd_steps([(s * n_chunks + i, s, 0), (s * n_chunks + n_chunks - 1 - i, s, 1)],
                       xs_ref, cs_ref, bt_ref, dt_ref, ac_ref, ht_ref, y_ref)
        return carry

    lax.fori_loop(0, n_chunks, body, 0)

    if want_state:
        for s in range(n_seq):
            for d in range(2):
                for lo in range(0, GROUP_W, LANES):
                    hout_ref[s, d, lo:lo + LANES, :] = ht_ref[s, d, :, lo:lo + LANES].T
    for g in range(total):
        r0 = g * CHUNK
        ssq = jnp.zeros((CHUNK, 1), F32)
        for lo in range(0, GROUP_W, LANES):
            cols = slice(lo, lo + LANES)
            y = y_ref[g, :, cols] * _silu(z_ref[r0:r0 + CHUNK, cols].astype(F32))
            y_ref[g, :, cols] = y
            ssq = ssq + jnp.sum(y * y, axis=-1, keepdims=True)
        inv = lax.rsqrt(ssq * (1.0 / GROUP_W) + EPS)
        for lo in range(0, GROUP_W, LANES):
            cols = slice(lo, lo + LANES)
            yg_ref[r0:r0 + CHUNK, cols] = (y_ref[g, :, cols] * inv * nw_ref[:, cols]).astype(BF16)


def _ssd(proj, small_t, conv_w, conv_b, a_log_rep, dt_bias_rep, d_vec, norm_w, h0t, bsz, seq_len, want_state):
    n_chunks = seq_len // CHUNK
    n_tok = bsz * seq_len
    has_h0 = h0t is not None
    n_seq = max(1, SSD_CHUNKS_PER_STEP // n_chunks)
    assert bsz % n_seq == 0
    total = n_seq * n_chunks
    rows_per_step = n_seq * seq_len
    xbc_w = COL_XBC // GROUP_W
    bc_l = (COL_XBC + SSD_INNER) // LANES
    in_specs = [
        pl.BlockSpec((rows_per_step, GROUP_W), lambda b, g: (b, xbc_w + g)),
        pl.BlockSpec((rows_per_step, SSD_N), lambda b, g: (b, bc_l + g)),
        pl.BlockSpec((rows_per_step, SSD_N), lambda b, g: (b, bc_l + SSD_GROUPS + g)),
        pl.BlockSpec((rows_per_step, GROUP_W), lambda b, g: (b, COL_ZS // GROUP_W + g)),
        pl.BlockSpec((HEADS_PER_GROUP, rows_per_step), lambda b, g: (ROW_DT // HEADS_PER_GROUP + g, b)),
        pl.BlockSpec((HEADS_PER_GROUP, rows_per_step),
                     lambda b, g: ((ROW_DT + SSD_HEADS) // HEADS_PER_GROUP + g, b)),
        pl.BlockSpec((CONV_K, GROUP_W), lambda b, g: (0, g)),
        pl.BlockSpec((CONV_K, SSD_N), lambda b, g: (0, SSD_INNER // SSD_N + g)),
        pl.BlockSpec((CONV_K, SSD_N), lambda b, g: (0, SSD_INNER // SSD_N + SSD_GROUPS + g)),
        pl.BlockSpec((1, GROUP_W), lambda b, g: (0, g)),
        pl.BlockSpec((1, SSD_N), lambda b, g: (0, SSD_INNER // SSD_N + g)),
        pl.BlockSpec((1, SSD_N), lambda b, g: (0, SSD_INNER // SSD_N + SSD_GROUPS + g)),
        pl.BlockSpec((2, HEADS_PER_GROUP, LANES), lambda b, g: (0, g, 0)),
        pl.BlockSpec((2, HEADS_PER_GROUP, LANES), lambda b, g: (0, g, 0)),
        pl.BlockSpec((1, GROUP_W), lambda b, g: (0, g)),
        pl.BlockSpec((1, GROUP_W), lambda b, g: (0, g)),
    ]
    args = [proj, proj, proj, proj, small_t, small_t, conv_w, conv_w, conv_w, conv_b, conv_b, conv_b,
            a_log_rep, dt_bias_rep, d_vec, norm_w]
    state_spec = pl.BlockSpec((n_seq, 2, None, GROUP_W, SSD_N), lambda b, g: (b, 0, g, 0, 0))
    if has_h0:
        in_specs.append(state_spec)
        args.append(h0t)
    out_specs = [pl.BlockSpec((rows_per_step, GROUP_W), lambda b, g: (b, g))]
    out_shape = [jax.ShapeDtypeStruct((n_tok, SSD_INNER), BF16)]
    if want_state:
        out_specs.append(state_spec)
        out_shape.append(jax.ShapeDtypeStruct((bsz, 2, SSD_GROUPS, GROUP_W, SSD_N), F32))
    rows = pltpu.VMEM((total, 2 * HEADS_PER_GROUP, CHUNK), F32)
    return pl.pallas_call(
        functools.partial(_ssd_kernel, n_seq=n_seq, n_chunks=n_chunks, has_h0=has_h0, want_state=want_state),
        grid=(bsz // n_seq, SSD_GROUPS),
        in_specs=in_specs,
        out_specs=out_specs,
        out_shape=out_shape,
        scratch_shapes=[
            pltpu.VMEM((n_seq, seq_len + 2 * CONV_PAD, GROUP_W + 2 * SSD_N), F32),
            pltpu.VMEM((total, CHUNK, GROUP_W), BF16),
            pltpu.VMEM((total, CHUNK, SSD_N), F32),
            pltpu.VMEM((total, SSD_N, CHUNK), F32),
            rows, rows,
            pltpu.VMEM((total, CHUNK, GROUP_W), F32),
            pltpu.VMEM((n_seq, 2, SSD_N, GROUP_W), F32),
        ],
        compiler_params=pltpu.CompilerParams(dimension_semantics=("arbitrary", "arbitrary"),
                                             vmem_limit_bytes=VMEM_LIMIT),
        name="ssd",
    )(*args)


def _route(logits):
    lane = lax.broadcasted_iota(jnp.int32, logits.shape, 1)
    neg = -jnp.inf
    is_grp = (lane >= N_EXPERTS) & (lane < N_EXPERTS + N_GROUPS)
    gl = jnp.where(is_grp, logits, neg)
    gmax = jnp.max(gl, axis=-1, keepdims=True)
    ge = jnp.exp(gl - gmax)
    p_grp = ge / jnp.sum(ge, axis=-1, keepdims=True)
    p_top = jnp.max(p_grp, axis=-1, keepdims=True)
    g_idx = jnp.min(jnp.where(is_grp & (p_grp == p_top), lane, 2 * LANES), axis=-1, keepdims=True) - N_EXPERTS
    in_grp = (lane >= g_idx * EXPERTS_PER_GROUP) & (lane < (g_idx + 1) * EXPERTS_PER_GROUP)
    el = jnp.where(in_grp, logits, neg)
    emax = jnp.max(el, axis=-1, keepdims=True)
    ee = jnp.exp(el - emax)
    p_e = ee / jnp.sum(ee, axis=-1, keepdims=True)
    w1 = jnp.max(p_e, axis=-1, keepdims=True)
    i1 = jnp.min(jnp.where(in_grp & (p_e == w1), lane, 2 * LANES), axis=-1, keepdims=True)
    rest = jnp.where(in_grp & (lane != i1), p_e, -1.0)
    w2 = jnp.max(rest, axis=-1, keepdims=True)
    i2 = jnp.min(jnp.where(rest == w2, lane, 2 * LANES), axis=-1, keepdims=True)
    tot = w1 + w2
    gates = jnp.where(lane == i1, w1 / tot * p_top, 0.0) + jnp.where(lane == i2, w2 / tot * p_top, 0.0)
    return jnp.where(lane == GROUP_LANE, g_idx.astype(F32), gates)


def _post_kernel(*refs, has_pos):
    it = iter(refs)
    x_ref = next(it)
    pos_ref = next(it) if has_pos else None
    og_ref, yg_ref, gg_ref, gs_ref, ada_ref = (next(it) for _ in range(5))
    wg_ref, ws_ref, wo_ref, g1_ref, b1_ref, wr_ref, br_ref = (next(it) for _ in range(7))
    x1_ref, h2_ref, gates_ref = (next(it) for _ in range(3))

    x = x_ref[...]
    if has_pos:
        x = x + pos_ref[...]
    u_g = jnp.dot(og_ref[...], wg_ref[...], preferred_element_type=F32)
    u_s = jnp.dot(yg_ref[...], ws_ref[...], preferred_element_type=F32)
    m = _sigmoid(gg_ref[...].astype(F32)) * u_g + _sigmoid(gs_ref[...].astype(F32)) * u_s
    mix = jnp.dot(m.astype(BF16), wo_ref[...], preferred_element_type=F32)
    gate1 = ada_ref[:, 2 * D_MODEL:3 * D_MODEL]
    shift2 = ada_ref[:, 3 * D_MODEL:4 * D_MODEL]
    scale2 = ada_ref[:, 4 * D_MODEL:5 * D_MODEL]
    x1 = _ln(ALPHA * x + gate1 * mix) * g1_ref[...] + b1_ref[...]
    x1_ref[...] = x1
    h2 = _ln(x1) * (1.0 + scale2) + shift2
    h2b = h2.astype(BF16)
    h2_ref[...] = h2b
    logits = jnp.dot(h2b, wr_ref[...], preferred_element_type=F32) + br_ref[...]
    gates_ref[...] = _route(logits)


def _post(x2d, pos, og, yg, proj, ada3, w_gdn_out, w_ssd_out, w_o, ln_g, ln_b, w_router, b_router,
          seq_len, ada_row0):
    n_tok = x2d.shape[0]
    t = POST_BLOCK
    blocks_per_seq = seq_len // t
    has_pos = pos is not None

    def ada_map(i):
        return (ada_row0 + (i // blocks_per_seq if has_pos else 0), 0, 0)

    const = lambda i: (0, 0)
    in_specs = [pl.BlockSpec((t, D_MODEL), lambda i: (i, 0))]
    args = [x2d]
    if has_pos:
        in_specs.append(pl.BlockSpec((t, D_MODEL), lambda i: (i % blocks_per_seq, 0)))
        args.append(pos)
    in_specs += [
        pl.BlockSpec((t, GDN_HEADS * GDN_DV), lambda i: (i, 0)),
        pl.BlockSpec((t, SSD_INNER), lambda i: (i, 0)),
        pl.BlockSpec((t, D_MODEL), lambda i: (i, COL_GATE // D_MODEL)),
        pl.BlockSpec((t, D_MODEL), lambda i: (i, COL_GATE // D_MODEL + 1)),
        pl.BlockSpec((None, 1, 6 * D_MODEL), ada_map),
        pl.BlockSpec((GDN_HEADS * GDN_DV, D_MODEL), const, pipeline_mode=pl.Buffered(1)),
        pl.BlockSpec((SSD_INNER, D_MODEL), const, pipeline_mode=pl.Buffered(1)),
        pl.BlockSpec((D_MODEL, D_MODEL), const, pipeline_mode=pl.Buffered(1)),
        pl.BlockSpec((1, D_MODEL), const),
        pl.BlockSpec((1, D_MODEL), const),
        pl.BlockSpec((D_MODEL, LANES), const),
        pl.BlockSpec((1, LANES), const),
    ]
    args += [og, yg, proj, proj, ada3, w_gdn_out, w_ssd_out, w_o, ln_g, ln_b, w_router, b_router]
    return pl.pallas_call(
        functools.partial(_post_kernel, has_pos=has_pos),
        grid=(n_tok // t,),
        in_specs=in_specs,
        out_specs=[
            pl.BlockSpec((t, D_MODEL), lambda i: (i, 0)),
            pl.BlockSpec((t, D_MODEL), lambda i: (i, 0)),
            pl.BlockSpec((t, LANES), lambda i: (i, 0)),
        ],
        out_shape=[
            jax.ShapeDtypeStruct((n_tok, D_MODEL), F32),
            jax.ShapeDtypeStruct((n_tok, D_MODEL), BF16),
            jax.ShapeDtypeStruct((n_tok, LANES), F32),
        ],
        compiler_params=pltpu.CompilerParams(dimension_semantics=("arbitrary",), vmem_limit_bytes=VMEM_LIMIT),
        name="post",
    )(*args)


def _moe_kernel(h_ref, gates_ref, x1_ref, ada_ref, wg_ref, wu_ref, wd_ref, g2_ref, b2_ref, out_ref,
                col_ref, row_ref, gx_ref, cnt_ref):
    g = pl.program_id(1)
    t = MOE_WINDOW
    lane = lax.broadcasted_iota(jnp.int32, (t, LANES), 1)

    @pl.when(g == 0)
    def _():
        gates = gates_ref[...]
        grp = jnp.sum(jnp.where(lane == GROUP_LANE, gates, 0.0), axis=-1, keepdims=True)
        onehot = jnp.where((lane < N_GROUPS) & (grp == lane.astype(F32)), 1.0, 0.0)
        tri = (lax.broadcasted_iota(jnp.int32, (t, t), 0) >= lax.broadcasted_iota(jnp.int32, (t, t), 1))
        cum = jnp.dot(jnp.where(tri, 1.0, 0.0).astype(BF16), onehot.astype(BF16), preferred_element_type=F32)
        rank = jnp.sum(onehot * cum, axis=-1, keepdims=True) - 1.0
        info = jnp.where(lane == 0, grp, jnp.where(lane == 1, rank, 0.0))
        col_ref[...] = info
        row_ref[...] = info.T[0:SUBLANES, :]
        totals = cum[t - 1:t, :]
        for k in range(N_GROUPS):
            cnt_ref[k] = totals[0, k].astype(jnp.int32)
        hi = gates.astype(BF16).astype(F32)
        mid = (gates - hi).astype(BF16).astype(F32)
        low = gates - hi - mid
        packed = jnp.where(lane < N_EXPERTS, hi,
                           jnp.where(lane < 2 * N_EXPERTS, pltpu.roll(mid, N_EXPERTS, axis=1),
                                     jnp.where(lane < 3 * N_EXPERTS, pltpu.roll(low, 2 * N_EXPERTS, axis=1), 0.0)))
        gx_ref[...] = packed.astype(BF16)
        out_ref[...] = jnp.zeros(out_ref.shape, F32)

    n_g = cnt_ref[g]
    gf = g.astype(F32)
    def run_tile(base, rows):
        slot_r = (lax.broadcasted_iota(jnp.int32, (rows, t), 0) + base).astype(F32)
        pick = jnp.where((row_ref[1:2, :] == slot_r) & (row_ref[0:1, :] == gf), 1.0, 0.0).astype(BF16)
        hs = jnp.dot(pick, h_ref[...], preferred_element_type=F32).astype(BF16)
        gsx = jnp.dot(pick, gx_ref[...], preferred_element_type=F32)
        gs = gsx + pltpu.roll(gsx, LANES - N_EXPERTS, axis=1) + pltpu.roll(gsx, LANES - 2 * N_EXPERTS, axis=1)
        lane_t = lax.broadcasted_iota(jnp.int32, (rows, LANES), 1)
        y = jnp.zeros((rows, D_MODEL), F32)
        for e in range(EXPERTS_PER_GROUP):
            a = jnp.dot(hs, wg_ref[e], preferred_element_type=F32)
            b = jnp.dot(hs, wu_ref[e], preferred_element_type=F32)
            gate_e = jnp.sum(jnp.where(lane_t == g * EXPERTS_PER_GROUP + e, gs, 0.0), axis=-1, keepdims=True)
            y = y + jnp.dot((_silu(a) * b * gate_e).astype(BF16), wd_ref[e], preferred_element_type=F32)
        slot_c = (lax.broadcasted_iota(jnp.int32, (t, rows), 1) + base).astype(F32)
        place = jnp.where((col_ref[:, 1:2] == slot_c) & (col_ref[:, 0:1] == gf), 1.0, 0.0).astype(BF16)
        out_ref[...] += jnp.dot(place, y.astype(BF16), preferred_element_type=F32)

    for j in range(MOE_TILES):
        remaining = n_g - j * MOE_TILE

        @pl.when(remaining > MOE_TILE // 2)
        def _(j=j):
            run_tile(j * MOE_TILE, MOE_TILE)

        @pl.when((remaining > 0) & (remaining <= MOE_TILE // 2))
        def _(j=j):
            run_tile(j * MOE_TILE, MOE_TILE // 2)

    @pl.when(g == N_GROUPS - 1)
    def _():
        gate2 = ada_ref[:, 5 * D_MODEL:6 * D_MODEL]
        out_ref[...] = _ln(ALPHA * x1_ref[...] + gate2 * out_ref[...]) * g2_ref[...] + b2_ref[...]


def _moe(h2, gates, x1, ada3, w_gate, w_up, w_down, ln_g, ln_b, seq_len, ada_row0, per_seq_ada):
    n_tok = h2.shape[0]
    t = MOE_WINDOW
    blocks_per_seq = seq_len // t
    assert not per_seq_ada or seq_len % t == 0
    group_w = EXPERTS_PER_GROUP * D_EXPERT

    def ada_map(i, g):
        return (ada_row0 + (i // blocks_per_seq if per_seq_ada else 0), 0, 0)

    return pl.pallas_call(
        _moe_kernel,
        grid=(n_tok // t, N_GROUPS),
        in_specs=[
            pl.BlockSpec((t, D_MODEL), lambda i, g: (i, 0)),
            pl.BlockSpec((t, LANES), lambda i, g: (i, 0)),
            pl.BlockSpec((t, D_MODEL), lambda i, g: (i, 0), pipeline_mode=pl.Buffered(1)),
            pl.BlockSpec((None, 1, 6 * D_MODEL), ada_map),
            pl.BlockSpec((EXPERTS_PER_GROUP, D_MODEL, D_EXPERT), lambda i, g: (g, 0, 0)),
            pl.BlockSpec((EXPERTS_PER_GROUP, D_MODEL, D_EXPERT), lambda i, g: (g, 0, 0)),
            pl.BlockSpec((EXPERTS_PER_GROUP, D_EXPERT, D_MODEL), lambda i, g: (g, 0, 0)),
            pl.BlockSpec((1, D_MODEL), lambda i, g: (0, 0)),
            pl.BlockSpec((1, D_MODEL), lambda i, g: (0, 0)),
        ],
        out_specs=pl.BlockSpec((t, D_MODEL), lambda i, g: (i, 0)),
        out_shape=jax.ShapeDtypeStruct((n_tok, D_MODEL), F32),
        scratch_shapes=[
            pltpu.VMEM((t, LANES), F32),
            pltpu.VMEM((SUBLANES, t), F32),
            pltpu.VMEM((t, LANES), BF16),
            pltpu.SMEM((N_GROUPS,), jnp.int32),
        ],
        compiler_params=pltpu.CompilerParams(dimension_semantics=("arbitrary", "arbitrary"),
                                             vmem_limit_bytes=VMEM_LIMIT),
        name="moe",
    )(h2, gates, x1, ada3, w_gate, w_up, w_down, ln_g, ln_b)


def _grid_pos_embed(n_tokens, d):
    rows = n_tokens // GRID_W
    rr, cc = np.meshgrid(np.arange(rows, dtype=np.float32), np.arange(GRID_W, dtype=np.float32), indexing="ij")
    quarter = d // 4
    freqs = np.exp(-math.log(POS_BASE) * np.arange(quarter, dtype=np.float32) / quarter).astype(np.float32)
    ang_r = rr.reshape(-1, 1) * freqs
    ang_c = cc.reshape(-1, 1) * freqs
    table = np.concatenate([np.sin(ang_r), np.cos(ang_r), np.sin(ang_c), np.cos(ang_c)], axis=-1)
    return jnp.asarray(table, dtype=F32)


def _lane_rep(v):
    return jnp.broadcast_to(v[..., None], v.shape + (LANES,)).astype(F32)


def _stream(x3d, pos, ada3, ada_row0, s_gdn0, h_ssd0t, want_state, wts):
    bsz, seq_len, _ = x3d.shape
    x2d = x3d.reshape(bsz * seq_len, D_MODEL)
    proj, small_t = _in_proj(x2d, pos, ada3, wts["w_big"], wts["w_small_t"], seq_len, ada_row0)
    gdn_out = _gdn(proj, small_t, wts["gdn_conv_w"], wts["gdn_conv_b"], wts["gdn_params"],
                   wts["gdn_norm_w"], s_gdn0, bsz, seq_len, want_state)
    ssd_out = _ssd(proj, small_t, wts["ssd_conv_w"], wts["ssd_conv_b"], wts["ssd_a_log"], wts["ssd_dt_bias"],
                   wts["ssd_d"], wts["ssd_norm_w"], h_ssd0t, bsz, seq_len, want_state)
    x1, h2, gates = _post(x2d, pos, gdn_out[0], ssd_out[0], proj, ada3, wts["w_gdn_out"], wts["w_ssd_out"],
                          wts["w_o"], wts["ln1_g"], wts["ln1_b"], wts["w_router"], wts["b_router"],
                          seq_len, ada_row0)
    y = _moe(h2, gates, x1, ada3, wts["w_exp_gate"], wts["w_exp_up"], wts["w_exp_down"], wts["ln2_g"],
             wts["ln2_b"], seq_len, ada_row0, pos is not None)
    states = (gdn_out[1], ssd_out[1]) if want_state else None
    return y.reshape(bsz, seq_len, D_MODEL), states


def kernel(x_prompt, x_sample, state_gdn, state_ssd, c, c_ctx, w_ada, b_ada, w_in, gdn_conv_w, gdn_conv_b, gdn_a_log, gdn_dt_bias, gdn_norm_w, w_gdn_out, ssd_conv_w, ssd_conv_b, ssd_a_log, ssd_dt_bias, ssd_d, ssd_norm_w, w_ssd_out, w_o, ln1_g, ln1_b, w_router_group, b_router_group, w_router_expert, b_router_expert, w_exp_gate, w_exp_up, w_exp_down, ln2_g, ln2_b):
    assert w_in.shape[0] == DEPTH == 1
    l = 0
    bsz_c = x_prompt.shape[0]
    bsz_l, seq_l, _ = x_sample.shape

    cvec = jnp.zeros((SUBLANES, D_MODEL), F32).at[0].set(c_ctx).at[1:1 + bsz_l].set(c)
    ada3 = _ada(cvec, w_ada[l], b_ada[l]).reshape(SUBLANES, 1, 6 * D_MODEL)

    wi = w_in[l]
    o_zg = 3072
    o_beta = 4096
    o_a = 4112
    o_xbc = 4128
    o_zs = 7200
    o_dt = 9248
    o_gate = 9312
    w_big = jnp.concatenate([wi[:, 0:o_zg], wi[:, o_zg:o_beta], wi[:, o_xbc:o_zs], wi[:, o_zs:o_dt],
                             wi[:, o_gate:]], axis=1).astype(BF16)
    w_beta = wi[:, o_beta:o_a].T.reshape(2, GDN_HEADS, D_MODEL)
    w_a = wi[:, o_a:o_xbc].T.reshape(2, GDN_HEADS, D_MODEL)
    w_head = jnp.concatenate([w_beta, w_a, jnp.zeros((SUBLANES - 4, GDN_HEADS, D_MODEL), F32)], axis=0)
    w_head = jnp.transpose(w_head, (1, 0, 2)).reshape(ROW_DT, D_MODEL)
    w_small_t = jnp.concatenate([w_head, wi[:, o_dt:o_gate].T], axis=0).astype(BF16)
    assert w_small_t.shape == (SMALL_ROWS, D_MODEL)
    gdn_params = jnp.concatenate([gdn_a_log[l], gdn_dt_bias[l], jnp.zeros((SUBLANES - 4, GDN_HEADS), F32)], axis=0)
    gdn_params = _lane_rep(gdn_params.T)

    w_router = jnp.zeros((D_MODEL, LANES), F32)
    w_router = w_router.at[:, 0:N_EXPERTS].set(w_router_expert[l]).at[:, N_EXPERTS:N_EXPERTS + N_GROUPS].set(
        w_router_group[l])
    b_router = jnp.zeros((1, LANES), F32)
    b_router = b_router.at[0, 0:N_EXPERTS].set(b_router_expert[l]).at[0, N_EXPERTS:N_EXPERTS + N_GROUPS].set(
        b_router_group[l])

    wts = {
        "w_big": w_big, "w_small_t": w_small_t,
        "gdn_conv_w": gdn_conv_w[l], "gdn_conv_b": gdn_conv_b[l].reshape(1, -1),
        "gdn_params": gdn_params,
        "gdn_norm_w": gdn_norm_w[l].reshape(1, -1),
        "ssd_conv_w": ssd_conv_w[l], "ssd_conv_b": ssd_conv_b[l].reshape(1, -1),
        "ssd_a_log": _lane_rep(ssd_a_log[l]), "ssd_dt_bias": _lane_rep(ssd_dt_bias[l]),
        "ssd_d": jnp.repeat(ssd_d[l], SSD_P).reshape(1, -1), "ssd_norm_w": ssd_norm_w[l].reshape(1, -1),
        "w_gdn_out": w_gdn_out[l].astype(BF16), "w_ssd_out": w_ssd_out[l].astype(BF16),
        "w_o": w_o[l].astype(BF16),
        "ln1_g": ln1_g[l].reshape(1, -1), "ln1_b": ln1_b[l].reshape(1, -1),
        "w_router": w_router.astype(BF16), "b_router": b_router,
        "w_exp_gate": w_exp_gate[l].astype(BF16), "w_exp_up": w_exp_up[l].astype(BF16),
        "w_exp_down": w_exp_down[l].astype(BF16),
        "ln2_g": ln2_g[l].reshape(1, -1), "ln2_b": ln2_b[l].reshape(1, -1),
    }

    pos = _grid_pos_embed(seq_l, D_MODEL)
    h0 = state_ssd[:, l].reshape(bsz_l, 2, SSD_GROUPS, GROUP_W, SSD_N)

    y_ctx, (s_gdn, h_ssd) = _stream(x_prompt, None, ada3, 0, None, None, True, wts)
    y_lat, _ = _stream(x_sample, pos, ada3, 1, state_gdn, h0, False, wts)

    new_ssd = h_ssd.reshape(bsz_c, 1, 2, SSD_HEADS, SSD_P, SSD_N)
    return (y_ctx, y_lat, s_gdn, new_ssd)
```

```python
import functools
import math

import jax
import jax.numpy as jnp
import numpy as np
from jax import lax
from jax.experimental import pallas as pl
from jax.experimental.pallas import tpu as pltpu

F32 = jnp.float32
BF16 = jnp.bfloat16
HIGHEST = lax.Precision.HIGHEST

D_MODEL = 1024
GRID_W = 64
POS_BASE = 10000.0
CONV_K = 5
GDN_HEADS = 8
GDN_DK = 128
GDN_DV = 128
SSD_HEADS = 32
SSD_P = 64
SSD_INNER = SSD_HEADS * SSD_P
SSD_GROUPS = 4
SSD_N = 128
HEADS_PER_GROUP = SSD_HEADS // SSD_GROUPS
GROUP_W = HEADS_PER_GROUP * SSD_P
N_GROUPS = 4
EXPERTS_PER_GROUP = 8
N_EXPERTS = N_GROUPS * EXPERTS_PER_GROUP
D_EXPERT = 256
EPS = 1e-6
DEPTH = 1
ALPHA = (2.0 * DEPTH) ** 0.25

LANES = 128
SUBLANES = 8
CHUNK = 128
CONV_PAD = SUBLANES
VMEM_LIMIT = 56 * 1024 * 1024

COL_QKV = 0
COL_ZG = 3072
COL_XBC = 4096
COL_ZS = 7168
COL_GATE = 9216
PROJ_COLS = 11264
PROJ_TN = 1024
GDN_ROW_BETA = 0
GDN_ROW_A = 2
ROW_DT = GDN_HEADS * SUBLANES
SMALL_ROWS = 128

TOK_BLOCK = 2048
POST_BLOCK = 512
MOE_WINDOW = 1024
MOE_TILE = 288
GROUP_LANE = N_EXPERTS
SSD_CHUNKS_PER_STEP = 8


def _bdot(a, b):
    return jnp.dot(a.astype(BF16), b.astype(BF16), preferred_element_type=F32)


def _sigmoid(x):
    return 0.5 * jnp.tanh(0.5 * x) + 0.5


def _silu(x):
    return x * _sigmoid(x)


def _softplus(x):
    return jnp.maximum(x, 0.0) + jnp.log1p(jnp.exp(-jnp.abs(x)))


def _ln(x):
    mu = jnp.mean(x, axis=-1, keepdims=True)
    xc = x - mu
    var = jnp.mean(xc * xc, axis=-1, keepdims=True)
    return xc * lax.rsqrt(var + EPS)


def _lane_scan(x, reverse):
    lane = lax.broadcasted_iota(jnp.int32, x.shape, 1)
    s = 1
    while s < CHUNK:
        if reverse:
            shifted = pltpu.roll(x, CHUNK - s, axis=1)
            x = x + jnp.where(lane < CHUNK - s, shifted, 0.0)
        else:
            shifted = pltpu.roll(x, s, axis=1)
            x = x + jnp.where(lane >= s, shifted, 0.0)
        s *= 2
    return x


def _tri_masks(reverse):
    row = lax.broadcasted_iota(jnp.int32, (CHUNK, CHUNK), 0)
    col = lax.broadcasted_iota(jnp.int32, (CHUNK, CHUNK), 1)
    if reverse:
        return row <= col, row < col
    return row >= col, row > col


def _decay_matrix(acc_row, incl):
    acc_rb = jnp.broadcast_to(acc_row, (CHUNK, CHUNK))
    acc_col = acc_rb.T
    decay = jnp.exp(jnp.where(incl, acc_col - acc_rb, -jnp.inf))
    return decay, acc_col


def _conv_tap_sum(pad_ref, t0, lo, hi, w_ref, b_ref):
    acc = jnp.broadcast_to(b_ref[...], (CHUNK, hi - lo))
    for j in range(CONV_K):
        start = t0 + CONV_PAD - CONV_K // 2 + j
        acc = acc + pad_ref[start:start + CHUNK, lo:hi] * w_ref[j:j + 1, :]
    return acc


def _ada_kernel(c_ref, w_ref, b_ref, o_ref):
    s = _silu(c_ref[...])
    o_ref[...] = jnp.dot(s, w_ref[...], precision=HIGHEST, preferred_element_type=F32) + b_ref[...]


def _ada(cvec, w_ada, b_ada):
    n_out = w_ada.shape[1]
    tn = 1024
    return pl.pallas_call(
        _ada_kernel,
        grid=(n_out // tn,),
        in_specs=[
            pl.BlockSpec((SUBLANES, D_MODEL), lambda j: (0, 0)),
            pl.BlockSpec((D_MODEL, tn), lambda j: (0, j)),
            pl.BlockSpec((1, tn), lambda j: (0, j)),
        ],
        out_specs=pl.BlockSpec((SUBLANES, tn), lambda j: (0, j)),
        out_shape=jax.ShapeDtypeStruct((SUBLANES, n_out), F32),
        compiler_params=pltpu.CompilerParams(dimension_semantics=("arbitrary",), vmem_limit_bytes=VMEM_LIMIT),
        name="ada",
    )(cvec, w_ada, b_ada.reshape(1, n_out))


def _inproj_kernel(*refs, has_pos):
    if has_pos:
        x_ref, pos_ref, ada_ref, w_ref, wst_ref, proj_ref, small_ref, h_ref = refs
    else:
        x_ref, ada_ref, w_ref, wst_ref, proj_ref, small_ref, h_ref = refs
        pos_ref = None

    @pl.when(pl.program_id(1) == 0)
    def _():
        x = x_ref[...]
        if pos_ref is not None:
            x = x + pos_ref[...]
        shift = ada_ref[:, 0:D_MODEL]
        scale = ada_ref[:, D_MODEL:2 * D_MODEL]
        h = (_ln(x) * (1.0 + scale) + shift).astype(BF16)
        h_ref[...] = h
        small_ref[...] = lax.dot_general(wst_ref[...], h, (((1,), (1,)), ((), ())), preferred_element_type=F32)

    proj_ref[...] = jnp.dot(h_ref[...], w_ref[...], preferred_element_type=F32).astype(BF16)


def _in_proj(x2d, pos, ada3, w_big, w_small_t, seq_len, ada_row0):
    n_tok = x2d.shape[0]
    t = TOK_BLOCK
    blocks_per_seq = seq_len // t
    has_pos = pos is not None

    def ada_map(i, j):
        return (ada_row0 + (i // blocks_per_seq if has_pos else 0), 0, 0)

    in_specs = [pl.BlockSpec((t, D_MODEL), lambda i, j: (i, 0))]
    args = [x2d]
    if has_pos:
        pos_mode = dict(pipeline_mode=pl.Buffered(1)) if blocks_per_seq == 1 else {}
        in_specs.append(pl.BlockSpec((t, D_MODEL), lambda i, j: (i % blocks_per_seq, 0), **pos_mode))
        args.append(pos)
    in_specs += [
        pl.BlockSpec((None, 1, 6 * D_MODEL), ada_map),
        pl.BlockSpec((D_MODEL, PROJ_TN), lambda i, j: (0, j)),
        pl.BlockSpec((SMALL_ROWS, D_MODEL), lambda i, j: (0, 0)),
    ]
    args += [ada3, w_big, w_small_t]
    return pl.pallas_call(
        functools.partial(_inproj_kernel, has_pos=has_pos),
        grid=(n_tok // t, PROJ_COLS // PROJ_TN),
        in_specs=in_specs,
        out_specs=[
            pl.BlockSpec((t, PROJ_TN), lambda i, j: (i, j)),
            pl.BlockSpec((SMALL_ROWS, t), lambda i, j: (0, i)),
        ],
        out_shape=[
            jax.ShapeDtypeStruct((n_tok, PROJ_COLS), BF16),
            jax.ShapeDtypeStruct((SMALL_ROWS, n_tok), F32),
        ],
        scratch_shapes=[pltpu.VMEM((t, D_MODEL), BF16)],
        compiler_params=pltpu.CompilerParams(dimension_semantics=("arbitrary", "arbitrary"),
                                             vmem_limit_bytes=VMEM_LIMIT),
        name="in_proj",
    )(*args)


INV_BASE = 8
GDN_GROUP = 8


def _take_blocks(m, size, odd):
    parts = [m[k * size:(k + 1) * size] for k in range(CHUNK // size) if (k % 2 == 1) == odd]
    return parts[0] if len(parts) == 1 else jnp.concatenate(parts, axis=0)


def _interleave_blocks(even_rows, odd_rows, size):
    parts = []
    for k in range(CHUNK // (2 * size)):
        parts.append(even_rows[k * size:(k + 1) * size])
        parts.append(odd_rows[k * size:(k + 1) * size])
    return jnp.concatenate(parts, axis=0)


def _unit_tri_inverses(nmats, uppers):
    row = lax.broadcasted_iota(jnp.int32, (CHUNK, CHUNK), 0)
    col = lax.broadcasted_iota(jnp.int32, (CHUNK, CHUNK), 1)

    def same_block(size):
        shift = int(math.log2(size))
        return (row >> shift) == (col >> shift)

    eye = (row == col).astype(F32)
    base = same_block(INV_BASE)
    nds = [jnp.where(base, n, 0.0) for n in nmats]
    xs = [eye - nd for nd in nds]
    pws = [_bdot(nd, nd) for nd in nds]
    size = 2
    while True:
        xs = [x + _bdot(x, pw) for x, pw in zip(xs, pws)]
        size *= 2
        if size >= INV_BASE:
            break
        pws = [_bdot(pw, pw) for pw in pws]
    size = INV_BASE
    while size < CHUNK:
        coupling = same_block(2 * size) & jnp.logical_not(same_block(size))
        offs = [_take_blocks(jnp.where(coupling, n, 0.0), size, odd=not up) for n, up in zip(nmats, uppers)]
        tmps = [_bdot(off, x) for off, x in zip(offs, xs)]
        zeros = jnp.zeros((CHUNK // 2, CHUNK), F32)
        fulls = [_interleave_blocks(t, zeros, size) if up else _interleave_blocks(zeros, t, size)
                 for t, up in zip(tmps, uppers)]
        moved = [_take_blocks(x, size, odd=not up) for x, up in zip(xs, uppers)]
        kept = [_take_blocks(x, size, odd=up) for x, up in zip(xs, uppers)]
        news = [m - _bdot(m, f) for m, f in zip(moved, fulls)]
        xs = [_interleave_blocks(nw, kp, size) if up else _interleave_blocks(kp, nw, size)
              for nw, kp, up in zip(news, kept, uppers)]
        size *= 2
    return xs


def _gdn_prepare(chunks, qn_ref, kn_ref, vn_ref, kt_ref, sc_ref, lhs_ref, sb_ref, ob_ref, eg_ref):
    grams = [_bdot(jnp.concatenate([kn_ref[g], qn_ref[g]], axis=0), kt_ref[g]) for g in chunks]
    units = []
    for g, gram in zip(chunks, grams):
        for d in range(2):
            reverse = d == 1
            incl, strict = _tri_masks(reverse)
            beta = sc_ref[g, d:d + 1, :]
            gc = sc_ref[g, 2 + d:3 + d, :]
            decay, gcol = _decay_matrix(gc, incl)
            beta_rb = jnp.broadcast_to(beta, (CHUNK, CHUNK))
            glast = gc[:, 0:1] if reverse else gc[:, CHUNK - 1:CHUNK]
            sc_ref[g, 4 + d:5 + d, :] = jnp.broadcast_to(jnp.exp(glast), (1, CHUNK))
            eg_ref[g, d] = jnp.exp(gcol)
            units.append(dict(
                g=g, d=d,
                nmat=jnp.where(strict, gram[0:CHUNK] * decay, 0.0) * beta_rb,
                att=gram[CHUNK:2 * CHUNK] * decay * beta_rb,
                kd_scale=beta * jnp.exp(glast - gc)))
    xinvs = _unit_tri_inverses([u["nmat"] for u in units], [u["d"] == 1 for u in units])
    uws = [_bdot(x, jnp.concatenate([vn_ref[u["g"]], kn_ref[u["g"]] * eg_ref[u["g"], u["d"]]], axis=1))
           for u, x in zip(units, xinvs)]
    mixeds = []
    for u, uw in zip(units, uws):
        kd = kt_ref[u["g"]] * jnp.broadcast_to(u["kd_scale"], (CHUNK, CHUNK))
        mixeds.append(_bdot(jnp.concatenate([kd, u["att"]], axis=0), uw))
    for u, mixed in zip(units, mixeds):
        g, d = u["g"], u["d"]
        lhs_ref[g, d, 0:CHUNK, :] = mixed[0:CHUNK, GDN_DV:].astype(BF16)
        lhs_ref[g, d, CHUNK:2 * CHUNK, :] = qn_ref[g].astype(BF16)
        lhs_ref[g, d, 2 * CHUNK:3 * CHUNK, :] = mixed[CHUNK:, GDN_DV:].astype(BF16)
        sb_ref[g, d] = mixed[0:CHUNK, 0:GDN_DV]
        ob_ref[g, d] = mixed[CHUNK:, 0:GDN_DV]


def _gdn_steps(steps, lhs_ref, sb_ref, ob_ref, eg_ref, sc_ref, s_ref, o_refs):
    states = [s_ref[s, d] for _, s, d in steps]
    rs = [jnp.dot(lhs_ref[g, d], st.astype(BF16), preferred_element_type=F32)
          for (g, _, d), st in zip(steps, states)]
    for (g, s, d), st, r in zip(steps, states, rs):
        s_ref[s, d] = st * sc_ref[g, 4 + d:5 + d, :] - r[0:CHUNK] + sb_ref[g, d]
        o_refs[d][g] = eg_ref[g, d] * r[CHUNK:2 * CHUNK] - r[2 * CHUNK:3 * CHUNK] + ob_ref[g, d]


def _gdn_kernel(*refs, n_seq, n_chunks, has_s0, want_state):
    it = iter(refs)
    q_ref, k_ref, v_ref, z_ref, sm_ref = (next(it) for _ in range(5))
    wq_ref, wk_ref, wv_ref, bq_ref, bk_ref, bv_ref = (next(it) for _ in range(6))
    par_ref, nw_ref = (next(it) for _ in range(2))
    s0_ref = next(it) if has_s0 else None
    og_ref = next(it)
    sout_ref = next(it) if want_state else None
    pad_ref, qn_ref, kn_ref, vn_ref, kt_ref, of_ref, ob_ref, sc_ref, s_ref = (next(it) for _ in range(9))
    lhs_ref, sb_ref, ou_ref, eg_ref, st_ref = (next(it) for _ in range(5))

    seq = n_chunks * CHUNK
    total = n_seq * n_chunks
    zero_rows = jnp.zeros((CONV_PAD, 3 * LANES), F32)
    for s in range(n_seq):
        pad_ref[s, 0:CONV_PAD, :] = zero_rows
        pad_ref[s, CONV_PAD + seq:2 * CONV_PAD + seq, :] = zero_rows
        for c in range(n_chunks):
            t0 = c * CHUNK
            r0 = s * seq + t0
            pad_ref[s, CONV_PAD + t0:CONV_PAD + t0 + CHUNK, 0:LANES] = q_ref[r0:r0 + CHUNK, :].astype(F32)
            pad_ref[s, CONV_PAD + t0:CONV_PAD + t0 + CHUNK, LANES:2 * LANES] = k_ref[r0:r0 + CHUNK, :].astype(F32)
            pad_ref[s, CONV_PAD + t0:CONV_PAD + t0 + CHUNK, 2 * LANES:3 * LANES] = v_ref[r0:r0 + CHUNK, :].astype(F32)

    a_neg = [-jnp.exp(par_ref[d:d + 1, :]) for d in range(2)]
    dt_bias = [par_ref[2 + d:3 + d, :] for d in range(2)]
    for s in range(n_seq):
        for c in range(n_chunks):
            t0 = c * CHUNK
            r0 = s * seq + t0
            g = s * n_chunks + c
            pad_s = pad_ref.at[s]
            qa = _silu(_conv_tap_sum(pad_s, t0, 0, LANES, wq_ref, bq_ref))
            ka = _silu(_conv_tap_sum(pad_s, t0, LANES, 2 * LANES, wk_ref, bk_ref))
            va = _silu(_conv_tap_sum(pad_s, t0, 2 * LANES, 3 * LANES, wv_ref, bv_ref))
            qn = qa * lax.rsqrt(jnp.sum(qa * qa, axis=-1, keepdims=True) + EPS) * (GDN_DK ** -0.5)
            kn = ka * lax.rsqrt(jnp.sum(ka * ka, axis=-1, keepdims=True) + EPS)
            qn_ref[g] = qn
            kn_ref[g] = kn
            vn_ref[g] = va
            kt_ref[g] = kn.T

    for g in range(total):
        r0 = g * CHUNK
        for d in range(2):
            st_ref[d, g:g + 1, :] = sm_ref[GDN_ROW_A + d:GDN_ROW_A + d + 1, r0:r0 + CHUNK]
            st_ref[2 + d, g:g + 1, :] = sm_ref[GDN_ROW_BETA + d:GDN_ROW_BETA + d + 1, r0:r0 + CHUNK]
    for d in range(2):
        gcum = _lane_scan(a_neg[d] * _softplus(st_ref[d] + dt_bias[d]), reverse=d == 1)
        beta = _sigmoid(st_ref[2 + d])
        for g in range(total):
            sc_ref[g, d:d + 1, :] = beta[g:g + 1, :]
            sc_ref[g, 2 + d:3 + d, :] = gcum[g:g + 1, :]

    def prepare(i, carry):
        _gdn_prepare([i * GDN_GROUP + j for j in range(GDN_GROUP)], qn_ref, kn_ref, vn_ref, kt_ref, sc_ref,
                     lhs_ref, sb_ref, ou_ref, eg_ref)
        return carry

    lax.fori_loop(0, total // GDN_GROUP, prepare, 0)

    if has_s0:
        s_ref[...] = s0_ref[...]
    else:
        s_ref[...] = jnp.zeros(s_ref.shape, F32)

    def advance(i, carry):
        steps = []
        for s in range(n_seq):
            steps.append((s * n_chunks + i, s, 0))
            steps.append((s * n_chunks + n_chunks - 1 - i, s, 1))
        _gdn_steps(steps, lhs_ref, sb_ref, ou_ref, eg_ref, sc_ref, s_ref, (of_ref, ob_ref))
        return carry

    lax.fori_loop(0, n_chunks, advance, 0)

    if want_state:
        sout_ref[...] = s_ref[...]
    for g in range(total):
        r0 = g * CHUNK
        o = of_ref[g] + ob_ref[g]
        o = o * lax.rsqrt(jnp.mean(o * o, axis=-1, keepdims=True) + EPS)
        og_ref[r0:r0 + CHUNK, :] = (o * nw_ref[...] * _silu(z_ref[r0:r0 + CHUNK, :].astype(F32))).astype(BF16)


def _gdn(proj, small_t, conv_w, conv_b, head_params, norm_w, s0, bsz, seq_len, want_state):
    n_chunks = seq_len // CHUNK
    n_tok = bsz * seq_len
    has_s0 = s0 is not None
    n_seq = max(1, GDN_GROUP * 2 // n_chunks)
    assert bsz % n_seq == 0 and (n_seq * n_chunks) % GDN_GROUP == 0
    total = n_seq * n_chunks
    rows = n_seq * seq_len
    col = lambda off: (lambda b, h: (b, off + h))
    cw = lambda off: (lambda b, h: (0, off + h))
    in_specs = [
        pl.BlockSpec((rows, LANES), col(COL_QKV // LANES)),
        pl.BlockSpec((rows, LANES), col(COL_QKV // LANES + GDN_HEADS)),
        pl.BlockSpec((rows, LANES), col(COL_QKV // LANES + 2 * GDN_HEADS)),
        pl.BlockSpec((rows, LANES), col(COL_ZG // LANES)),
        pl.BlockSpec((SUBLANES, rows), lambda b, h: (h, b)),
        pl.BlockSpec((CONV_K, LANES), cw(0)),
        pl.BlockSpec((CONV_K, LANES), cw(GDN_HEADS)),
        pl.BlockSpec((CONV_K, LANES), cw(2 * GDN_HEADS)),
        pl.BlockSpec((1, LANES), cw(0)),
        pl.BlockSpec((1, LANES), cw(GDN_HEADS)),
        pl.BlockSpec((1, LANES), cw(2 * GDN_HEADS)),
        pl.BlockSpec((None, SUBLANES, LANES), lambda b, h: (h, 0, 0)),
        pl.BlockSpec((1, LANES), lambda b, h: (0, 0)),
    ]
    args = [proj, proj, proj, proj, small_t, conv_w, conv_w, conv_w, conv_b, conv_b, conv_b,
            head_params, norm_w]
    state_spec = pl.BlockSpec((n_seq, None, 2, None, GDN_DK, GDN_DV), lambda b, h: (b, 0, 0, h, 0, 0))
    if has_s0:
        in_specs.append(state_spec)
        args.append(s0)
    out_specs = [pl.BlockSpec((rows, LANES), lambda b, h: (b, h))]
    out_shape = [jax.ShapeDtypeStruct((n_tok, GDN_HEADS * GDN_DV), BF16)]
    if want_state:
        out_specs.append(state_spec)
        out_shape.append(jax.ShapeDtypeStruct((bsz, 1, 2, GDN_HEADS, GDN_DK, GDN_DV), F32))
    chunked = pltpu.VMEM((total, CHUNK, LANES), F32)
    per_dir = pltpu.VMEM((total, 2, CHUNK, LANES), F32)
    return pl.pallas_call(
        functools.partial(_gdn_kernel, n_seq=n_seq, n_chunks=n_chunks, has_s0=has_s0, want_state=want_state),
        grid=(bsz // n_seq, GDN_HEADS),
        in_specs=in_specs,
        out_specs=out_specs,
        out_shape=out_shape,
        scratch_shapes=[
            pltpu.VMEM((n_seq, seq_len + 2 * CONV_PAD, 3 * LANES), F32),
            chunked, chunked, chunked, chunked, chunked, chunked,
            pltpu.VMEM((total, SUBLANES, CHUNK), F32),
            pltpu.VMEM((n_seq, 2, GDN_DK, GDN_DV), F32),
            pltpu.VMEM((total, 2, 3 * CHUNK, GDN_DK), BF16),
            per_dir, per_dir, per_dir,
            pltpu.VMEM((4, total, CHUNK), F32),
        ],
        compiler_params=pltpu.CompilerParams(dimension_semantics=("arbitrary", "arbitrary"),
                                             vmem_limit_bytes=VMEM_LIMIT),
        name="gdn",
    )(*args)


def _ssd_steps(steps, xs_ref, cs_ref, bt_ref, dt_ref, ac_ref, ht_ref, y_ref):
    lane = lax.broadcasted_iota(jnp.int32, (CHUNK, LANES), 1)
    low_half = lane < SSD_P
    shared = [(_bdot(cs_ref[c], bt_ref[c]), _bdot(cs_ref[c], ht_ref[s, d])) for c, s, d in steps]
    pairs_per_stage = 2
    for p0 in range(0, HEADS_PER_GROUP // 2, pairs_per_stage):
        units = []
        for (c, s, d), (cb, y_off) in zip(steps, shared):
            reverse = d == 1
            incl, _ = _tri_masks(reverse)
            bt = bt_ref[c]
            for p in range(p0, p0 + pairs_per_stage):
                halves = []
                for r in (2 * p, 2 * p + 1):
                    row = d * HEADS_PER_GROUP + r
                    dt = dt_ref[c, row:row + 1, :]
                    ac = ac_ref[c, row:row + 1, :]
                    decay, acol = _decay_matrix(ac, incl)
                    last = ac[:, 0:1] if reverse else ac[:, CHUNK - 1:CHUNK]
                    halves.append(dict(
                        m=(cb * decay * jnp.broadcast_to(dt, (CHUNK, CHUNK))).astype(BF16),
                        bts=(bt * jnp.broadcast_to(dt * jnp.exp(last - ac), (SSD_N, CHUNK))).astype(BF16),
                        acol=acol, elast=jnp.exp(last)))
                units.append(dict(c=c, s=s, d=d, p=p, halves=halves, y_off=y_off[:, p * LANES:(p + 1) * LANES]))
        for u in units:
            x = xs_ref[u["c"], :, u["p"] * LANES:(u["p"] + 1) * LANES]
            zero = jnp.zeros_like(x)
            xh = (jnp.where(low_half, x, zero), jnp.where(low_half, zero, x))
            u["yd"] = [jnp.dot(hf["m"], xm, preferred_element_type=F32) for hf, xm in zip(u["halves"], xh)]
            u["st"] = [jnp.dot(hf["bts"], xm, preferred_element_type=F32) for hf, xm in zip(u["halves"], xh)]
        for u in units:
            c, s, d, p = u["c"], u["s"], u["d"], u["p"]
            h0, h1 = u["halves"]
            eoff = jnp.exp(jnp.where(low_half, h0["acol"], h1["acol"]))
            elast = jnp.where(low_half, h0["elast"], h1["elast"])
            cols = slice(p * LANES, (p + 1) * LANES)
            y_ref[c, :, cols] = y_ref[c, :, cols] + (u["yd"][0] + u["yd"][1]) + eoff * u["y_off"]
            ht_ref[s, d, :, cols] = ht_ref[s, d, :, cols] * elast + (u["st"][0] + u["st"][1])


def _ssd_kernel(*refs, n_seq, n_chunks, has_h0, want_state):
    it = iter(refs)
    x_ref, b_ref, c_ref, z_ref, dtf_ref, dtb_ref = (next(it) for _ in range(6))
    wx_ref, wb_ref, wc_ref, bx_ref, bb_ref, bc_ref = (next(it) for _ in range(6))
    alog_ref, dtbias_ref, dvec_ref, nw_ref = (next(it) for _ in range(4))
    h0_ref = next(it) if has_h0 else None
    yg_ref = next(it)
    hout_ref = next(it) if want_state else None
    pad_ref, xs_ref, cs_ref, bt_ref, dt_ref, ac_ref, y_ref, ht_ref = (next(it) for _ in range(8))

    seq = n_chunks * CHUNK
    total = n_seq * n_chunks
    width = GROUP_W + 2 * SSD_N
    zero_rows = jnp.zeros((CONV_PAD, width), F32)
    for s in range(n_seq):
        pad_ref[s, 0:CONV_PAD, :] = zero_rows
        pad_ref[s, CONV_PAD + seq:2 * CONV_PAD + seq, :] = zero_rows
        for c in range(n_chunks):
            t0 = c * CHUNK
            r0 = s * seq + t0
            pad_ref[s, CONV_PAD + t0:CONV_PAD + t0 + CHUNK, 0:GROUP_W] = x_ref[r0:r0 + CHUNK, :].astype(F32)
            pad_ref[s, CONV_PAD + t0:CONV_PAD + t0 + CHUNK, GROUP_W:GROUP_W + SSD_N] = b_ref[r0:r0 + CHUNK, :].astype(F32)
            pad_ref[s, CONV_PAD + t0:CONV_PAD + t0 + CHUNK, GROUP_W + SSD_N:width] = c_ref[r0:r0 + CHUNK, :].astype(F32)

    raw_dt = (dtf_ref, dtb_ref)
    for s in range(n_seq):
        pad_s = pad_ref.at[s]
        for c in range(n_chunks):
            t0 = c * CHUNK
            g = s * n_chunks + c
            for lo in range(0, GROUP_W, LANES):
                cols = slice(lo, lo + LANES)
                xa = _silu(_conv_tap_sum(pad_s, t0, lo, lo + LANES, wx_ref.at[:, cols], bx_ref.at[:, cols]))
                xs_ref[g, :, cols] = xa.astype(BF16)
                y_ref[g, :, cols] = xa * dvec_ref[:, cols]
            ba = _silu(_conv_tap_sum(pad_s, t0, GROUP_W, GROUP_W + SSD_N, wb_ref, bb_ref))
            ca = _silu(_conv_tap_sum(pad_s, t0, GROUP_W + SSD_N, width, wc_ref, bc_ref))
            cs_ref[g] = ca
            bt_ref[g] = ba.T
            for d in range(2):
                dt_ref[g, d * HEADS_PER_GROUP:(d + 1) * HEADS_PER_GROUP, :] = raw_dt[d][:, g * CHUNK:(g + 1) * CHUNK]

    for d in range(2):
        lo, hi = d * HEADS_PER_GROUP, (d + 1) * HEADS_PER_GROUP
        dt = _softplus(dt_ref[:, lo:hi, :] + dtbias_ref[d][None])
        dt_ref[:, lo:hi, :] = dt
        scaled = (dt * (-jnp.exp(alog_ref[d]))[None]).reshape(total * HEADS_PER_GROUP, CHUNK)
        ac_ref[:, lo:hi, :] = _lane_scan(scaled, reverse=d == 1).reshape(total, HEADS_PER_GROUP, CHUNK)

    if has_h0:
        for s in range(n_seq):
            for d in range(2):
                for lo in range(0, GROUP_W, LANES):
                    ht_ref[s, d, :, lo:lo + LANES] = h0_ref[s, d, lo:lo + LANES, :].T
    else:
        ht_ref[...] = jnp.zeros(ht_ref.shape, F32)

    def body(i, carry):
        for s in range(n_seq):
            _ssd_steps([(s * n_chunks + i, s, 0), (s * n_chunks + n_chunks - 1 - i, s, 1)],
                       xs_ref, cs_ref, bt_ref, dt_ref, ac_ref, ht_ref, y_ref)
        return carry

    lax.fori_loop(0, n_chunks, body, 0)

    if want_state:
        for s in range(n_seq):
            for d in range(2):
                for lo in range(0, GROUP_W, LANES):
                    hout_ref[s, d, lo:lo + LANES, :] = ht_ref[s, d, :, lo:lo + LANES].T
    for g in range(total):
        r0 = g * CHUNK
        ssq = jnp.zeros((CHUNK, 1), F32)
        for lo in range(0, GROUP_W, LANES):
            cols = slice(lo, lo + LANES)
            y = y_ref[g, :, cols] * _silu(z_ref[r0:r0 + CHUNK, cols].astype(F32))
            y_ref[g, :, cols] = y
            ssq = ssq + jnp.sum(y * y, axis=-1, keepdims=True)
        inv = lax.rsqrt(ssq * (1.0 / GROUP_W) + EPS)
        for lo in range(0, GROUP_W, LANES):
            cols = slice(lo, lo + LANES)
            yg_ref[r0:r0 + CHUNK, cols] = (y_ref[g, :, cols] * inv * nw_ref[:, cols]).astype(BF16)


def _ssd(proj, small_t, conv_w, conv_b, a_log_rep, dt_bias_rep, d_vec, norm_w, h0t, bsz, seq_len, want_state):
    n_chunks = seq_len // CHUNK
    n_tok = bsz * seq_len
    has_h0 = h0t is not None
    n_seq = max(1, SSD_CHUNKS_PER_STEP // n_chunks)
    assert bsz % n_seq == 0
    total = n_seq * n_chunks
    rows_per_step = n_seq * seq_len
    xbc_w = COL_XBC // GROUP_W
    bc_l = (COL_XBC + SSD_INNER) // LANES
    in_specs = [
        pl.BlockSpec((rows_per_step, GROUP_W), lambda b, g: (b, xbc_w + g)),
        pl.BlockSpec((rows_per_step, SSD_N), lambda b, g: (b, bc_l + g)),
        pl.BlockSpec((rows_per_step, SSD_N), lambda b, g: (b, bc_l + SSD_GROUPS + g)),
        pl.BlockSpec((rows_per_step, GROUP_W), lambda b, g: (b, COL_ZS // GROUP_W + g)),
        pl.BlockSpec((HEADS_PER_GROUP, rows_per_step), lambda b, g: (ROW_DT // HEADS_PER_GROUP + g, b)),
        pl.BlockSpec((HEADS_PER_GROUP, rows_per_step),
                     lambda b, g: ((ROW_DT + SSD_HEADS) // HEADS_PER_GROUP + g, b)),
        pl.BlockSpec((CONV_K, GROUP_W), lambda b, g: (0, g)),
        pl.BlockSpec((CONV_K, SSD_N), lambda b, g: (0, SSD_INNER // SSD_N + g)),
        pl.BlockSpec((CONV_K, SSD_N), lambda b, g: (0, SSD_INNER // SSD_N + SSD_GROUPS + g)),
        pl.BlockSpec((1, GROUP_W), lambda b, g: (0, g)),
        pl.BlockSpec((1, SSD_N), lambda b, g: (0, SSD_INNER // SSD_N + g)),
        pl.BlockSpec((1, SSD_N), lambda b, g: (0, SSD_INNER // SSD_N + SSD_GROUPS + g)),
        pl.BlockSpec((2, HEADS_PER_GROUP, LANES), lambda b, g: (0, g, 0)),
        pl.BlockSpec((2, HEADS_PER_GROUP, LANES), lambda b, g: (0, g, 0)),
        pl.BlockSpec((1, GROUP_W), lambda b, g: (0, g)),
        pl.BlockSpec((1, GROUP_W), lambda b, g: (0, g)),
    ]
    args = [proj, proj, proj, proj, small_t, small_t, conv_w, conv_w, conv_w, conv_b, conv_b, conv_b,
            a_log_rep, dt_bias_rep, d_vec, norm_w]
    state_spec = pl.BlockSpec((n_seq, 2, None, GROUP_W, SSD_N), lambda b, g: (b, 0, g, 0, 0))
    if has_h0:
        in_specs.append(state_spec)
        args.append(h0t)
    out_specs = [pl.BlockSpec((rows_per_step, GROUP_W), lambda b, g: (b, g))]
    out_shape = [jax.ShapeDtypeStruct((n_tok, SSD_INNER), BF16)]
    if want_state:
        out_specs.append(state_spec)
        out_shape.append(jax.ShapeDtypeStruct((bsz, 2, SSD_GROUPS, GROUP_W, SSD_N), F32))
    rows = pltpu.VMEM((total, 2 * HEADS_PER_GROUP, CHUNK), F32)
    return pl.pallas_call(
        functools.partial(_ssd_kernel, n_seq=n_seq, n_chunks=n_chunks, has_h0=has_h0, want_state=want_state),
        grid=(bsz // n_seq, SSD_GROUPS),
        in_specs=in_specs,
        out_specs=out_specs,
        out_shape=out_shape,
        scratch_shapes=[
            pltpu.VMEM((n_seq, seq_len + 2 * CONV_PAD, GROUP_W + 2 * SSD_N), F32),
            pltpu.VMEM((total, CHUNK, GROUP_W), BF16),
            pltpu.VMEM((total, CHUNK, SSD_N), F32),
            pltpu.VMEM((total, SSD_N, CHUNK), F32),
            rows, rows,
            pltpu.VMEM((total, CHUNK, GROUP_W), F32),
            pltpu.VMEM((n_seq, 2, SSD_N, GROUP_W), F32),
        ],
        compiler_params=pltpu.CompilerParams(dimension_semantics=("arbitrary", "arbitrary"),
                                             vmem_limit_bytes=VMEM_LIMIT),
        name="ssd",
    )(*args)


def _route(logits):
    lane = lax.broadcasted_iota(jnp.int32, logits.shape, 1)
    neg = -jnp.inf
    is_grp = (lane >= N_EXPERTS) & (lane < N_EXPERTS + N_GROUPS)
    gl = jnp.where(is_grp, logits, neg)
    gmax = jnp.max(gl, axis=-1, keepdims=True)
    ge = jnp.exp(gl - gmax)
    p_grp = ge / jnp.sum(ge, axis=-1, keepdims=True)
    p_top = jnp.max(p_grp, axis=-1, keepdims=True)
    g_idx = jnp.min(jnp.where(is_grp & (p_grp == p_top), lane, 2 * LANES), axis=-1, keepdims=True) - N_EXPERTS
    in_grp = (lane >= g_idx * EXPERTS_PER_GROUP) & (lane < (g_idx + 1) * EXPERTS_PER_GROUP)
    el = jnp.where(in_grp, logits, neg)
    emax = jnp.max(el, axis=-1, keepdims=True)
    ee = jnp.exp(el - emax)
    p_e = ee / jnp.sum(ee, axis=-1, keepdims=True)
    w1 = jnp.max(p_e, axis=-1, keepdims=True)
    i1 = jnp.min(jnp.where(in_grp & (p_e == w1), lane, 2 * LANES), axis=-1, keepdims=True)
    rest = jnp.where(in_grp & (lane != i1), p_e, -1.0)
    w2 = jnp.max(rest, axis=-1, keepdims=True)
    i2 = jnp.min(jnp.where(rest == w2, lane, 2 * LANES), axis=-1, keepdims=True)
    tot = w1 + w2
    gates = jnp.where(lane == i1, w1 / tot * p_top, 0.0) + jnp.where(lane == i2, w2 / tot * p_top, 0.0)
    return jnp.where(lane == GROUP_LANE, g_idx.astype(F32), gates)


def _post_kernel(*refs, has_pos):
    it = iter(refs)
    x_ref = next(it)
    pos_ref = next(it) if has_pos else None
    og_ref, yg_ref, gg_ref, gs_ref, ada_ref = (next(it) for _ in range(5))
    wg_ref, ws_ref, wo_ref, g1_ref, b1_ref, wr_ref, br_ref = (next(it) for _ in range(7))
    x1_ref, h2_ref, gates_ref = (next(it) for _ in range(3))

    x = x_ref[...]
    if has_pos:
        x = x + pos_ref[...]
    u_g = jnp.dot(og_ref[...], wg_ref[...], preferred_element_type=F32)
    u_s = jnp.dot(yg_ref[...], ws_ref[...], preferred_element_type=F32)
    m = _sigmoid(gg_ref[...].astype(F32)) * u_g + _sigmoid(gs_ref[...].astype(F32)) * u_s
    mix = jnp.dot(m.astype(BF16), wo_ref[...], preferred_element_type=F32)
    gate1 = ada_ref[:, 2 * D_MODEL:3 * D_MODEL]
    shift2 = ada_ref[:, 3 * D_MODEL:4 * D_MODEL]
    scale2 = ada_ref[:, 4 * D_MODEL:5 * D_MODEL]
    x1 = _ln(ALPHA * x + gate1 * mix) * g1_ref[...] + b1_ref[...]
    x1_ref[...] = x1
    h2 = _ln(x1) * (1.0 + scale2) + shift2
    h2b = h2.astype(BF16)
    h2_ref[...] = h2b
    logits = jnp.dot(h2b, wr_ref[...], preferred_element_type=F32) + br_ref[...]
    gates_ref[...] = _route(logits)


def _post(x2d, pos, og, yg, proj, ada3, w_gdn_out, w_ssd_out, w_o, ln_g, ln_b, w_router, b_router,
          seq_len, ada_row0):
    n_tok = x2d.shape[0]
    t = POST_BLOCK
    blocks_per_seq = seq_len // t
    has_pos = pos is not None

    def ada_map(i):
        return (ada_row0 + (i // blocks_per_seq if has_pos else 0), 0, 0)

    const = lambda i: (0, 0)
    in_specs = [pl.BlockSpec((t, D_MODEL), lambda i: (i, 0))]
    args = [x2d]
    if has_pos:
        in_specs.append(pl.BlockSpec((t, D_MODEL), lambda i: (i % blocks_per_seq, 0)))
        args.append(pos)
    in_specs += [
        pl.BlockSpec((t, GDN_HEADS * GDN_DV), lambda i: (i, 0)),
        pl.BlockSpec((t, SSD_INNER), lambda i: (i, 0)),
        pl.BlockSpec((t, D_MODEL), lambda i: (i, COL_GATE // D_MODEL)),
        pl.BlockSpec((t, D_MODEL), lambda i: (i, COL_GATE // D_MODEL + 1)),
        pl.BlockSpec((None, 1, 6 * D_MODEL), ada_map),
        pl.BlockSpec((GDN_HEADS * GDN_DV, D_MODEL), const, pipeline_mode=pl.Buffered(1)),
        pl.BlockSpec((SSD_INNER, D_MODEL), const, pipeline_mode=pl.Buffered(1)),
        pl.BlockSpec((D_MODEL, D_MODEL), const, pipeline_mode=pl.Buffered(1)),
        pl.BlockSpec((1, D_MODEL), const),
        pl.BlockSpec((1, D_MODEL), const),
        pl.BlockSpec((D_MODEL, LANES), const),
        pl.BlockSpec((1, LANES), const),
    ]
    args += [og, yg, proj, proj, ada3, w_gdn_out, w_ssd_out, w_o, ln_g, ln_b, w_router, b_router]
    return pl.pallas_call(
        functools.partial(_post_kernel, has_pos=has_pos),
        grid=(n_tok // t,),
        in_specs=in_specs,
        out_specs=[
            pl.BlockSpec((t, D_MODEL), lambda i: (i, 0)),
            pl.BlockSpec((t, D_MODEL), lambda i: (i, 0)),
            pl.BlockSpec((t, LANES), lambda i: (i, 0)),
        ],
        out_shape=[
            jax.ShapeDtypeStruct((n_tok, D_MODEL), F32),
            jax.ShapeDtypeStruct((n_tok, D_MODEL), BF16),
            jax.ShapeDtypeStruct((n_tok, LANES), F32),
        ],
        compiler_params=pltpu.CompilerParams(dimension_semantics=("arbitrary",), vmem_limit_bytes=VMEM_LIMIT),
        name="post",
    )(*args)


def _moe_kernel(h_ref, gates_ref, x1_ref, ada_ref, wg_ref, wu_ref, wd_ref, g2_ref, b2_ref, out_ref,
                col_ref, row_ref, gx_ref, cnt_ref):
    g = pl.program_id(1)
    t = MOE_WINDOW
    lane = lax.broadcasted_iota(jnp.int32, (t, LANES), 1)

    @pl.when(g == 0)
    def _():
        gates = gates_ref[...]
        grp = jnp.sum(jnp.where(lane == GROUP_LANE, gates, 0.0), axis=-1, keepdims=True)
        onehot = jnp.where((lane < N_GROUPS) & (grp == lane.astype(F32)), 1.0, 0.0)
        tri = (lax.broadcasted_iota(jnp.int32, (t, t), 0) >= lax.broadcasted_iota(jnp.int32, (t, t), 1))
        cum = jnp.dot(jnp.where(tri, 1.0, 0.0).astype(BF16), onehot.astype(BF16), preferred_element_type=F32)
        rank = jnp.sum(onehot * cum, axis=-1, keepdims=True) - 1.0
        info = jnp.where(lane == 0, grp, jnp.where(lane == 1, rank, 0.0))
        col_ref[...] = info
        row_ref[...] = info.T[0:SUBLANES, :]
        totals = cum[t - 1:t, :]
        for k in range(N_GROUPS):
            cnt_ref[k] = totals[0, k].astype(jnp.int32)
        hi = gates.astype(BF16).astype(F32)
        mid = (gates - hi).astype(BF16).astype(F32)
        low = gates - hi - mid
        packed = jnp.where(lane < N_EXPERTS, hi,
                           jnp.where(lane < 2 * N_EXPERTS, pltpu.roll(mid, N_EXPERTS, axis=1),
                                     jnp.where(lane < 3 * N_EXPERTS, pltpu.roll(low, 2 * N_EXPERTS, axis=1), 0.0)))
        gx_ref[...] = packed.astype(BF16)
        out_ref[...] = jnp.zeros(out_ref.shape, F32)

    n_g = cnt_ref[g]
    gf = g.astype(F32)
    def run_tile(base, rows):
        slot_r = (lax.broadcasted_iota(jnp.int32, (rows, t), 0) + base).astype(F32)
        pick = jnp.where((row_ref[1:2, :] == slot_r) & (row_ref[0:1, :] == gf), 1.0, 0.0).astype(BF16)
        hs = jnp.dot(pick, h_ref[...], preferred_element_type=F32).astype(BF16)
        gsx = jnp.dot(pick, gx_ref[...], preferred_element_type=F32)
        gs = gsx + pltpu.roll(gsx, LANES - N_EXPERTS, axis=1) + pltpu.roll(gsx, LANES - 2 * N_EXPERTS, axis=1)
        lane_t = lax.broadcasted_iota(jnp.int32, (rows, LANES), 1)
        y = jnp.zeros((rows, D_MODEL), F32)
        for e in range(EXPERTS_PER_GROUP):
            a = jnp.dot(hs, wg_ref[e], preferred_element_type=F32)
            b = jnp.dot(hs, wu_ref[e], preferred_element_type=F32)
            gate_e = jnp.sum(jnp.where(lane_t == g * EXPERTS_PER_GROUP + e, gs, 0.0), axis=-1, keepdims=True)
            y = y + jnp.dot((_silu(a) * b * gate_e).astype(BF16), wd_ref[e], preferred_element_type=F32)
        slot_c = (lax.broadcasted_iota(jnp.int32, (t, rows), 1) + base).astype(F32)
        place = jnp.where((col_ref[:, 1:2] == slot_c) & (col_ref[:, 0:1] == gf), 1.0, 0.0).astype(BF16)
        out_ref[...] += jnp.dot(place, y.astype(BF16), preferred_element_type=F32)

    half = MOE_TILE // 2
    n_full = (n_g + half - 1) // MOE_TILE

    def full_tile(j, carry):
        run_tile(j * MOE_TILE, MOE_TILE)
        return carry

    lax.fori_loop(0, n_full, full_tile, 0)

    @pl.when(n_g > n_full * MOE_TILE)
    def _():
        run_tile(n_full * MOE_TILE, half)

    @pl.when(g == N_GROUPS - 1)
    def _():
        gate2 = ada_ref[:, 5 * D_MODEL:6 * D_MODEL]
        out_ref[...] = _ln(ALPHA * x1_ref[...] + gate2 * out_ref[...]) * g2_ref[...] + b2_ref[...]


def _moe(h2, gates, x1, ada3, w_gate, w_up, w_down, ln_g, ln_b, seq_len, ada_row0, per_seq_ada):
    n_tok = h2.shape[0]
    t = MOE_WINDOW
    blocks_per_seq = seq_len // t
    assert not per_seq_ada or seq_len % t == 0
    group_w = EXPERTS_PER_GROUP * D_EXPERT

    def ada_map(i, g):
        return (ada_row0 + (i // blocks_per_seq if per_seq_ada else 0), 0, 0)

    return pl.pallas_call(
        _moe_kernel,
        grid=(n_tok // t, N_GROUPS),
        in_specs=[
            pl.BlockSpec((t, D_MODEL), lambda i, g: (i, 0)),
            pl.BlockSpec((t, LANES), lambda i, g: (i, 0)),
            pl.BlockSpec((t, D_MODEL), lambda i, g: (i, 0), pipeline_mode=pl.Buffered(1)),
            pl.BlockSpec((None, 1, 6 * D_MODEL), ada_map),
            pl.BlockSpec((EXPERTS_PER_GROUP, D_MODEL, D_EXPERT), lambda i, g: (g, 0, 0)),
            pl.BlockSpec((EXPERTS_PER_GROUP, D_MODEL, D_EXPERT), lambda i, g: (g, 0, 0)),
            pl.BlockSpec((EXPERTS_PER_GROUP, D_EXPERT, D_MODEL), lambda i, g: (g, 0, 0)),
            pl.BlockSpec((1, D_MODEL), lambda i, g: (0, 0)),
            pl.BlockSpec((1, D_MODEL), lambda i, g: (0, 0)),
        ],
        out_specs=pl.BlockSpec((t, D_MODEL), lambda i, g: (i, 0)),
        out_shape=jax.ShapeDtypeStruct((n_tok, D_MODEL), F32),
        scratch_shapes=[
            pltpu.VMEM((t, LANES), F32),
            pltpu.VMEM((SUBLANES, t), F32),
            pltpu.VMEM((t, LANES), BF16),
            pltpu.SMEM((N_GROUPS,), jnp.int32),
        ],
        compiler_params=pltpu.CompilerParams(dimension_semantics=("arbitrary", "arbitrary"),
                                             vmem_limit_bytes=VMEM_LIMIT),
        name="moe",
    )(h2, gates, x1, ada3, w_gate, w_up, w_down, ln_g, ln_b)


def _grid_pos_embed(n_tokens, d):
    rows = n_tokens // GRID_W
    rr, cc = np.meshgrid(np.arange(rows, dtype=np.float32), np.arange(GRID_W, dtype=np.float32), indexing="ij")
    quarter = d // 4
    freqs = np.exp(-math.log(POS_BASE) * np.arange(quarter, dtype=np.float32) / quarter).astype(np.float32)
    ang_r = rr.reshape(-1, 1) * freqs
    ang_c = cc.reshape(-1, 1) * freqs
    table = np.concatenate([np.sin(ang_r), np.cos(ang_r), np.sin(ang_c), np.cos(ang_c)], axis=-1)
    return jnp.asarray(table, dtype=F32)


def _lane_rep(v):
    return jnp.broadcast_to(v[..., None], v.shape + (LANES,)).astype(F32)


def _stream(x3d, pos, ada3, ada_row0, s_gdn0, h_ssd0t, want_state, wts):
    bsz, seq_len, _ = x3d.shape
    x2d = x3d.reshape(bsz * seq_len, D_MODEL)
    proj, small_t = _in_proj(x2d, pos, ada3, wts["w_big"], wts["w_small_t"], seq_len, ada_row0)
    gdn_out = _gdn(proj, small_t, wts["gdn_conv_w"], wts["gdn_conv_b"], wts["gdn_params"],
                   wts["gdn_norm_w"], s_gdn0, bsz, seq_len, want_state)
    ssd_out = _ssd(proj, small_t, wts["ssd_conv_w"], wts["ssd_conv_b"], wts["ssd_a_log"], wts["ssd_dt_bias"],
                   wts["ssd_d"], wts["ssd_norm_w"], h_ssd0t, bsz, seq_len, want_state)
    x1, h2, gates = _post(x2d, pos, gdn_out[0], ssd_out[0], proj, ada3, wts["w_gdn_out"], wts["w_ssd_out"],
                          wts["w_o"], wts["ln1_g"], wts["ln1_b"], wts["w_router"], wts["b_router"],
                          seq_len, ada_row0)
    y = _moe(h2, gates, x1, ada3, wts["w_exp_gate"], wts["w_exp_up"], wts["w_exp_down"], wts["ln2_g"],
             wts["ln2_b"], seq_len, ada_row0, pos is not None)
    states = (gdn_out[1], ssd_out[1]) if want_state else None
    return y.reshape(bsz, seq_len, D_MODEL), states


def kernel(x_prompt, x_sample, state_gdn, state_ssd, c, c_ctx, w_ada, b_ada, w_in, gdn_conv_w, gdn_conv_b, gdn_a_log, gdn_dt_bias, gdn_norm_w, w_gdn_out, ssd_conv_w, ssd_conv_b, ssd_a_log, ssd_dt_bias, ssd_d, ssd_norm_w, w_ssd_out, w_o, ln1_g, ln1_b, w_router_group, b_router_group, w_router_expert, b_router_expert, w_exp_gate, w_exp_up, w_exp_down, ln2_g, ln2_b):
    assert w_in.shape[0] == DEPTH == 1
    l = 0
    bsz_c = x_prompt.shape[0]
    bsz_l, seq_l, _ = x_sample.shape

    cvec = jnp.zeros((SUBLANES, D_MODEL), F32).at[0].set(c_ctx).at[1:1 + bsz_l].set(c)
    ada3 = _ada(cvec, w_ada[l], b_ada[l]).reshape(SUBLANES, 1, 6 * D_MODEL)

    wi = w_in[l]
    o_zg = 3072
    o_beta = 4096
    o_a = 4112
    o_xbc = 4128
    o_zs = 7200
    o_dt = 9248
    o_gate = 9312
    w_big = jnp.concatenate([wi[:, 0:o_zg], wi[:, o_zg:o_beta], wi[:, o_xbc:o_zs], wi[:, o_zs:o_dt],
                             wi[:, o_gate:]], axis=1).astype(BF16)
    w_beta = wi[:, o_beta:o_a].T.reshape(2, GDN_HEADS, D_MODEL)
    w_a = wi[:, o_a:o_xbc].T.reshape(2, GDN_HEADS, D_MODEL)
    w_head = jnp.concatenate([w_beta, w_a, jnp.zeros((SUBLANES - 4, GDN_HEADS, D_MODEL), F32)], axis=0)
    w_head = jnp.transpose(w_head, (1, 0, 2)).reshape(ROW_DT, D_MODEL)
    w_small_t = jnp.concatenate([w_head, wi[:, o_dt:o_gate].T], axis=0).astype(BF16)
    assert w_small_t.shape == (SMALL_ROWS, D_MODEL)
    gdn_params = jnp.concatenate([gdn_a_log[l], gdn_dt_bias[l], jnp.zeros((SUBLANES - 4, GDN_HEADS), F32)], axis=0)
    gdn_params = _lane_rep(gdn_params.T)

    w_router = jnp.zeros((D_MODEL, LANES), F32)
    w_router = w_router.at[:, 0:N_EXPERTS].set(w_router_expert[l]).at[:, N_EXPERTS:N_EXPERTS + N_GROUPS].set(
        w_router_group[l])
    b_router = jnp.zeros((1, LANES), F32)
    b_router = b_router.at[0, 0:N_EXPERTS].set(b_router_expert[l]).at[0, N_EXPERTS:N_EXPERTS + N_GROUPS].set(
        b_router_group[l])

    wts = {
        "w_big": w_big, "w_small_t": w_small_t,
        "gdn_conv_w": gdn_conv_w[l], "gdn_conv_b": gdn_conv_b[l].reshape(1, -1),
        "gdn_params": gdn_params,
        "gdn_norm_w": gdn_norm_w[l].reshape(1, -1),
        "ssd_conv_w": ssd_conv_w[l], "ssd_conv_b": ssd_conv_b[l].reshape(1, -1),
        "ssd_a_log": _lane_rep(ssd_a_log[l]), "ssd_dt_bias": _lane_rep(ssd_dt_bias[l]),
        "ssd_d": jnp.repeat(ssd_d[l], SSD_P).reshape(1, -1), "ssd_norm_w": ssd_norm_w[l].reshape(1, -1),
        "w_gdn_out": w_gdn_out[l].astype(BF16), "w_ssd_out": w_ssd_out[l].astype(BF16),
        "w_o": w_o[l].astype(BF16),
        "ln1_g": ln1_g[l].reshape(1, -1), "ln1_b": ln1_b[l].reshape(1, -1),
        "w_router": w_router.astype(BF16), "b_router": b_router,
        "w_exp_gate": w_exp_gate[l].astype(BF16), "w_exp_up": w_exp_up[l].astype(BF16),
        "w_exp_down": w_exp_down[l].astype(BF16),
        "ln2_g": ln2_g[l].reshape(1, -1), "ln2_b": ln2_b[l].reshape(1, -1),
    }

    pos = _grid_pos_embed(seq_l, D_MODEL)
    h0 = state_ssd[:, l].reshape(bsz_l, 2, SSD_GROUPS, GROUP_W, SSD_N)

    y_ctx, (s_gdn, h_ssd) = _stream(x_prompt, None, ada3, 0, None, None, True, wts)
    y_lat, _ = _stream(x_sample, pos, ada3, 1, state_gdn, h0, False, wts)

    new_ssd = h_ssd.reshape(bsz_c, 1, 2, SSD_HEADS, SSD_P, SSD_N)
    return (y_ctx, y_lat, s_gdn, new_ssd)
```

```python
import functools
import math

import jax
import jax.numpy as jnp
import numpy as np
from jax import lax
from jax.experimental import pallas as pl
from jax.experimental.pallas import tpu as pltpu

F32 = jnp.float32
BF16 = jnp.bfloat16
HIGHEST = lax.Precision.HIGHEST

D_MODEL = 1024
GRID_W = 64
POS_BASE = 10000.0
CONV_K = 5
GDN_HEADS = 8
GDN_DK = 128
GDN_DV = 128
SSD_HEADS = 32
SSD_P = 64
SSD_INNER = SSD_HEADS * SSD_P
SSD_GROUPS = 4
SSD_N = 128
HEADS_PER_GROUP = SSD_HEADS // SSD_GROUPS
GROUP_W = HEADS_PER_GROUP * SSD_P
N_GROUPS = 4
EXPERTS_PER_GROUP = 8
N_EXPERTS = N_GROUPS * EXPERTS_PER_GROUP
D_EXPERT = 256
EPS = 1e-6
LOG2E = math.log2(math.e)
DEPTH = 1
ALPHA = (2.0 * DEPTH) ** 0.25

LANES = 128
SUBLANES = 8
CHUNK = 128
CONV_PAD = SUBLANES
VMEM_LIMIT = 56 * 1024 * 1024

COL_QKV = 0
COL_ZG = 3072
COL_XBC = 4096
COL_ZS = 7168
COL_GATE = 9216
PROJ_COLS = 11264
PROJ_TN = 1024
GDN_ROW_BETA = 0
GDN_ROW_A = 2
ROW_DT = GDN_HEADS * SUBLANES
SMALL_ROWS = 128

TOK_BLOCK = 2048
POST_BLOCK = 512
MOE_WINDOW = 1024
MOE_TILE = 288
GROUP_LANE = N_EXPERTS
SSD_CHUNKS_PER_STEP = 8


def _bdot(a, b):
    return jnp.dot(a.astype(BF16), b.astype(BF16), preferred_element_type=F32)


def _sigmoid(x):
    return 0.5 * jnp.tanh(0.5 * x) + 0.5


def _silu(x):
    return x * _sigmoid(x)


def _softplus(x):
    return jnp.maximum(x, 0.0) + jnp.log1p(jnp.exp(-jnp.abs(x)))


def _ln(x):
    mu = jnp.mean(x, axis=-1, keepdims=True)
    xc = x - mu
    var = jnp.mean(xc * xc, axis=-1, keepdims=True)
    return xc * lax.rsqrt(var + EPS)


def _lane_scan(x, reverse):
    lane = lax.broadcasted_iota(jnp.int32, x.shape, 1)
    s = 1
    while s < CHUNK:
        if reverse:
            shifted = pltpu.roll(x, CHUNK - s, axis=1)
            x = x + jnp.where(lane < CHUNK - s, shifted, 0.0)
        else:
            shifted = pltpu.roll(x, s, axis=1)
            x = x + jnp.where(lane >= s, shifted, 0.0)
        s *= 2
    return x


def _tri_masks(reverse):
    row = lax.broadcasted_iota(jnp.int32, (CHUNK, CHUNK), 0)
    col = lax.broadcasted_iota(jnp.int32, (CHUNK, CHUNK), 1)
    if reverse:
        return row <= col, row < col
    return row >= col, row > col


def _decay_matrix(acc_row, incl):
    acc_rb = jnp.broadcast_to(acc_row, (CHUNK, CHUNK))
    acc_col = acc_rb.T
    decay = jnp.exp(jnp.where(incl, acc_col - acc_rb, -jnp.inf))
    return decay, acc_col


def _conv_tap_sum(pad_ref, t0, lo, hi, w_ref, b_ref):
    acc = jnp.broadcast_to(b_ref[...], (CHUNK, hi - lo))
    for j in range(CONV_K):
        start = t0 + CONV_PAD - CONV_K // 2 + j
        acc = acc + pad_ref[start:start + CHUNK, lo:hi] * w_ref[j:j + 1, :]
    return acc


def _ada_kernel(c_ref, w_ref, b_ref, o_ref):
    s = _silu(c_ref[...])
    o_ref[...] = jnp.dot(s, w_ref[...], precision=HIGHEST, preferred_element_type=F32) + b_ref[...]


def _ada(cvec, w_ada, b_ada):
    n_out = w_ada.shape[1]
    tn = 1024
    return pl.pallas_call(
        _ada_kernel,
        grid=(n_out // tn,),
        in_specs=[
            pl.BlockSpec((SUBLANES, D_MODEL), lambda j: (0, 0)),
            pl.BlockSpec((D_MODEL, tn), lambda j: (0, j)),
            pl.BlockSpec((1, tn), lambda j: (0, j)),
        ],
        out_specs=pl.BlockSpec((SUBLANES, tn), lambda j: (0, j)),
        out_shape=jax.ShapeDtypeStruct((SUBLANES, n_out), F32),
        compiler_params=pltpu.CompilerParams(dimension_semantics=("arbitrary",), vmem_limit_bytes=VMEM_LIMIT),
        name="ada",
    )(cvec, w_ada, b_ada.reshape(1, n_out))


def _inproj_kernel(*refs, has_pos):
    if has_pos:
        x_ref, pos_ref, ada_ref, w_ref, wst_ref, proj_ref, small_ref, h_ref = refs
    else:
        x_ref, ada_ref, w_ref, wst_ref, proj_ref, small_ref, h_ref = refs
        pos_ref = None

    @pl.when(pl.program_id(1) == 0)
    def _():
        x = x_ref[...]
        if pos_ref is not None:
            x = x + pos_ref[...]
        shift = ada_ref[:, 0:D_MODEL]
        scale = ada_ref[:, D_MODEL:2 * D_MODEL]
        h = (_ln(x) * (1.0 + scale) + shift).astype(BF16)
        h_ref[...] = h
        small_ref[...] = lax.dot_general(wst_ref[...], h, (((1,), (1,)), ((), ())), preferred_element_type=F32)

    proj_ref[...] = jnp.dot(h_ref[...], w_ref[...], preferred_element_type=F32).astype(BF16)


def _in_proj(x2d, pos, ada3, w_big, w_small_t, seq_len, ada_row0):
    n_tok = x2d.shape[0]
    t = TOK_BLOCK
    blocks_per_seq = seq_len // t
    has_pos = pos is not None

    def ada_map(i, j):
        return (ada_row0 + (i // blocks_per_seq if has_pos else 0), 0, 0)

    in_specs = [pl.BlockSpec((t, D_MODEL), lambda i, j: (i, 0))]
    args = [x2d]
    if has_pos:
        pos_mode = dict(pipeline_mode=pl.Buffered(1)) if blocks_per_seq == 1 else {}
        in_specs.append(pl.BlockSpec((t, D_MODEL), lambda i, j: (i % blocks_per_seq, 0), **pos_mode))
        args.append(pos)
    in_specs += [
        pl.BlockSpec((None, 1, 6 * D_MODEL), ada_map),
        pl.BlockSpec((D_MODEL, PROJ_TN), lambda i, j: (0, j)),
        pl.BlockSpec((SMALL_ROWS, D_MODEL), lambda i, j: (0, 0)),
    ]
    args += [ada3, w_big, w_small_t]
    return pl.pallas_call(
        functools.partial(_inproj_kernel, has_pos=has_pos),
        grid=(n_tok // t, PROJ_COLS // PROJ_TN),
        in_specs=in_specs,
        out_specs=[
            pl.BlockSpec((t, PROJ_TN), lambda i, j: (i, j)),
            pl.BlockSpec((SMALL_ROWS, t), lambda i, j: (0, i)),
        ],
        out_shape=[
            jax.ShapeDtypeStruct((n_tok, PROJ_COLS), BF16),
            jax.ShapeDtypeStruct((SMALL_ROWS, n_tok), F32),
        ],
        scratch_shapes=[pltpu.VMEM((t, D_MODEL), BF16)],
        compiler_params=pltpu.CompilerParams(dimension_semantics=("arbitrary", "arbitrary"),
                                             vmem_limit_bytes=VMEM_LIMIT),
        name="in_proj",
    )(*args)


INV_BASE = 8
GDN_GROUP = 16


def _take_blocks(m, size, odd):
    parts = [m[k * size:(k + 1) * size] for k in range(CHUNK // size) if (k % 2 == 1) == odd]
    return parts[0] if len(parts) == 1 else jnp.concatenate(parts, axis=0)


def _interleave_blocks(even_rows, odd_rows, size):
    parts = []
    for k in range(CHUNK // (2 * size)):
        parts.append(even_rows[k * size:(k + 1) * size])
        parts.append(odd_rows[k * size:(k + 1) * size])
    return jnp.concatenate(parts, axis=0)


def _unit_tri_inverses(nmats, uppers):
    row = lax.broadcasted_iota(jnp.int32, (CHUNK, CHUNK), 0)
    col = lax.broadcasted_iota(jnp.int32, (CHUNK, CHUNK), 1)

    def same_block(size):
        shift = int(math.log2(size))
        return (row >> shift) == (col >> shift)

    eye = (row == col).astype(F32)
    base = same_block(INV_BASE)
    nds = [jnp.where(base, n, 0.0) for n in nmats]
    xs = [eye - nd for nd in nds]
    pws = [_bdot(nd, nd) for nd in nds]
    size = 2
    while True:
        xs = [x + _bdot(x, pw) for x, pw in zip(xs, pws)]
        size *= 2
        if size >= INV_BASE:
            break
        pws = [_bdot(pw, pw) for pw in pws]
    size = INV_BASE
    while size < CHUNK:
        coupling = same_block(2 * size) & jnp.logical_not(same_block(size))
        offs = [_take_blocks(jnp.where(coupling, n, 0.0), size, odd=not up) for n, up in zip(nmats, uppers)]
        tmps = [_bdot(off, x) for off, x in zip(offs, xs)]
        zeros = jnp.zeros((CHUNK // 2, CHUNK), F32)
        fulls = [_interleave_blocks(t, zeros, size) if up else _interleave_blocks(zeros, t, size)
                 for t, up in zip(tmps, uppers)]
        moved = [_take_blocks(x, size, odd=not up) for x, up in zip(xs, uppers)]
        kept = [_take_blocks(x, size, odd=up) for x, up in zip(xs, uppers)]
        news = [m - _bdot(m, f) for m, f in zip(moved, fulls)]
        xs = [_interleave_blocks(nw, kp, size) if up else _interleave_blocks(kp, nw, size)
              for nw, kp, up in zip(news, kept, uppers)]
        size *= 2
    return xs


def _gdn_prepare(chunks, qn_ref, kn_ref, vn_ref, kt_ref, sc_ref, lhs_ref, sb_ref, ob_ref, eg_ref):
    grams = [_bdot(jnp.concatenate([kn_ref[g], qn_ref[g]], axis=0), kt_ref[g]) for g in chunks]
    units = []
    for g, gram in zip(chunks, grams):
        for d in range(2):
            reverse = d == 1
            incl, strict = _tri_masks(reverse)
            beta = sc_ref[g, d:d + 1, :]
            gc = sc_ref[g, 2 + d:3 + d, :]
            decay, gcol = _decay_matrix(gc, incl)
            beta_rb = jnp.broadcast_to(beta, (CHUNK, CHUNK))
            glast = gc[:, 0:1] if reverse else gc[:, CHUNK - 1:CHUNK]
            sc_ref[g, 4 + d:5 + d, :] = jnp.broadcast_to(jnp.exp(glast), (1, CHUNK))
            eg_ref[g, d] = jnp.exp(gcol)
            units.append(dict(
                g=g, d=d,
                nmat=jnp.where(strict, gram[0:CHUNK] * decay, 0.0) * beta_rb,
                att=gram[CHUNK:2 * CHUNK] * decay * beta_rb,
                kd_scale=beta * jnp.exp(glast - gc)))
    xinvs = _unit_tri_inverses([u["nmat"] for u in units], [u["d"] == 1 for u in units])
    uws = [_bdot(x, jnp.concatenate([vn_ref[u["g"]], kn_ref[u["g"]] * eg_ref[u["g"], u["d"]]], axis=1))
           for u, x in zip(units, xinvs)]
    mixeds = []
    for u, uw in zip(units, uws):
        kd = kt_ref[u["g"]] * jnp.broadcast_to(u["kd_scale"], (CHUNK, CHUNK))
        mixeds.append(_bdot(jnp.concatenate([kd, u["att"]], axis=0), uw))
    for u, mixed in zip(units, mixeds):
        g, d = u["g"], u["d"]
        lhs_ref[g, d, 0:CHUNK, :] = mixed[0:CHUNK, GDN_DV:].astype(BF16)
        lhs_ref[g, d, CHUNK:2 * CHUNK, :] = qn_ref[g].astype(BF16)
        lhs_ref[g, d, 2 * CHUNK:3 * CHUNK, :] = mixed[CHUNK:, GDN_DV:].astype(BF16)
        sb_ref[g, d] = mixed[0:CHUNK, 0:GDN_DV]
        ob_ref[g, d] = mixed[CHUNK:, 0:GDN_DV]


def _gdn_steps(steps, lhs_ref, sb_ref, ob_ref, eg_ref, sc_ref, s_ref, o_refs):
    states = [s_ref[s, d] for _, s, d in steps]
    rs = [jnp.dot(lhs_ref[g, d], st.astype(BF16), preferred_element_type=F32)
          for (g, _, d), st in zip(steps, states)]
    for (g, s, d), st, r in zip(steps, states, rs):
        s_ref[s, d] = st * sc_ref[g, 4 + d:5 + d, :] - r[0:CHUNK] + sb_ref[g, d]
        o_refs[d][g] = eg_ref[g, d] * r[CHUNK:2 * CHUNK] - r[2 * CHUNK:3 * CHUNK] + ob_ref[g, d]


def _gdn_kernel(*refs, n_seq, n_chunks, has_s0, want_state):
    it = iter(refs)
    q_ref, k_ref, v_ref, z_ref, sm_ref = (next(it) for _ in range(5))
    wq_ref, wk_ref, wv_ref, bq_ref, bk_ref, bv_ref = (next(it) for _ in range(6))
    par_ref, nw_ref = (next(it) for _ in range(2))
    s0_ref = next(it) if has_s0 else None
    og_ref = next(it)
    sout_ref = next(it) if want_state else None
    pad_ref, qn_ref, kn_ref, vn_ref, kt_ref, of_ref, ob_ref, sc_ref, s_ref = (next(it) for _ in range(9))
    lhs_ref, sb_ref, ou_ref, eg_ref, st_ref = (next(it) for _ in range(5))

    seq = n_chunks * CHUNK
    total = n_seq * n_chunks
    zero_rows = jnp.zeros((CONV_PAD, 3 * LANES), F32)
    for s in range(n_seq):
        pad_ref[s, 0:CONV_PAD, :] = zero_rows
        pad_ref[s, CONV_PAD + seq:2 * CONV_PAD + seq, :] = zero_rows
        for c in range(n_chunks):
            t0 = c * CHUNK
            r0 = s * seq + t0
            pad_ref[s, CONV_PAD + t0:CONV_PAD + t0 + CHUNK, 0:LANES] = q_ref[r0:r0 + CHUNK, :].astype(F32)
            pad_ref[s, CONV_PAD + t0:CONV_PAD + t0 + CHUNK, LANES:2 * LANES] = k_ref[r0:r0 + CHUNK, :].astype(F32)
            pad_ref[s, CONV_PAD + t0:CONV_PAD + t0 + CHUNK, 2 * LANES:3 * LANES] = v_ref[r0:r0 + CHUNK, :].astype(F32)

    a_neg = [-jnp.exp(par_ref[d:d + 1, :]) for d in range(2)]
    dt_bias = [par_ref[2 + d:3 + d, :] for d in range(2)]
    for s in range(n_seq):
        for c in range(n_chunks):
            t0 = c * CHUNK
            r0 = s * seq + t0
            g = s * n_chunks + c
            pad_s = pad_ref.at[s]
            qa = _silu(_conv_tap_sum(pad_s, t0, 0, LANES, wq_ref, bq_ref))
            ka = _silu(_conv_tap_sum(pad_s, t0, LANES, 2 * LANES, wk_ref, bk_ref))
            va = _silu(_conv_tap_sum(pad_s, t0, 2 * LANES, 3 * LANES, wv_ref, bv_ref))
            qn = qa * lax.rsqrt(jnp.sum(qa * qa, axis=-1, keepdims=True) + EPS) * (GDN_DK ** -0.5)
            kn = ka * lax.rsqrt(jnp.sum(ka * ka, axis=-1, keepdims=True) + EPS)
            qn_ref[g] = qn
            kn_ref[g] = kn
            vn_ref[g] = va
            kt_ref[g] = kn.T

    for g in range(total):
        r0 = g * CHUNK
        for d in range(2):
            st_ref[d, g:g + 1, :] = sm_ref[GDN_ROW_A + d:GDN_ROW_A + d + 1, r0:r0 + CHUNK]
            st_ref[2 + d, g:g + 1, :] = sm_ref[GDN_ROW_BETA + d:GDN_ROW_BETA + d + 1, r0:r0 + CHUNK]
    for d in range(2):
        gcum = _lane_scan(a_neg[d] * _softplus(st_ref[d] + dt_bias[d]), reverse=d == 1)
        beta = _sigmoid(st_ref[2 + d])
        for g in range(total):
            sc_ref[g, d:d + 1, :] = beta[g:g + 1, :]
            sc_ref[g, 2 + d:3 + d, :] = gcum[g:g + 1, :]

    def prepare(i, carry):
        _gdn_prepare([i * GDN_GROUP + j for j in range(GDN_GROUP)], qn_ref, kn_ref, vn_ref, kt_ref, sc_ref,
                     lhs_ref, sb_ref, ou_ref, eg_ref)
        return carry

    lax.fori_loop(0, total // GDN_GROUP, prepare, 0)

    if has_s0:
        s_ref[...] = s0_ref[...]
    else:
        s_ref[...] = jnp.zeros(s_ref.shape, F32)

    def advance(i, carry):
        steps = []
        for s in range(n_seq):
            steps.append((s * n_chunks + i, s, 0))
            steps.append((s * n_chunks + n_chunks - 1 - i, s, 1))
        _gdn_steps(steps, lhs_ref, sb_ref, ou_ref, eg_ref, sc_ref, s_ref, (of_ref, ob_ref))
        return carry

    lax.fori_loop(0, n_chunks, advance, 0)

    if want_state:
        sout_ref[...] = s_ref[...]
    for g in range(total):
        r0 = g * CHUNK
        o = of_ref[g] + ob_ref[g]
        o = o * lax.rsqrt(jnp.mean(o * o, axis=-1, keepdims=True) + EPS)
        og_ref[r0:r0 + CHUNK, :] = (o * nw_ref[...] * _silu(z_ref[r0:r0 + CHUNK, :].astype(F32))).astype(BF16)


def _gdn(proj, small_t, conv_w, conv_b, head_params, norm_w, s0, bsz, seq_len, want_state):
    n_chunks = seq_len // CHUNK
    n_tok = bsz * seq_len
    has_s0 = s0 is not None
    n_seq = max(1, GDN_GROUP // n_chunks)
    assert bsz % n_seq == 0 and (n_seq * n_chunks) % GDN_GROUP == 0
    total = n_seq * n_chunks
    rows = n_seq * seq_len
    col = lambda off: (lambda b, h: (b, off + h))
    cw = lambda off: (lambda b, h: (0, off + h))
    in_specs = [
        pl.BlockSpec((rows, LANES), col(COL_QKV // LANES)),
        pl.BlockSpec((rows, LANES), col(COL_QKV // LANES + GDN_HEADS)),
        pl.BlockSpec((rows, LANES), col(COL_QKV // LANES + 2 * GDN_HEADS)),
        pl.BlockSpec((rows, LANES), col(COL_ZG // LANES)),
        pl.BlockSpec((SUBLANES, rows), lambda b, h: (h, b)),
        pl.BlockSpec((CONV_K, LANES), cw(0)),
        pl.BlockSpec((CONV_K, LANES), cw(GDN_HEADS)),
        pl.BlockSpec((CONV_K, LANES), cw(2 * GDN_HEADS)),
        pl.BlockSpec((1, LANES), cw(0)),
        pl.BlockSpec((1, LANES), cw(GDN_HEADS)),
        pl.BlockSpec((1, LANES), cw(2 * GDN_HEADS)),
        pl.BlockSpec((None, SUBLANES, LANES), lambda b, h: (h, 0, 0)),
        pl.BlockSpec((1, LANES), lambda b, h: (0, 0)),
    ]
    args = [proj, proj, proj, proj, small_t, conv_w, conv_w, conv_w, conv_b, conv_b, conv_b,
            head_params, norm_w]
    state_spec = pl.BlockSpec((n_seq, None, 2, None, GDN_DK, GDN_DV), lambda b, h: (b, 0, 0, h, 0, 0))
    if has_s0:
        in_specs.append(state_spec)
        args.append(s0)
    out_specs = [pl.BlockSpec((rows, LANES), lambda b, h: (b, h))]
    out_shape = [jax.ShapeDtypeStruct((n_tok, GDN_HEADS * GDN_DV), BF16)]
    if want_state:
        out_specs.append(state_spec)
        out_shape.append(jax.ShapeDtypeStruct((bsz, 1, 2, GDN_HEADS, GDN_DK, GDN_DV), F32))
    chunked = pltpu.VMEM((total, CHUNK, LANES), F32)
    per_dir = pltpu.VMEM((total, 2, CHUNK, LANES), F32)
    return pl.pallas_call(
        functools.partial(_gdn_kernel, n_seq=n_seq, n_chunks=n_chunks, has_s0=has_s0, want_state=want_state),
        grid=(bsz // n_seq, GDN_HEADS),
        in_specs=in_specs,
        out_specs=out_specs,
        out_shape=out_shape,
        scratch_shapes=[
            pltpu.VMEM((n_seq, seq_len + 2 * CONV_PAD, 3 * LANES), F32),
            chunked, chunked, chunked, chunked, chunked, chunked,
            pltpu.VMEM((total, SUBLANES, CHUNK), F32),
            pltpu.VMEM((n_seq, 2, GDN_DK, GDN_DV), F32),
            pltpu.VMEM((total, 2, 3 * CHUNK, GDN_DK), BF16),
            per_dir, per_dir, per_dir,
            pltpu.VMEM((4, total, CHUNK), F32),
        ],
        compiler_params=pltpu.CompilerParams(dimension_semantics=("arbitrary", "arbitrary"),
                                             vmem_limit_bytes=VMEM_LIMIT),
        name="gdn",
    )(*args)


def _ssd_steps(steps, xs_ref, cs_ref, bt_ref, dt_ref, ac_ref, ht_ref, y_ref):
    lane = lax.broadcasted_iota(jnp.int32, (CHUNK, LANES), 1)
    low_half = lane < SSD_P
    shared = [(_bdot(cs_ref[c], bt_ref[c]), _bdot(cs_ref[c], ht_ref[s, d])) for c, s, d in steps]
    pairs_per_stage = 2
    for p0 in range(0, HEADS_PER_GROUP // 2, pairs_per_stage):
        units = []
        for (c, s, d), (cb, y_off) in zip(steps, shared):
            reverse = d == 1
            incl, _ = _tri_masks(reverse)
            bt = bt_ref[c]
            for p in range(p0, p0 + pairs_per_stage):
                halves = []
                for r in (2 * p, 2 * p + 1):
                    row = d * HEADS_PER_GROUP + r
                    dt = dt_ref[c, row:row + 1, :]
                    ac = ac_ref[c, row:row + 1, :]
                    acol = jnp.broadcast_to(ac, (CHUNK, CHUNK)).T
                    shifted = jnp.broadcast_to(ac - jnp.log2(dt), (CHUNK, CHUNK))
                    last = ac[:, 0:1] if reverse else ac[:, CHUNK - 1:CHUNK]
                    halves.append(dict(
                        m=(cb * jnp.exp2(jnp.where(incl, acol - shifted, -jnp.inf))).astype(BF16),
                        bts=(bt * jnp.broadcast_to(dt * jnp.exp2(last - ac), (SSD_N, CHUNK))).astype(BF16),
                        acol=acol, elast=jnp.exp2(last)))
                units.append(dict(c=c, s=s, d=d, p=p, halves=halves, y_off=y_off[:, p * LANES:(p + 1) * LANES]))
        for u in units:
            x = xs_ref[u["c"], :, u["p"] * LANES:(u["p"] + 1) * LANES]
            zero = jnp.zeros_like(x)
            h0, h1 = u["halves"]
            lhs = jnp.concatenate([jnp.concatenate([h0["m"], h1["m"]], axis=1),
                                   jnp.concatenate([h0["bts"], h1["bts"]], axis=1)], axis=0)
            rhs = jnp.concatenate([jnp.where(low_half, x, zero), jnp.where(low_half, zero, x)], axis=0)
            u["prod"] = jnp.dot(lhs, rhs, preferred_element_type=F32)
        for u in units:
            c, s, d, p = u["c"], u["s"], u["d"], u["p"]
            h0, h1 = u["halves"]
            eoff = jnp.exp2(jnp.where(low_half, h0["acol"], h1["acol"]))
            elast = jnp.where(low_half, h0["elast"], h1["elast"])
            cols = slice(p * LANES, (p + 1) * LANES)
            y_ref[c, :, cols] = y_ref[c, :, cols] + u["prod"][0:CHUNK] + eoff * u["y_off"]
            ht_ref[s, d, :, cols] = ht_ref[s, d, :, cols] * elast + u["prod"][CHUNK:CHUNK + SSD_N]


def _ssd_kernel(*refs, n_seq, n_chunks, has_h0, want_state):
    it = iter(refs)
    x_ref, b_ref, c_ref, z_ref, dtf_ref, dtb_ref = (next(it) for _ in range(6))
    wx_ref, wb_ref, wc_ref, bx_ref, bb_ref, bc_ref = (next(it) for _ in range(6))
    alog_ref, dtbias_ref, dvec_ref, nw_ref = (next(it) for _ in range(4))
    h0_ref = next(it) if has_h0 else None
    yg_ref = next(it)
    hout_ref = next(it) if want_state else None
    pad_ref, xs_ref, cs_ref, bt_ref, dt_ref, ac_ref, y_ref, ht_ref = (next(it) for _ in range(8))

    seq = n_chunks * CHUNK
    total = n_seq * n_chunks
    width = GROUP_W + 2 * SSD_N
    zero_rows = jnp.zeros((CONV_PAD, width), F32)
    for s in range(n_seq):
        pad_ref[s, 0:CONV_PAD, :] = zero_rows
        pad_ref[s, CONV_PAD + seq:2 * CONV_PAD + seq, :] = zero_rows
        for c in range(n_chunks):
            t0 = c * CHUNK
            r0 = s * seq + t0
            pad_ref[s, CONV_PAD + t0:CONV_PAD + t0 + CHUNK, 0:GROUP_W] = x_ref[r0:r0 + CHUNK, :].astype(F32)
            pad_ref[s, CONV_PAD + t0:CONV_PAD + t0 + CHUNK, GROUP_W:GROUP_W + SSD_N] = b_ref[r0:r0 + CHUNK, :].astype(F32)
            pad_ref[s, CONV_PAD + t0:CONV_PAD + t0 + CHUNK, GROUP_W + SSD_N:width] = c_ref[r0:r0 + CHUNK, :].astype(F32)

    raw_dt = (dtf_ref, dtb_ref)
    for s in range(n_seq):
        pad_s = pad_ref.at[s]
        for c in range(n_chunks):
            t0 = c * CHUNK
            g = s * n_chunks + c
            for lo in range(0, GROUP_W, LANES):
                cols = slice(lo, lo + LANES)
                xa = _silu(_conv_tap_sum(pad_s, t0, lo, lo + LANES, wx_ref.at[:, cols], bx_ref.at[:, cols]))
                xs_ref[g, :, cols] = xa.astype(BF16)
                y_ref[g, :, cols] = xa * dvec_ref[:, cols]
            ba = _silu(_conv_tap_sum(pad_s, t0, GROUP_W, GROUP_W + SSD_N, wb_ref, bb_ref))
            ca = _silu(_conv_tap_sum(pad_s, t0, GROUP_W + SSD_N, width, wc_ref, bc_ref))
            cs_ref[g] = ca
            bt_ref[g] = ba.T
            for d in range(2):
                dt_ref[g, d * HEADS_PER_GROUP:(d + 1) * HEADS_PER_GROUP, :] = raw_dt[d][:, g * CHUNK:(g + 1) * CHUNK]

    for d in range(2):
        lo, hi = d * HEADS_PER_GROUP, (d + 1) * HEADS_PER_GROUP
        dt = _softplus(dt_ref[:, lo:hi, :] + dtbias_ref[d][None])
        dt_ref[:, lo:hi, :] = dt
        scaled = (dt * (-jnp.exp(alog_ref[d]))[None]).reshape(total * HEADS_PER_GROUP, CHUNK)
        ac_ref[:, lo:hi, :] = (_lane_scan(scaled, reverse=d == 1) * LOG2E).reshape(total, HEADS_PER_GROUP, CHUNK)

    if has_h0:
        for s in range(n_seq):
            for d in range(2):
                for lo in range(0, GROUP_W, LANES):
                    ht_ref[s, d, :, lo:lo + LANES] = h0_ref[s, d, lo:lo + LANES, :].T
    else:
        ht_ref[...] = jnp.zeros(ht_ref.shape, F32)

    def body(i, carry):
        for s in range(n_seq):
            _ssd_steps([(s * n_chunks + i, s, 0), (s * n_chunks + n_chunks - 1 - i, s, 1)],
                       xs_ref, cs_ref, bt_ref, dt_ref, ac_ref, ht_ref, y_ref)
        return carry

    lax.fori_loop(0, n_chunks, body, 0)

    if want_state:
        for s in range(n_seq):
            for d in range(2):
                for lo in range(0, GROUP_W, LANES):
                    hout_ref[s, d, lo:lo + LANES, :] = ht_ref[s, d, :, lo:lo + LANES].T
    for g in range(total):
        r0 = g * CHUNK
        ssq = jnp.zeros((CHUNK, 1), F32)
        for lo in range(0, GROUP_W, LANES):
            cols = slice(lo, lo + LANES)
            y = y_ref[g, :, cols] * _silu(z_ref[r0:r0 + CHUNK, cols].astype(F32))
            y_ref[g, :, cols] = y
            ssq = ssq + jnp.sum(y * y, axis=-1, keepdims=True)
        inv = lax.rsqrt(ssq * (1.0 / GROUP_W) + EPS)
        for lo in range(0, GROUP_W, LANES):
            cols = slice(lo, lo + LANES)
            yg_ref[r0:r0 + CHUNK, cols] = (y_ref[g, :, cols] * inv * nw_ref[:, cols]).astype(BF16)


def _ssd(proj, small_t, conv_w, conv_b, a_log_rep, dt_bias_rep, d_vec, norm_w, h0t, bsz, seq_len, want_state):
    n_chunks = seq_len // CHUNK
    n_tok = bsz * seq_len
    has_h0 = h0t is not None
    n_seq = max(1, SSD_CHUNKS_PER_STEP // n_chunks)
    assert bsz % n_seq == 0
    total = n_seq * n_chunks
    rows_per_step = n_seq * seq_len
    xbc_w = COL_XBC // GROUP_W
    bc_l = (COL_XBC + SSD_INNER) // LANES
    in_specs = [
        pl.BlockSpec((rows_per_step, GROUP_W), lambda b, g: (b, xbc_w + g)),
        pl.BlockSpec((rows_per_step, SSD_N), lambda b, g: (b, bc_l + g)),
        pl.BlockSpec((rows_per_step, SSD_N), lambda b, g: (b, bc_l + SSD_GROUPS + g)),
        pl.BlockSpec((rows_per_step, GROUP_W), lambda b, g: (b, COL_ZS // GROUP_W + g)),
        pl.BlockSpec((HEADS_PER_GROUP, rows_per_step), lambda b, g: (ROW_DT // HEADS_PER_GROUP + g, b)),
        pl.BlockSpec((HEADS_PER_GROUP, rows_per_step),
                     lambda b, g: ((ROW_DT + SSD_HEADS) // HEADS_PER_GROUP + g, b)),
        pl.BlockSpec((CONV_K, GROUP_W), lambda b, g: (0, g)),
        pl.BlockSpec((CONV_K, SSD_N), lambda b, g: (0, SSD_INNER // SSD_N + g)),
        pl.BlockSpec((CONV_K, SSD_N), lambda b, g: (0, SSD_INNER // SSD_N + SSD_GROUPS + g)),
        pl.BlockSpec((1, GROUP_W), lambda b, g: (0, g)),
        pl.BlockSpec((1, SSD_N), lambda b, g: (0, SSD_INNER // SSD_N + g)),
        pl.BlockSpec((1, SSD_N), lambda b, g: (0, SSD_INNER // SSD_N + SSD_GROUPS + g)),
        pl.BlockSpec((2, HEADS_PER_GROUP, LANES), lambda b, g: (0, g, 0)),
        pl.BlockSpec((2, HEADS_PER_GROUP, LANES), lambda b, g: (0, g, 0)),
        pl.BlockSpec((1, GROUP_W), lambda b, g: (0, g)),
        pl.BlockSpec((1, GROUP_W), lambda b, g: (0, g)),
    ]
    args = [proj, proj, proj, proj, small_t, small_t, conv_w, conv_w, conv_w, conv_b, conv_b, conv_b,
            a_log_rep, dt_bias_rep, d_vec, norm_w]
    state_spec = pl.BlockSpec((n_seq, 2, None, GROUP_W, SSD_N), lambda b, g: (b, 0, g, 0, 0))
    if has_h0:
        in_specs.append(state_spec)
        args.append(h0t)
    out_specs = [pl.BlockSpec((rows_per_step, GROUP_W), lambda b, g: (b, g))]
    out_shape = [jax.ShapeDtypeStruct((n_tok, SSD_INNER), BF16)]
    if want_state:
        out_specs.append(state_spec)
        out_shape.append(jax.ShapeDtypeStruct((bsz, 2, SSD_GROUPS, GROUP_W, SSD_N), F32))
    rows = pltpu.VMEM((total, 2 * HEADS_PER_GROUP, CHUNK), F32)
    return pl.pallas_call(
        functools.partial(_ssd_kernel, n_seq=n_seq, n_chunks=n_chunks, has_h0=has_h0, want_state=want_state),
        grid=(bsz // n_seq, SSD_GROUPS),
        in_specs=in_specs,
        out_specs=out_specs,
        out_shape=out_shape,
        scratch_shapes=[
            pltpu.VMEM((n_seq, seq_len + 2 * CONV_PAD, GROUP_W + 2 * SSD_N), F32),
            pltpu.VMEM((total, CHUNK, GROUP_W), BF16),
            pltpu.VMEM((total, CHUNK, SSD_N), F32),
            pltpu.VMEM((total, SSD_N, CHUNK), F32),
            rows, rows,
            pltpu.VMEM((total, CHUNK, GROUP_W), F32),
            pltpu.VMEM((n_seq, 2, SSD_N, GROUP_W), F32),
        ],
        compiler_params=pltpu.CompilerParams(dimension_semantics=("arbitrary", "arbitrary"),
                                             vmem_limit_bytes=VMEM_LIMIT),
        name="ssd",
    )(*args)


def _route(logits):
    lane = lax.broadcasted_iota(jnp.int32, logits.shape, 1)
    neg = -jnp.inf
    is_grp = (lane >= N_EXPERTS) & (lane < N_EXPERTS + N_GROUPS)
    gl = jnp.where(is_grp, logits, neg)
    gmax = jnp.max(gl, axis=-1, keepdims=True)
    ge = jnp.exp(gl - gmax)
    p_grp = ge / jnp.sum(ge, axis=-1, keepdims=True)
    p_top = jnp.max(p_grp, axis=-1, keepdims=True)
    g_idx = jnp.min(jnp.where(is_grp & (p_grp == p_top), lane, 2 * LANES), axis=-1, keepdims=True) - N_EXPERTS
    in_grp = (lane >= g_idx * EXPERTS_PER_GROUP) & (lane < (g_idx + 1) * EXPERTS_PER_GROUP)
    el = jnp.where(in_grp, logits, neg)
    emax = jnp.max(el, axis=-1, keepdims=True)
    ee = jnp.exp(el - emax)
    p_e = ee / jnp.sum(ee, axis=-1, keepdims=True)
    w1 = jnp.max(p_e, axis=-1, keepdims=True)
    i1 = jnp.min(jnp.where(in_grp & (p_e == w1), lane, 2 * LANES), axis=-1, keepdims=True)
    rest = jnp.where(in_grp & (lane != i1), p_e, -1.0)
    w2 = jnp.max(rest, axis=-1, keepdims=True)
    i2 = jnp.min(jnp.where(rest == w2, lane, 2 * LANES), axis=-1, keepdims=True)
    tot = w1 + w2
    gates = jnp.where(lane == i1, w1 / tot * p_top, 0.0) + jnp.where(lane == i2, w2 / tot * p_top, 0.0)
    return jnp.where(lane == GROUP_LANE, g_idx.astype(F32), gates)


def _post_kernel(*refs, has_pos):
    it = iter(refs)
    x_ref = next(it)
    pos_ref = next(it) if has_pos else None
    og_ref, yg_ref, gg_ref, gs_ref, ada_ref = (next(it) for _ in range(5))
    wg_ref, ws_ref, wo_ref, g1_ref, b1_ref, wr_ref, br_ref = (next(it) for _ in range(7))
    x1_ref, h2_ref, gates_ref = (next(it) for _ in range(3))

    x = x_ref[...]
    if has_pos:
        x = x + pos_ref[...]
    u_g = jnp.dot(og_ref[...], wg_ref[...], preferred_element_type=F32)
    u_s = jnp.dot(yg_ref[...], ws_ref[...], preferred_element_type=F32)
    m = _sigmoid(gg_ref[...].astype(F32)) * u_g + _sigmoid(gs_ref[...].astype(F32)) * u_s
    mix = jnp.dot(m.astype(BF16), wo_ref[...], preferred_element_type=F32)
    gate1 = ada_ref[:, 2 * D_MODEL:3 * D_MODEL]
    shift2 = ada_ref[:, 3 * D_MODEL:4 * D_MODEL]
    scale2 = ada_ref[:, 4 * D_MODEL:5 * D_MODEL]
    x1 = _ln(ALPHA * x + gate1 * mix) * g1_ref[...] + b1_ref[...]
    x1_ref[...] = x1
    h2 = _ln(x1) * (1.0 + scale2) + shift2
    h2b = h2.astype(BF16)
    h2_ref[...] = h2b
    logits = jnp.dot(h2b, wr_ref[...], preferred_element_type=F32) + br_ref[...]
    gates_ref[...] = _route(logits)


def _post(x2d, pos, og, yg, proj, ada3, w_gdn_out, w_ssd_out, w_o, ln_g, ln_b, w_router, b_router,
          seq_len, ada_row0):
    n_tok = x2d.shape[0]
    t = POST_BLOCK
    blocks_per_seq = seq_len // t
    has_pos = pos is not None

    def ada_map(i):
        return (ada_row0 + (i // blocks_per_seq if has_pos else 0), 0, 0)

    const = lambda i: (0, 0)
    in_specs = [pl.BlockSpec((t, D_MODEL), lambda i: (i, 0))]
    args = [x2d]
    if has_pos:
        in_specs.append(pl.BlockSpec((t, D_MODEL), lambda i: (i % blocks_per_seq, 0)))
        args.append(pos)
    in_specs += [
        pl.BlockSpec((t, GDN_HEADS * GDN_DV), lambda i: (i, 0)),
        pl.BlockSpec((t, SSD_INNER), lambda i: (i, 0)),
        pl.BlockSpec((t, D_MODEL), lambda i: (i, COL_GATE // D_MODEL)),
        pl.BlockSpec((t, D_MODEL), lambda i: (i, COL_GATE // D_MODEL + 1)),
        pl.BlockSpec((None, 1, 6 * D_MODEL), ada_map),
        pl.BlockSpec((GDN_HEADS * GDN_DV, D_MODEL), const, pipeline_mode=pl.Buffered(1)),
        pl.BlockSpec((SSD_INNER, D_MODEL), const, pipeline_mode=pl.Buffered(1)),
        pl.BlockSpec((D_MODEL, D_MODEL), const, pipeline_mode=pl.Buffered(1)),
        pl.BlockSpec((1, D_MODEL), const),
        pl.BlockSpec((1, D_MODEL), const),
        pl.BlockSpec((D_MODEL, LANES), const),
        pl.BlockSpec((1, LANES), const),
    ]
    args += [og, yg, proj, proj, ada3, w_gdn_out, w_ssd_out, w_o, ln_g, ln_b, w_router, b_router]
    return pl.pallas_call(
        functools.partial(_post_kernel, has_pos=has_pos),
        grid=(n_tok // t,),
        in_specs=in_specs,
        out_specs=[
            pl.BlockSpec((t, D_MODEL), lambda i: (i, 0)),
            pl.BlockSpec((t, D_MODEL), lambda i: (i, 0)),
            pl.BlockSpec((t, LANES), lambda i: (i, 0)),
        ],
        out_shape=[
            jax.ShapeDtypeStruct((n_tok, D_MODEL), F32),
            jax.ShapeDtypeStruct((n_tok, D_MODEL), BF16),
            jax.ShapeDtypeStruct((n_tok, LANES), F32),
        ],
        compiler_params=pltpu.CompilerParams(dimension_semantics=("arbitrary",), vmem_limit_bytes=VMEM_LIMIT),
        name="post",
    )(*args)


def _moe_kernel(h_ref, gates_ref, x1_ref, ada_ref, wg_ref, wu_ref, wd_ref, g2_ref, b2_ref, out_ref,
                col_ref, row_ref, gx_ref, cnt_ref):
    g = pl.program_id(1)
    t = MOE_WINDOW
    lane = lax.broadcasted_iota(jnp.int32, (t, LANES), 1)

    @pl.when(g == 0)
    def _():
        gates = gates_ref[...]
        grp = jnp.sum(jnp.where(lane == GROUP_LANE, gates, 0.0), axis=-1, keepdims=True)
        onehot = jnp.where((lane < N_GROUPS) & (grp == lane.astype(F32)), 1.0, 0.0)
        tri = (lax.broadcasted_iota(jnp.int32, (t, t), 0) >= lax.broadcasted_iota(jnp.int32, (t, t), 1))
        cum = jnp.dot(jnp.where(tri, 1.0, 0.0).astype(BF16), onehot.astype(BF16), preferred_element_type=F32)
        rank = jnp.sum(onehot * cum, axis=-1, keepdims=True) - 1.0
        info = jnp.where(lane == 0, grp, jnp.where(lane == 1, rank, 0.0))
        col_ref[...] = info
        row_ref[...] = info.T[0:SUBLANES, :]
        totals = cum[t - 1:t, :]
        for k in range(N_GROUPS):
            cnt_ref[k] = totals[0, k].astype(jnp.int32)
        hi = gates.astype(BF16).astype(F32)
        mid = (gates - hi).astype(BF16).astype(F32)
        low = gates - hi - mid
        packed = jnp.where(lane < N_EXPERTS, hi,
                           jnp.where(lane < 2 * N_EXPERTS, pltpu.roll(mid, N_EXPERTS, axis=1),
                                     jnp.where(lane < 3 * N_EXPERTS, pltpu.roll(low, 2 * N_EXPERTS, axis=1), 0.0)))
        gx_ref[...] = packed.astype(BF16)
        out_ref[...] = jnp.zeros(out_ref.shape, F32)

    n_g = cnt_ref[g]
    gf = g.astype(F32)
    def run_tile(base, rows):
        slot_r = (lax.broadcasted_iota(jnp.int32, (rows, t), 0) + base).astype(F32)
        pick = jnp.where((row_ref[1:2, :] == slot_r) & (row_ref[0:1, :] == gf), 1.0, 0.0).astype(BF16)
        hs = jnp.dot(pick, h_ref[...], preferred_element_type=F32).astype(BF16)
        gsx = jnp.dot(pick, gx_ref[...], preferred_element_type=F32)
        gs = gsx + pltpu.roll(gsx, LANES - N_EXPERTS, axis=1) + pltpu.roll(gsx, LANES - 2 * N_EXPERTS, axis=1)
        lane_t = lax.broadcasted_iota(jnp.int32, (rows, LANES), 1)
        y = jnp.zeros((rows, D_MODEL), F32)
        for e in range(EXPERTS_PER_GROUP):
            a = jnp.dot(hs, wg_ref[e], preferred_element_type=F32)
            b = jnp.dot(hs, wu_ref[e], preferred_element_type=F32)
            gate_e = jnp.sum(jnp.where(lane_t == g * EXPERTS_PER_GROUP + e, gs, 0.0), axis=-1, keepdims=True)
            y = y + jnp.dot((_silu(a) * b * gate_e).astype(BF16), wd_ref[e], preferred_element_type=F32)
        slot_c = (lax.broadcasted_iota(jnp.int32, (t, rows), 1) + base).astype(F32)
        place = jnp.where((col_ref[:, 1:2] == slot_c) & (col_ref[:, 0:1] == gf), 1.0, 0.0).astype(BF16)
        out_ref[...] += jnp.dot(place, y.astype(BF16), preferred_element_type=F32)

    half = MOE_TILE // 2
    n_full = (n_g + half - 1) // MOE_TILE

    def full_tile(j, carry):
        run_tile(j * MOE_TILE, MOE_TILE)
        return carry

    lax.fori_loop(0, n_full, full_tile, 0)

    @pl.when(n_g > n_full * MOE_TILE)
    def _():
        run_tile(n_full * MOE_TILE, half)

    @pl.when(g == N_GROUPS - 1)
    def _():
        gate2 = ada_ref[:, 5 * D_MODEL:6 * D_MODEL]
        out_ref[...] = _ln(ALPHA * x1_ref[...] + gate2 * out_ref[...]) * g2_ref[...] + b2_ref[...]


def _moe(h2, gates, x1, ada3, w_gate, w_up, w_down, ln_g, ln_b, seq_len, ada_row0, per_seq_ada):
    n_tok = h2.shape[0]
    t = MOE_WINDOW
    blocks_per_seq = seq_len // t
    assert not per_seq_ada or seq_len % t == 0
    group_w = EXPERTS_PER_GROUP * D_EXPERT

    def ada_map(i, g):
        return (ada_row0 + (i // blocks_per_seq if per_seq_ada else 0), 0, 0)

    return pl.pallas_call(
        _moe_kernel,
        grid=(n_tok // t, N_GROUPS),
        in_specs=[
            pl.BlockSpec((t, D_MODEL), lambda i, g: (i, 0)),
            pl.BlockSpec((t, LANES), lambda i, g: (i, 0)),
            pl.BlockSpec((t, D_MODEL), lambda i, g: (i, 0), pipeline_mode=pl.Buffered(1)),
            pl.BlockSpec((None, 1, 6 * D_MODEL), ada_map),
            pl.BlockSpec((EXPERTS_PER_GROUP, D_MODEL, D_EXPERT), lambda i, g: (g, 0, 0)),
            pl.BlockSpec((EXPERTS_PER_GROUP, D_MODEL, D_EXPERT), lambda i, g: (g, 0, 0)),
            pl.BlockSpec((EXPERTS_PER_GROUP, D_EXPERT, D_MODEL), lambda i, g: (g, 0, 0)),
            pl.BlockSpec((1, D_MODEL), lambda i, g: (0, 0)),
            pl.BlockSpec((1, D_MODEL), lambda i, g: (0, 0)),
        ],
        out_specs=pl.BlockSpec((t, D_MODEL), lambda i, g: (i, 0)),
        out_shape=jax.ShapeDtypeStruct((n_tok, D_MODEL), F32),
        scratch_shapes=[
            pltpu.VMEM((t, LANES), F32),
            pltpu.VMEM((SUBLANES, t), F32),
            pltpu.VMEM((t, LANES), BF16),
            pltpu.SMEM((N_GROUPS,), jnp.int32),
        ],
        compiler_params=pltpu.CompilerParams(dimension_semantics=("arbitrary", "arbitrary"),
                                             vmem_limit_bytes=VMEM_LIMIT),
        name="moe",
    )(h2, gates, x1, ada3, w_gate, w_up, w_down, ln_g, ln_b)


def _grid_pos_embed(n_tokens, d):
    rows = n_tokens // GRID_W
    rr, cc = np.meshgrid(np.arange(rows, dtype=np.float32), np.arange(GRID_W, dtype=np.float32), indexing="ij")
    quarter = d // 4
    freqs = np.exp(-math.log(POS_BASE) * np.arange(quarter, dtype=np.float32) / quarter).astype(np.float32)
    ang_r = rr.reshape(-1, 1) * freqs
    ang_c = cc.reshape(-1, 1) * freqs
    table = np.concatenate([np.sin(ang_r), np.cos(ang_r), np.sin(ang_c), np.cos(ang_c)], axis=-1)
    return jnp.asarray(table, dtype=F32)


def _lane_rep(v):
    return jnp.broadcast_to(v[..., None], v.shape + (LANES,)).astype(F32)


def _stream(x3d, pos, ada3, ada_row0, s_gdn0, h_ssd0t, want_state, wts):
    bsz, seq_len, _ = x3d.shape
    x2d = x3d.reshape(bsz * seq_len, D_MODEL)
    proj, small_t = _in_proj(x2d, pos, ada3, wts["w_big"], wts["w_small_t"], seq_len, ada_row0)
    gdn_out = _gdn(proj, small_t, wts["gdn_conv_w"], wts["gdn_conv_b"], wts["gdn_params"],
                   wts["gdn_norm_w"], s_gdn0, bsz, seq_len, want_state)
    ssd_out = _ssd(proj, small_t, wts["ssd_conv_w"], wts["ssd_conv_b"], wts["ssd_a_log"], wts["ssd_dt_bias"],
                   wts["ssd_d"], wts["ssd_norm_w"], h_ssd0t, bsz, seq_len, want_state)
    x1, h2, gates = _post(x2d, pos, gdn_out[0], ssd_out[0], proj, ada3, wts["w_gdn_out"], wts["w_ssd_out"],
                          wts["w_o"], wts["ln1_g"], wts["ln1_b"], wts["w_router"], wts["b_router"],
                          seq_len, ada_row0)
    y = _moe(h2, gates, x1, ada3, wts["w_exp_gate"], wts["w_exp_up"], wts["w_exp_down"], wts["ln2_g"],
             wts["ln2_b"], seq_len, ada_row0, pos is not None)
    states = (gdn_out[1], ssd_out[1]) if want_state else None
    return y.reshape(bsz, seq_len, D_MODEL), states


def kernel(x_prompt, x_sample, state_gdn, state_ssd, c, c_ctx, w_ada, b_ada, w_in, gdn_conv_w, gdn_conv_b, gdn_a_log, gdn_dt_bias, gdn_norm_w, w_gdn_out, ssd_conv_w, ssd_conv_b, ssd_a_log, ssd_dt_bias, ssd_d, ssd_norm_w, w_ssd_out, w_o, ln1_g, ln1_b, w_router_group, b_router_group, w_router_expert, b_router_expert, w_exp_gate, w_exp_up, w_exp_down, ln2_g, ln2_b):
    assert w_in.shape[0] == DEPTH == 1
    l = 0
    bsz_c = x_prompt.shape[0]
    bsz_l, seq_l, _ = x_sample.shape

    cvec = jnp.zeros((SUBLANES, D_MODEL), F32).at[0].set(c_ctx).at[1:1 + bsz_l].set(c)
    ada3 = _ada(cvec, w_ada[l], b_ada[l]).reshape(SUBLANES, 1, 6 * D_MODEL)

    wi = w_in[l]
    o_zg = 3072
    o_beta = 4096
    o_a = 4112
    o_xbc = 4128
    o_zs = 7200
    o_dt = 9248
    o_gate = 9312
    w_big = jnp.concatenate([wi[:, 0:o_zg], wi[:, o_zg:o_beta], wi[:, o_xbc:o_zs], wi[:, o_zs:o_dt],
                             wi[:, o_gate:]], axis=1).astype(BF16)
    w_beta = wi[:, o_beta:o_a].T.reshape(2, GDN_HEADS, D_MODEL)
    w_a = wi[:, o_a:o_xbc].T.reshape(2, GDN_HEADS, D_MODEL)
    w_head = jnp.concatenate([w_beta, w_a, jnp.zeros((SUBLANES - 4, GDN_HEADS, D_MODEL), F32)], axis=0)
    w_head = jnp.transpose(w_head, (1, 0, 2)).reshape(ROW_DT, D_MODEL)
    w_small_t = jnp.concatenate([w_head, wi[:, o_dt:o_gate].T], axis=0).astype(BF16)
    assert w_small_t.shape == (SMALL_ROWS, D_MODEL)
    gdn_params = jnp.concatenate([gdn_a_log[l], gdn_dt_bias[l], jnp.zeros((SUBLANES - 4, GDN_HEADS), F32)], axis=0)
    gdn_params = _lane_rep(gdn_params.T)

    w_router = jnp.zeros((D_MODEL, LANES), F32)
    w_router = w_router.at[:, 0:N_EXPERTS].set(w_router_expert[l]).at[:, N_EXPERTS:N_EXPERTS + N_GROUPS].set(
        w_router_group[l])
    b_router = jnp.zeros((1, LANES), F32)
    b_router = b_router.at[0, 0:N_EXPERTS].set(b_router_expert[l]).at[0, N_EXPERTS:N_EXPERTS + N_GROUPS].set(
        b_router_group[l])

    wts = {
        "w_big": w_big, "w_small_t": w_small_t,
        "gdn_conv_w": gdn_conv_w[l], "gdn_conv_b": gdn_conv_b[l].reshape(1, -1),
        "gdn_params": gdn_params,
        "gdn_norm_w": gdn_norm_w[l].reshape(1, -1),
        "ssd_conv_w": ssd_conv_w[l], "ssd_conv_b": ssd_conv_b[l].reshape(1, -1),
        "ssd_a_log": _lane_rep(ssd_a_log[l]), "ssd_dt_bias": _lane_rep(ssd_dt_bias[l]),
        "ssd_d": jnp.repeat(ssd_d[l], SSD_P).reshape(1, -1), "ssd_norm_w": ssd_norm_w[l].reshape(1, -1),
        "w_gdn_out": w_gdn_out[l].astype(BF16), "w_ssd_out": w_ssd_out[l].astype(BF16),
        "w_o": w_o[l].astype(BF16),
        "ln1_g": ln1_g[l].reshape(1, -1), "ln1_b": ln1_b[l].reshape(1, -1),
        "w_router": w_router.astype(BF16), "b_router": b_router,
        "w_exp_gate": w_exp_gate[l].astype(BF16), "w_exp_up": w_exp_up[l].astype(BF16),
        "w_exp_down": w_exp_down[l].astype(BF16),
        "ln2_g": ln2_g[l].reshape(1, -1), "ln2_b": ln2_b[l].reshape(1, -1),
    }

    pos = _grid_pos_embed(seq_l, D_MODEL)
    h0 = state_ssd[:, l].reshape(bsz_l, 2, SSD_GROUPS, GROUP_W, SSD_N)

    y_ctx, (s_gdn, h_ssd) = _stream(x_prompt, None, ada3, 0, None, None, True, wts)
    y_lat, _ = _stream(x_sample, pos, ada3, 1, state_gdn, h0, False, wts)

    new_ssd = h_ssd.reshape(bsz_c, 1, 2, SSD_HEADS, SSD_P, SSD_N)
    return (y_ctx, y_lat, s_gdn, new_ssd)
```

```python
import functools
import math

import jax
import jax.numpy as jnp
import numpy as np
from jax import lax
from jax.experimental import pallas as pl
from jax.experimental.pallas import tpu as pltpu

F32 = jnp.float32
BF16 = jnp.bfloat16
HIGHEST = lax.Precision.HIGHEST

D_MODEL = 1024
GRID_W = 64
POS_BASE = 10000.0
CONV_K = 5
GDN_HEADS = 8
GDN_DK = 128
GDN_DV = 128
SSD_HEADS = 32
SSD_P = 64
SSD_INNER = SSD_HEADS * SSD_P
SSD_GROUPS = 4
SSD_N = 128
HEADS_PER_GROUP = SSD_HEADS // SSD_GROUPS
GROUP_W = HEADS_PER_GROUP * SSD_P
N_GROUPS = 4
EXPERTS_PER_GROUP = 8
N_EXPERTS = N_GROUPS * EXPERTS_PER_GROUP
D_EXPERT = 256
EPS = 1e-6
LOG2E = math.log2(math.e)
DEPTH = 1
ALPHA = (2.0 * DEPTH) ** 0.25

LANES = 128
SUBLANES = 8
CHUNK = 128
CONV_PAD = SUBLANES
VMEM_LIMIT = 56 * 1024 * 1024

COL_QKV = 0
COL_ZG = 3072
COL_XBC = 4096
COL_ZS = 7168
COL_GATE = 9216
PROJ_COLS = 11264
PROJ_TN = 1024
GDN_ROW_BETA = 0
GDN_ROW_A = 2
ROW_DT = GDN_HEADS * SUBLANES
SMALL_ROWS = 128

TOK_BLOCK = 2048
POST_BLOCK = 512
MOE_WINDOW = 1024
MOE_TILE = 288
GROUP_LANE = N_EXPERTS
SSD_CHUNKS_PER_STEP = 8


def _bdot(a, b):
    return jnp.dot(a.astype(BF16), b.astype(BF16), preferred_element_type=F32)


def _sigmoid(x):
    return 0.5 * jnp.tanh(0.5 * x) + 0.5


def _silu(x):
    half = 0.5 * x
    return half + half * jnp.tanh(half)


def _softplus(x):
    return jnp.maximum(x, 0.0) + jnp.log1p(jnp.exp(-jnp.abs(x)))


def _ln(x):
    mu = jnp.mean(x, axis=-1, keepdims=True)
    xc = x - mu
    var = jnp.mean(xc * xc, axis=-1, keepdims=True)
    return xc * lax.rsqrt(var + EPS)


def _lane_scan(x, reverse):
    lane = lax.broadcasted_iota(jnp.int32, x.shape, 1)
    s = 1
    while s < CHUNK:
        if reverse:
            shifted = pltpu.roll(x, CHUNK - s, axis=1)
            x = x + jnp.where(lane < CHUNK - s, shifted, 0.0)
        else:
            shifted = pltpu.roll(x, s, axis=1)
            x = x + jnp.where(lane >= s, shifted, 0.0)
        s *= 2
    return x


def _tri_masks(reverse):
    row = lax.broadcasted_iota(jnp.int32, (CHUNK, CHUNK), 0)
    col = lax.broadcasted_iota(jnp.int32, (CHUNK, CHUNK), 1)
    if reverse:
        return row <= col, row < col
    return row >= col, row > col


def _decay_matrix(acc_row, incl):
    acc_rb = jnp.broadcast_to(acc_row, (CHUNK, CHUNK))
    acc_col = acc_rb.T
    decay = jnp.exp(jnp.where(incl, acc_col - acc_rb, -jnp.inf))
    return decay, acc_col


def _conv_tap_sum(pad_ref, t0, lo, hi, w_ref, b_ref):
    acc = jnp.broadcast_to(b_ref[...], (CHUNK, hi - lo))
    for j in range(CONV_K):
        start = t0 + CONV_PAD - CONV_K // 2 + j
        acc = acc + pad_ref[start:start + CHUNK, lo:hi] * w_ref[j:j + 1, :]
    return acc


def _ada_kernel(c_ref, w_ref, b_ref, o_ref):
    s = _silu(c_ref[...])
    o_ref[...] = jnp.dot(s, w_ref[...], precision=HIGHEST, preferred_element_type=F32) + b_ref[...]


def _ada(cvec, w_ada, b_ada):
    n_out = w_ada.shape[1]
    tn = 1024
    return pl.pallas_call(
        _ada_kernel,
        grid=(n_out // tn,),
        in_specs=[
            pl.BlockSpec((SUBLANES, D_MODEL), lambda j: (0, 0)),
            pl.BlockSpec((D_MODEL, tn), lambda j: (0, j)),
            pl.BlockSpec((1, tn), lambda j: (0, j)),
        ],
        out_specs=pl.BlockSpec((SUBLANES, tn), lambda j: (0, j)),
        out_shape=jax.ShapeDtypeStruct((SUBLANES, n_out), F32),
        compiler_params=pltpu.CompilerParams(dimension_semantics=("arbitrary",), vmem_limit_bytes=VMEM_LIMIT),
        name="ada",
    )(cvec, w_ada, b_ada.reshape(1, n_out))


def _inproj_kernel(*refs, has_pos):
    if has_pos:
        x_ref, pos_ref, ada_ref, w_ref, wst_ref, proj_ref, small_ref, h_ref = refs
    else:
        x_ref, ada_ref, w_ref, wst_ref, proj_ref, small_ref, h_ref = refs
        pos_ref = None

    @pl.when(pl.program_id(1) == 0)
    def _():
        x = x_ref[...]
        if pos_ref is not None:
            x = x + pos_ref[...]
        shift = ada_ref[:, 0:D_MODEL]
        scale = ada_ref[:, D_MODEL:2 * D_MODEL]
        h = (_ln(x) * (1.0 + scale) + shift).astype(BF16)
        h_ref[...] = h
        small_ref[...] = lax.dot_general(wst_ref[...], h, (((1,), (1,)), ((), ())), preferred_element_type=F32)

    proj_ref[...] = jnp.dot(h_ref[...], w_ref[...], preferred_element_type=F32).astype(BF16)


def _in_proj(x2d, pos, ada3, w_big, w_small_t, seq_len, ada_row0):
    n_tok = x2d.shape[0]
    t = TOK_BLOCK
    blocks_per_seq = seq_len // t
    has_pos = pos is not None

    def ada_map(i, j):
        return (ada_row0 + (i // blocks_per_seq if has_pos else 0), 0, 0)

    in_specs = [pl.BlockSpec((t, D_MODEL), lambda i, j: (i, 0))]
    args = [x2d]
    if has_pos:
        pos_mode = dict(pipeline_mode=pl.Buffered(1)) if blocks_per_seq == 1 else {}
        in_specs.append(pl.BlockSpec((t, D_MODEL), lambda i, j: (i % blocks_per_seq, 0), **pos_mode))
        args.append(pos)
    in_specs += [
        pl.BlockSpec((None, 1, 6 * D_MODEL), ada_map),
        pl.BlockSpec((D_MODEL, PROJ_TN), lambda i, j: (0, j)),
        pl.BlockSpec((SMALL_ROWS, D_MODEL), lambda i, j: (0, 0)),
    ]
    args += [ada3, w_big, w_small_t]
    return pl.pallas_call(
        functools.partial(_inproj_kernel, has_pos=has_pos),
        grid=(n_tok // t, PROJ_COLS // PROJ_TN),
        in_specs=in_specs,
        out_specs=[
            pl.BlockSpec((t, PROJ_TN), lambda i, j: (i, j)),
            pl.BlockSpec((SMALL_ROWS, t), lambda i, j: (0, i)),
        ],
        out_shape=[
            jax.ShapeDtypeStruct((n_tok, PROJ_COLS), BF16),
            jax.ShapeDtypeStruct((SMALL_ROWS, n_tok), F32),
        ],
        scratch_shapes=[pltpu.VMEM((t, D_MODEL), BF16)],
        compiler_params=pltpu.CompilerParams(dimension_semantics=("arbitrary", "arbitrary"),
                                             vmem_limit_bytes=VMEM_LIMIT),
        name="in_proj",
    )(*args)


INV_BASE = 8
GDN_GROUP = 16


def _take_blocks(m, size, odd):
    parts = [m[k * size:(k + 1) * size] for k in range(CHUNK // size) if (k % 2 == 1) == odd]
    return parts[0] if len(parts) == 1 else jnp.concatenate(parts, axis=0)


def _interleave_blocks(even_rows, odd_rows, size):
    parts = []
    for k in range(CHUNK // (2 * size)):
        parts.append(even_rows[k * size:(k + 1) * size])
        parts.append(odd_rows[k * size:(k + 1) * size])
    return jnp.concatenate(parts, axis=0)


def _unit_tri_inverses(nmats, uppers):
    row = lax.broadcasted_iota(jnp.int32, (CHUNK, CHUNK), 0)
    col = lax.broadcasted_iota(jnp.int32, (CHUNK, CHUNK), 1)

    def same_block(size):
        shift = int(math.log2(size))
        return (row >> shift) == (col >> shift)

    eye = (row == col).astype(F32)
    base = same_block(INV_BASE)
    nds = [jnp.where(base, n, 0.0) for n in nmats]
    xs = [eye - nd for nd in nds]
    pws = [_bdot(nd, nd) for nd in nds]
    size = 2
    while True:
        xs = [x + _bdot(x, pw) for x, pw in zip(xs, pws)]
        size *= 2
        if size >= INV_BASE:
            break
        pws = [_bdot(pw, pw) for pw in pws]
    size = INV_BASE
    while size < CHUNK:
        coupling = same_block(2 * size) & jnp.logical_not(same_block(size))
        offs = [_take_blocks(jnp.where(coupling, n, 0.0), size, odd=not up) for n, up in zip(nmats, uppers)]
        tmps = [_bdot(off, x) for off, x in zip(offs, xs)]
        zeros = jnp.zeros((CHUNK // 2, CHUNK), F32)
        fulls = [_interleave_blocks(t, zeros, size) if up else _interleave_blocks(zeros, t, size)
                 for t, up in zip(tmps, uppers)]
        moved = [_take_blocks(x, size, odd=not up) for x, up in zip(xs, uppers)]
        kept = [_take_blocks(x, size, odd=up) for x, up in zip(xs, uppers)]
        news = [m - _bdot(m, f) for m, f in zip(moved, fulls)]
        xs = [_interleave_blocks(nw, kp, size) if up else _interleave_blocks(kp, nw, size)
              for nw, kp, up in zip(news, kept, uppers)]
        size *= 2
    return xs


def _gdn_prepare(chunks, qn_ref, kn_ref, vn_ref, kt_ref, sc_ref, lhs_ref, sb_ref, ob_ref, eg_ref):
    grams = [_bdot(jnp.concatenate([kn_ref[g], qn_ref[g]], axis=0), kt_ref[g]) for g in chunks]
    units = []
    for g, gram in zip(chunks, grams):
        for d in range(2):
            reverse = d == 1
            incl, strict = _tri_masks(reverse)
            beta = sc_ref[g, d:d + 1, :]
            gc = sc_ref[g, 2 + d:3 + d, :]
            decay, gcol = _decay_matrix(gc, incl)
            beta_rb = jnp.broadcast_to(beta, (CHUNK, CHUNK))
            glast = gc[:, 0:1] if reverse else gc[:, CHUNK - 1:CHUNK]
            sc_ref[g, 4 + d:5 + d, :] = jnp.broadcast_to(jnp.exp(glast), (1, CHUNK))
            eg_ref[g, d] = jnp.exp(gcol)
            units.append(dict(
                g=g, d=d,
                nmat=jnp.where(strict, gram[0:CHUNK] * decay, 0.0) * beta_rb,
                att=gram[CHUNK:2 * CHUNK] * decay * beta_rb,
                kd_scale=beta * jnp.exp(glast - gc)))
    xinvs = _unit_tri_inverses([u["nmat"] for u in units], [u["d"] == 1 for u in units])
    uws = [_bdot(x, jnp.concatenate([vn_ref[u["g"]], kn_ref[u["g"]] * eg_ref[u["g"], u["d"]]], axis=1))
           for u, x in zip(units, xinvs)]
    mixeds = []
    for u, uw in zip(units, uws):
        kd = kt_ref[u["g"]] * jnp.broadcast_to(u["kd_scale"], (CHUNK, CHUNK))
        mixeds.append(_bdot(jnp.concatenate([kd, u["att"]], axis=0), uw))
    for u, mixed in zip(units, mixeds):
        g, d = u["g"], u["d"]
        lhs_ref[g, d, 0:CHUNK, :] = mixed[0:CHUNK, GDN_DV:].astype(BF16)
        lhs_ref[g, d, CHUNK:2 * CHUNK, :] = qn_ref[g].astype(BF16)
        lhs_ref[g, d, 2 * CHUNK:3 * CHUNK, :] = mixed[CHUNK:, GDN_DV:].astype(BF16)
        sb_ref[g, d] = mixed[0:CHUNK, 0:GDN_DV]
        ob_ref[g, d] = mixed[CHUNK:, 0:GDN_DV]


def _gdn_steps(steps, lhs_ref, sb_ref, ob_ref, eg_ref, sc_ref, s_ref, o_refs):
    states = [s_ref[s, d] for _, s, d in steps]
    rs = [jnp.dot(lhs_ref[g, d], st.astype(BF16), preferred_element_type=F32)
          for (g, _, d), st in zip(steps, states)]
    for (g, s, d), st, r in zip(steps, states, rs):
        s_ref[s, d] = st * sc_ref[g, 4 + d:5 + d, :] - r[0:CHUNK] + sb_ref[g, d]
        o_refs[d][g] = eg_ref[g, d] * r[CHUNK:2 * CHUNK] - r[2 * CHUNK:3 * CHUNK] + ob_ref[g, d]


def _gdn_kernel(*refs, n_seq, n_chunks, has_s0, want_state):
    it = iter(refs)
    q_ref, k_ref, v_ref, z_ref, sm_ref = (next(it) for _ in range(5))
    wq_ref, wk_ref, wv_ref, bq_ref, bk_ref, bv_ref = (next(it) for _ in range(6))
    par_ref, nw_ref = (next(it) for _ in range(2))
    s0_ref = next(it) if has_s0 else None
    og_ref = next(it)
    sout_ref = next(it) if want_state else None
    pad_ref, qn_ref, kn_ref, vn_ref, kt_ref, of_ref, ob_ref, sc_ref, s_ref = (next(it) for _ in range(9))
    lhs_ref, sb_ref, ou_ref, eg_ref, st_ref = (next(it) for _ in range(5))

    seq = n_chunks * CHUNK
    total = n_seq * n_chunks
    zero_rows = jnp.zeros((CONV_PAD, 3 * LANES), F32)
    for s in range(n_seq):
        pad_ref[s, 0:CONV_PAD, :] = zero_rows
        pad_ref[s, CONV_PAD + seq:2 * CONV_PAD + seq, :] = zero_rows
        for c in range(n_chunks):
            t0 = c * CHUNK
            r0 = s * seq + t0
            pad_ref[s, CONV_PAD + t0:CONV_PAD + t0 + CHUNK, 0:LANES] = q_ref[r0:r0 + CHUNK, :].astype(F32)
            pad_ref[s, CONV_PAD + t0:CONV_PAD + t0 + CHUNK, LANES:2 * LANES] = k_ref[r0:r0 + CHUNK, :].astype(F32)
            pad_ref[s, CONV_PAD + t0:CONV_PAD + t0 + CHUNK, 2 * LANES:3 * LANES] = v_ref[r0:r0 + CHUNK, :].astype(F32)

    a_neg = [-jnp.exp(par_ref[d:d + 1, :]) for d in range(2)]
    dt_bias = [par_ref[2 + d:3 + d, :] for d in range(2)]
    for s in range(n_seq):
        for c in range(n_chunks):
            t0 = c * CHUNK
            g = s * n_chunks + c
            pad_s = pad_ref.at[s]
            qa = _silu(_conv_tap_sum(pad_s, t0, 0, LANES, wq_ref, bq_ref))
            ka = _silu(_conv_tap_sum(pad_s, t0, LANES, 2 * LANES, wk_ref, bk_ref))
            va = _silu(_conv_tap_sum(pad_s, t0, 2 * LANES, 3 * LANES, wv_ref, bv_ref))
            qn = qa * lax.rsqrt(jnp.sum(qa * qa, axis=-1, keepdims=True) + EPS) * (GDN_DK ** -0.5)
            kn = ka * lax.rsqrt(jnp.sum(ka * ka, axis=-1, keepdims=True) + EPS)
            qn_ref[g] = qn
            kn_ref[g] = kn
            vn_ref[g] = va
            kt_ref[g] = kn.T

    for g in range(total):
        r0 = g * CHUNK
        for d in range(2):
            st_ref[d, g:g + 1, :] = sm_ref[GDN_ROW_A + d:GDN_ROW_A + d + 1, r0:r0 + CHUNK]
            st_ref[2 + d, g:g + 1, :] = sm_ref[GDN_ROW_BETA + d:GDN_ROW_BETA + d + 1, r0:r0 + CHUNK]
    for d in range(2):
        gcum = _lane_scan(a_neg[d] * _softplus(st_ref[d] + dt_bias[d]), reverse=d == 1)
        beta = _sigmoid(st_ref[2 + d])
        for g in range(total):
            sc_ref[g, d:d + 1, :] = beta[g:g + 1, :]
            sc_ref[g, 2 + d:3 + d, :] = gcum[g:g + 1, :]

    def prepare(i, carry):
        _gdn_prepare([i * GDN_GROUP + j for j in range(GDN_GROUP)], qn_ref, kn_ref, vn_ref, kt_ref, sc_ref,
                     lhs_ref, sb_ref, ou_ref, eg_ref)
        return carry

    lax.fori_loop(0, total // GDN_GROUP, prepare, 0)

    if has_s0:
        s_ref[...] = s0_ref[...]
    else:
        s_ref[...] = jnp.zeros(s_ref.shape, F32)

    def advance(i, carry):
        steps = []
        for s in range(n_seq):
            steps.append((s * n_chunks + i, s, 0))
            steps.append((s * n_chunks + n_chunks - 1 - i, s, 1))
        _gdn_steps(steps, lhs_ref, sb_ref, ou_ref, eg_ref, sc_ref, s_ref, (of_ref, ob_ref))
        return carry

    lax.fori_loop(0, n_chunks, advance, 0)

    if want_state:
        sout_ref[...] = s_ref[...]
    for g in range(total):
        r0 = g * CHUNK
        o = of_ref[g] + ob_ref[g]
        o = o * lax.rsqrt(jnp.mean(o * o, axis=-1, keepdims=True) + EPS)
        og_ref[r0:r0 + CHUNK, :] = (o * nw_ref[...] * _silu(z_ref[r0:r0 + CHUNK, :].astype(F32))).astype(BF16)


def _gdn(proj, small_t, conv_w, conv_b, head_params, norm_w, s0, bsz, seq_len, want_state):
    n_chunks = seq_len // CHUNK
    n_tok = bsz * seq_len
    has_s0 = s0 is not None
    n_seq = max(1, GDN_GROUP // n_chunks)
    assert bsz % n_seq == 0 and (n_seq * n_chunks) % GDN_GROUP == 0
    total = n_seq * n_chunks
    rows = n_seq * seq_len
    col = lambda off: (lambda b, h: (b, off + h))
    cw = lambda off: (lambda b, h: (0, off + h))
    in_specs = [
        pl.BlockSpec((rows, LANES), col(COL_QKV // LANES)),
        pl.BlockSpec((rows, LANES), col(COL_QKV // LANES + GDN_HEADS)),
        pl.BlockSpec((rows, LANES), col(COL_QKV // LANES + 2 * GDN_HEADS)),
        pl.BlockSpec((rows, LANES), col(COL_ZG // LANES)),
        pl.BlockSpec((SUBLANES, rows), lambda b, h: (h, b)),
        pl.BlockSpec((CONV_K, LANES), cw(0)),
        pl.BlockSpec((CONV_K, LANES), cw(GDN_HEADS)),
        pl.BlockSpec((CONV_K, LANES), cw(2 * GDN_HEADS)),
        pl.BlockSpec((1, LANES), cw(0)),
        pl.BlockSpec((1, LANES), cw(GDN_HEADS)),
        pl.BlockSpec((1, LANES), cw(2 * GDN_HEADS)),
        pl.BlockSpec((None, SUBLANES, LANES), lambda b, h: (h, 0, 0)),
        pl.BlockSpec((1, LANES), lambda b, h: (0, 0)),
    ]
    args = [proj, proj, proj, proj, small_t, conv_w, conv_w, conv_w, conv_b, conv_b, conv_b,
            head_params, norm_w]
    state_spec = pl.BlockSpec((n_seq, None, 2, None, GDN_DK, GDN_DV), lambda b, h: (b, 0, 0, h, 0, 0))
    if has_s0:
        in_specs.append(state_spec)
        args.append(s0)
    out_specs = [pl.BlockSpec((rows, LANES), lambda b, h: (b, h))]
    out_shape = [jax.ShapeDtypeStruct((n_tok, GDN_HEADS * GDN_DV), BF16)]
    if want_state:
        out_specs.append(state_spec)
        out_shape.append(jax.ShapeDtypeStruct((bsz, 1, 2, GDN_HEADS, GDN_DK, GDN_DV), F32))
    chunked = pltpu.VMEM((total, CHUNK, LANES), F32)
    per_dir = pltpu.VMEM((total, 2, CHUNK, LANES), F32)
    return pl.pallas_call(
        functools.partial(_gdn_kernel, n_seq=n_seq, n_chunks=n_chunks, has_s0=has_s0, want_state=want_state),
        grid=(bsz // n_seq, GDN_HEADS),
        in_specs=in_specs,
        out_specs=out_specs,
        out_shape=out_shape,
        scratch_shapes=[
            pltpu.VMEM((n_seq, seq_len + 2 * CONV_PAD, 3 * LANES), F32),
            chunked, chunked, chunked, chunked, chunked, chunked,
            pltpu.VMEM((total, SUBLANES, CHUNK), F32),
            pltpu.VMEM((n_seq, 2, GDN_DK, GDN_DV), F32),
            pltpu.VMEM((total, 2, 3 * CHUNK, GDN_DK), BF16),
            per_dir, per_dir, per_dir,
            pltpu.VMEM((4, total, CHUNK), F32),
        ],
        compiler_params=pltpu.CompilerParams(dimension_semantics=("arbitrary", "arbitrary"),
                                             vmem_limit_bytes=VMEM_LIMIT),
        name="gdn",
    )(*args)


def _ssd_steps(steps, xs_ref, cs_ref, bt_ref, dt_ref, ac_ref, ht_ref, y_ref):
    lane = lax.broadcasted_iota(jnp.int32, (CHUNK, LANES), 1)
    low_half = lane < SSD_P
    shared = [(_bdot(cs_ref[c], bt_ref[c]), _bdot(cs_ref[c], ht_ref[s, d])) for c, s, d in steps]
    pairs_per_stage = 2
    for p0 in range(0, HEADS_PER_GROUP // 2, pairs_per_stage):
        units = []
        for (c, s, d), (cb, y_off) in zip(steps, shared):
            reverse = d == 1
            incl, _ = _tri_masks(reverse)
            bt = bt_ref[c]
            for p in range(p0, p0 + pairs_per_stage):
                halves = []
                for r in (2 * p, 2 * p + 1):
                    row = d * HEADS_PER_GROUP + r
                    dt = dt_ref[c, row:row + 1, :]
                    ac = ac_ref[c, row:row + 1, :]
                    acol = jnp.broadcast_to(ac, (CHUNK, CHUNK)).T
                    shifted = jnp.broadcast_to(ac - jnp.log2(dt), (CHUNK, CHUNK))
                    last = ac[:, 0:1] if reverse else ac[:, CHUNK - 1:CHUNK]
                    halves.append(dict(
                        m=(cb * jnp.exp2(jnp.where(incl, acol - shifted, -jnp.inf))).astype(BF16),
                        bts=(bt * jnp.broadcast_to(dt * jnp.exp2(last - ac), (SSD_N, CHUNK))).astype(BF16),
                        acol=acol, elast=jnp.exp2(last)))
                units.append(dict(c=c, s=s, d=d, p=p, halves=halves, y_off=y_off[:, p * LANES:(p + 1) * LANES]))
        for u in units:
            x = xs_ref[u["c"], :, u["p"] * LANES:(u["p"] + 1) * LANES]
            zero = jnp.zeros_like(x)
            h0, h1 = u["halves"]
            lhs = jnp.concatenate([jnp.concatenate([h0["m"], h1["m"]], axis=1),
                                   jnp.concatenate([h0["bts"], h1["bts"]], axis=1)], axis=0)
            rhs = jnp.concatenate([jnp.where(low_half, x, zero), jnp.where(low_half, zero, x)], axis=0)
            u["prod"] = jnp.dot(lhs, rhs, preferred_element_type=F32)
        for u in units:
            c, s, d, p = u["c"], u["s"], u["d"], u["p"]
            h0, h1 = u["halves"]
            eoff = jnp.exp2(jnp.where(low_half, h0["acol"], h1["acol"]))
            elast = jnp.where(low_half, h0["elast"], h1["elast"])
            cols = slice(p * LANES, (p + 1) * LANES)
            y_ref[c, :, cols] = y_ref[c, :, cols] + u["prod"][0:CHUNK] + eoff * u["y_off"]
            ht_ref[s, d, :, cols] = ht_ref[s, d, :, cols] * elast + u["prod"][CHUNK:CHUNK + SSD_N]


def _ssd_kernel(*refs, n_seq, n_chunks, has_h0, want_state):
    it = iter(refs)
    x_ref, b_ref, c_ref, z_ref, dtf_ref, dtb_ref = (next(it) for _ in range(6))
    wx_ref, wb_ref, wc_ref, bx_ref, bb_ref, bc_ref = (next(it) for _ in range(6))
    alog_ref, dtbias_ref, dvec_ref, nw_ref = (next(it) for _ in range(4))
    h0_ref = next(it) if has_h0 else None
    yg_ref = next(it)
    hout_ref = next(it) if want_state else None
    pad_ref, xs_ref, cs_ref, bt_ref, dt_ref, ac_ref, y_ref, ht_ref = (next(it) for _ in range(8))

    seq = n_chunks * CHUNK
    total = n_seq * n_chunks
    width = GROUP_W + 2 * SSD_N
    zero_rows = jnp.zeros((CONV_PAD, width), F32)
    for s in range(n_seq):
        pad_ref[s, 0:CONV_PAD, :] = zero_rows
        pad_ref[s, CONV_PAD + seq:2 * CONV_PAD + seq, :] = zero_rows
        for c in range(n_chunks):
            t0 = c * CHUNK
            r0 = s * seq + t0
            pad_ref[s, CONV_PAD + t0:CONV_PAD + t0 + CHUNK, 0:GROUP_W] = x_ref[r0:r0 + CHUNK, :].astype(F32)
            pad_ref[s, CONV_PAD + t0:CONV_PAD + t0 + CHUNK, GROUP_W:GROUP_W + SSD_N] = b_ref[r0:r0 + CHUNK, :].astype(F32)
            pad_ref[s, CONV_PAD + t0:CONV_PAD + t0 + CHUNK, GROUP_W + SSD_N:width] = c_ref[r0:r0 + CHUNK, :].astype(F32)

    raw_dt = (dtf_ref, dtb_ref)
    for s in range(n_seq):
        pad_s = pad_ref.at[s]
        for c in range(n_chunks):
            t0 = c * CHUNK
            g = s * n_chunks + c
            for lo in range(0, GROUP_W, LANES):
                cols = slice(lo, lo + LANES)
                xa = _silu(_conv_tap_sum(pad_s, t0, lo, lo + LANES, wx_ref.at[:, cols], bx_ref.at[:, cols]))
                xs_ref[g, :, cols] = xa.astype(BF16)
                y_ref[g, :, cols] = xa * dvec_ref[:, cols]
            ba = _silu(_conv_tap_sum(pad_s, t0, GROUP_W, GROUP_W + SSD_N, wb_ref, bb_ref))
            ca = _silu(_conv_tap_sum(pad_s, t0, GROUP_W + SSD_N, width, wc_ref, bc_ref))
            cs_ref[g] = ca
            bt_ref[g] = ba.T
            for d in range(2):
                dt_ref[g, d * HEADS_PER_GROUP:(d + 1) * HEADS_PER_GROUP, :] = raw_dt[d][:, g * CHUNK:(g + 1) * CHUNK]

    for d in range(2):
        lo, hi = d * HEADS_PER_GROUP, (d + 1) * HEADS_PER_GROUP
        dt = _softplus(dt_ref[:, lo:hi, :] + dtbias_ref[d][None])
        dt_ref[:, lo:hi, :] = dt
        scaled = (dt * (-jnp.exp(alog_ref[d]))[None]).reshape(total * HEADS_PER_GROUP, CHUNK)
        ac_ref[:, lo:hi, :] = (_lane_scan(scaled, reverse=d == 1) * LOG2E).reshape(total, HEADS_PER_GROUP, CHUNK)

    if has_h0:
        for s in range(n_seq):
            for d in range(2):
                for lo in range(0, GROUP_W, LANES):
                    ht_ref[s, d, :, lo:lo + LANES] = h0_ref[s, d, lo:lo + LANES, :].T
    else:
        ht_ref[...] = jnp.zeros(ht_ref.shape, F32)

    def body(i, carry):
        for s in range(n_seq):
            _ssd_steps([(s * n_chunks + i, s, 0), (s * n_chunks + n_chunks - 1 - i, s, 1)],
                       xs_ref, cs_ref, bt_ref, dt_ref, ac_ref, ht_ref, y_ref)
        return carry

    lax.fori_loop(0, n_chunks, body, 0)

    if want_state:
        for s in range(n_seq):
            for d in range(2):
                for lo in range(0, GROUP_W, LANES):
                    hout_ref[s, d, lo:lo + LANES, :] = ht_ref[s, d, :, lo:lo + LANES].T
    for g in range(total):
        r0 = g * CHUNK
        ssq = jnp.zeros((CHUNK, 1), F32)
        for lo in range(0, GROUP_W, LANES):
            cols = slice(lo, lo + LANES)
            y = y_ref[g, :, cols] * _silu(z_ref[r0:r0 + CHUNK, cols].astype(F32))
            y_ref[g, :, cols] = y
            ssq = ssq + jnp.sum(y * y, axis=-1, keepdims=True)
        inv = lax.rsqrt(ssq * (1.0 / GROUP_W) + EPS)
        for lo in range(0, GROUP_W, LANES):
            cols = slice(lo, lo + LANES)
            yg_ref[r0:r0 + CHUNK, cols] = (y_ref[g, :, cols] * inv * nw_ref[:, cols]).astype(BF16)


def _ssd(proj, small_t, conv_w, conv_b, a_log_rep, dt_bias_rep, d_vec, norm_w, h0t, bsz, seq_len, want_state):
    n_chunks = seq_len // CHUNK
    n_tok = bsz * seq_len
    has_h0 = h0t is not None
    n_seq = max(1, SSD_CHUNKS_PER_STEP // n_chunks)
    assert bsz % n_seq == 0
    total = n_seq * n_chunks
    rows_per_step = n_seq * seq_len
    xbc_w = COL_XBC // GROUP_W
    bc_l = (COL_XBC + SSD_INNER) // LANES
    in_specs = [
        pl.BlockSpec((rows_per_step, GROUP_W), lambda b, g: (b, xbc_w + g)),
        pl.BlockSpec((rows_per_step, SSD_N), lambda b, g: (b, bc_l + g)),
        pl.BlockSpec((rows_per_step, SSD_N), lambda b, g: (b, bc_l + SSD_GROUPS + g)),
        pl.BlockSpec((rows_per_step, GROUP_W), lambda b, g: (b, COL_ZS // GROUP_W + g)),
        pl.BlockSpec((HEADS_PER_GROUP, rows_per_step), lambda b, g: (ROW_DT // HEADS_PER_GROUP + g, b)),
        pl.BlockSpec((HEADS_PER_GROUP, rows_per_step),
                     lambda b, g: ((ROW_DT + SSD_HEADS) // HEADS_PER_GROUP + g, b)),
        pl.BlockSpec((CONV_K, GROUP_W), lambda b, g: (0, g)),
        pl.BlockSpec((CONV_K, SSD_N), lambda b, g: (0, SSD_INNER // SSD_N + g)),
        pl.BlockSpec((CONV_K, SSD_N), lambda b, g: (0, SSD_INNER // SSD_N + SSD_GROUPS + g)),
        pl.BlockSpec((1, GROUP_W), lambda b, g: (0, g)),
        pl.BlockSpec((1, SSD_N), lambda b, g: (0, SSD_INNER // SSD_N + g)),
        pl.BlockSpec((1, SSD_N), lambda b, g: (0, SSD_INNER // SSD_N + SSD_GROUPS + g)),
        pl.BlockSpec((2, HEADS_PER_GROUP, LANES), lambda b, g: (0, g, 0)),
        pl.BlockSpec((2, HEADS_PER_GROUP, LANES), lambda b, g: (0, g, 0)),
        pl.BlockSpec((1, GROUP_W), lambda b, g: (0, g)),
        pl.BlockSpec((1, GROUP_W), lambda b, g: (0, g)),
    ]
    args = [proj, proj, proj, proj, small_t, small_t, conv_w, conv_w, conv_w, conv_b, conv_b, conv_b,
            a_log_rep, dt_bias_rep, d_vec, norm_w]
    state_spec = pl.BlockSpec((n_seq, 2, None, GROUP_W, SSD_N), lambda b, g: (b, 0, g, 0, 0))
    if has_h0:
        in_specs.append(state_spec)
        args.append(h0t)
    out_specs = [pl.BlockSpec((rows_per_step, GROUP_W), lambda b, g: (b, g))]
    out_shape = [jax.ShapeDtypeStruct((n_tok, SSD_INNER), BF16)]
    if want_state:
        out_specs.append(state_spec)
        out_shape.append(jax.ShapeDtypeStruct((bsz, 2, SSD_GROUPS, GROUP_W, SSD_N), F32))
    rows = pltpu.VMEM((total, 2 * HEADS_PER_GROUP, CHUNK), F32)
    return pl.pallas_call(
        functools.partial(_ssd_kernel, n_seq=n_seq, n_chunks=n_chunks, has_h0=has_h0, want_state=want_state),
        grid=(bsz // n_seq, SSD_GROUPS),
        in_specs=in_specs,
        out_specs=out_specs,
        out_shape=out_shape,
        scratch_shapes=[
            pltpu.VMEM((n_seq, seq_len + 2 * CONV_PAD, GROUP_W + 2 * SSD_N), F32),
            pltpu.VMEM((total, CHUNK, GROUP_W), BF16),
            pltpu.VMEM((total, CHUNK, SSD_N), F32),
            pltpu.VMEM((total, SSD_N, CHUNK), F32),
            rows, rows,
            pltpu.VMEM((total, CHUNK, GROUP_W), F32),
            pltpu.VMEM((n_seq, 2, SSD_N, GROUP_W), F32),
        ],
        compiler_params=pltpu.CompilerParams(dimension_semantics=("arbitrary", "arbitrary"),
                                             vmem_limit_bytes=VMEM_LIMIT),
        name="ssd",
    )(*args)


def _route(logits):
    lane = lax.broadcasted_iota(jnp.int32, logits.shape, 1)
    neg = -jnp.inf
    is_grp = (lane >= N_EXPERTS) & (lane < N_EXPERTS + N_GROUPS)
    gl = jnp.where(is_grp, logits, neg)
    gmax = jnp.max(gl, axis=-1, keepdims=True)
    ge = jnp.exp(gl - gmax)
    p_grp = ge / jnp.sum(ge, axis=-1, keepdims=True)
    p_top = jnp.max(p_grp, axis=-1, keepdims=True)
    g_idx = jnp.min(jnp.where(is_grp & (p_grp == p_top), lane, 2 * LANES), axis=-1, keepdims=True) - N_EXPERTS
    in_grp = (lane >= g_idx * EXPERTS_PER_GROUP) & (lane < (g_idx + 1) * EXPERTS_PER_GROUP)
    el = jnp.where(in_grp, logits, neg)
    emax = jnp.max(el, axis=-1, keepdims=True)
    ee = jnp.exp(el - emax)
    p_e = ee / jnp.sum(ee, axis=-1, keepdims=True)
    w1 = jnp.max(p_e, axis=-1, keepdims=True)
    i1 = jnp.min(jnp.where(in_grp & (p_e == w1), lane, 2 * LANES), axis=-1, keepdims=True)
    rest = jnp.where(in_grp & (lane != i1), p_e, -1.0)
    w2 = jnp.max(rest, axis=-1, keepdims=True)
    i2 = jnp.min(jnp.where(rest == w2, lane, 2 * LANES), axis=-1, keepdims=True)
    tot = w1 + w2
    gates = jnp.where(lane == i1, w1 / tot * p_top, 0.0) + jnp.where(lane == i2, w2 / tot * p_top, 0.0)
    return jnp.where(lane == GROUP_LANE, g_idx.astype(F32), gates)


def _post_kernel(*refs, has_pos):
    it = iter(refs)
    x_ref = next(it)
    pos_ref = next(it) if has_pos else None
    og_ref, yg_ref, gg_ref, gs_ref, ada_ref = (next(it) for _ in range(5))
    wg_ref, ws_ref, wo_ref, g1_ref, b1_ref, wr_ref, br_ref = (next(it) for _ in range(7))
    x1_ref, h2_ref, gates_ref = (next(it) for _ in range(3))

    x = x_ref[...]
    if has_pos:
        x = x + pos_ref[...]
    u_g = jnp.dot(og_ref[...], wg_ref[...], preferred_element_type=F32)
    u_s = jnp.dot(yg_ref[...], ws_ref[...], preferred_element_type=F32)
    m = _sigmoid(gg_ref[...].astype(F32)) * u_g + _sigmoid(gs_ref[...].astype(F32)) * u_s
    mix = jnp.dot(m.astype(BF16), wo_ref[...], preferred_element_type=F32)
    gate1 = ada_ref[:, 2 * D_MODEL:3 * D_MODEL]
    shift2 = ada_ref[:, 3 * D_MODEL:4 * D_MODEL]
    scale2 = ada_ref[:, 4 * D_MODEL:5 * D_MODEL]
    x1 = _ln(ALPHA * x + gate1 * mix) * g1_ref[...] + b1_ref[...]
    x1_ref[...] = x1
    h2 = _ln(x1) * (1.0 + scale2) + shift2
    h2b = h2.astype(BF16)
    h2_ref[...] = h2b
    logits = jnp.dot(h2b, wr_ref[...], preferred_element_type=F32) + br_ref[...]
    gates_ref[...] = _route(logits)


def _post(x2d, pos, og, yg, proj, ada3, w_gdn_out, w_ssd_out, w_o, ln_g, ln_b, w_router, b_router,
          seq_len, ada_row0):
    n_tok = x2d.shape[0]
    t = POST_BLOCK
    blocks_per_seq = seq_len // t
    has_pos = pos is not None

    def ada_map(i):
        return (ada_row0 + (i // blocks_per_seq if has_pos else 0), 0, 0)

    const = lambda i: (0, 0)
    in_specs = [pl.BlockSpec((t, D_MODEL), lambda i: (i, 0))]
    args = [x2d]
    if has_pos:
        in_specs.append(pl.BlockSpec((t, D_MODEL), lambda i: (i % blocks_per_seq, 0)))
        args.append(pos)
    in_specs += [
        pl.BlockSpec((t, GDN_HEADS * GDN_DV), lambda i: (i, 0)),
        pl.BlockSpec((t, SSD_INNER), lambda i: (i, 0)),
        pl.BlockSpec((t, D_MODEL), lambda i: (i, COL_GATE // D_MODEL)),
        pl.BlockSpec((t, D_MODEL), lambda i: (i, COL_GATE // D_MODEL + 1)),
        pl.BlockSpec((None, 1, 6 * D_MODEL), ada_map),
        pl.BlockSpec((GDN_HEADS * GDN_DV, D_MODEL), const, pipeline_mode=pl.Buffered(1)),
        pl.BlockSpec((SSD_INNER, D_MODEL), const, pipeline_mode=pl.Buffered(1)),
        pl.BlockSpec((D_MODEL, D_MODEL), const, pipeline_mode=pl.Buffered(1)),
        pl.BlockSpec((1, D_MODEL), const),
        pl.BlockSpec((1, D_MODEL), const),
        pl.BlockSpec((D_MODEL, LANES), const),
        pl.BlockSpec((1, LANES), const),
    ]
    args += [og, yg, proj, proj, ada3, w_gdn_out, w_ssd_out, w_o, ln_g, ln_b, w_router, b_router]
    return pl.pallas_call(
        functools.partial(_post_kernel, has_pos=has_pos),
        grid=(n_tok // t,),
        in_specs=in_specs,
        out_specs=[
            pl.BlockSpec((t, D_MODEL), lambda i: (i, 0)),
            pl.BlockSpec((t, D_MODEL), lambda i: (i, 0)),
            pl.BlockSpec((t, LANES), lambda i: (i, 0)),
        ],
        out_shape=[
            jax.ShapeDtypeStruct((n_tok, D_MODEL), F32),
            jax.ShapeDtypeStruct((n_tok, D_MODEL), BF16),
            jax.ShapeDtypeStruct((n_tok, LANES), F32),
        ],
        compiler_params=pltpu.CompilerParams(dimension_semantics=("arbitrary",), vmem_limit_bytes=VMEM_LIMIT),
        name="post",
    )(*args)


def _moe_kernel(h_ref, gates_ref, x1_ref, ada_ref, wg_ref, wu_ref, wd_ref, g2_ref, b2_ref, out_ref,
                col_ref, row_ref, gx_ref, cnt_ref):
    g = pl.program_id(1)
    t = MOE_WINDOW
    lane = lax.broadcasted_iota(jnp.int32, (t, LANES), 1)

    @pl.when(g == 0)
    def _():
        gates = gates_ref[...]
        grp = jnp.sum(jnp.where(lane == GROUP_LANE, gates, 0.0), axis=-1, keepdims=True)
        onehot = jnp.where((lane < N_GROUPS) & (grp == lane.astype(F32)), 1.0, 0.0)
        tri = (lax.broadcasted_iota(jnp.int32, (t, t), 0) >= lax.broadcasted_iota(jnp.int32, (t, t), 1))
        cum = jnp.dot(jnp.where(tri, 1.0, 0.0).astype(BF16), onehot.astype(BF16), preferred_element_type=F32)
        rank = jnp.sum(onehot * cum, axis=-1, keepdims=True) - 1.0
        info = jnp.where(lane == 0, grp, jnp.where(lane == 1, rank, 0.0))
        col_ref[...] = info
        row_ref[...] = info.T[0:SUBLANES, :]
        totals = cum[t - 1:t, :]
        for k in range(N_GROUPS):
            cnt_ref[k] = totals[0, k].astype(jnp.int32)
        hi = gates.astype(BF16).astype(F32)
        mid = (gates - hi).astype(BF16).astype(F32)
        low = gates - hi - mid
        packed = jnp.where(lane < N_EXPERTS, hi,
                           jnp.where(lane < 2 * N_EXPERTS, pltpu.roll(mid, N_EXPERTS, axis=1),
                                     jnp.where(lane < 3 * N_EXPERTS, pltpu.roll(low, 2 * N_EXPERTS, axis=1), 0.0)))
        gx_ref[...] = packed.astype(BF16)
        out_ref[...] = jnp.zeros(out_ref.shape, F32)

    n_g = cnt_ref[g]
    gf = g.astype(F32)
    def run_tile(base, rows):
        slot_r = (lax.broadcasted_iota(jnp.int32, (rows, t), 0) + base).astype(F32)
        pick = jnp.where((row_ref[1:2, :] == slot_r) & (row_ref[0:1, :] == gf), 1.0, 0.0).astype(BF16)
        hs = jnp.dot(pick, h_ref[...], preferred_element_type=F32).astype(BF16)
        gsx = jnp.dot(pick, gx_ref[...], preferred_element_type=F32)
        gs = gsx + pltpu.roll(gsx, LANES - N_EXPERTS, axis=1) + pltpu.roll(gsx, LANES - 2 * N_EXPERTS, axis=1)
        lane_t = lax.broadcasted_iota(jnp.int32, (rows, LANES), 1)
        y = jnp.zeros((rows, D_MODEL), F32)
        for e in range(EXPERTS_PER_GROUP):
            a = jnp.dot(hs, wg_ref[e], preferred_element_type=F32)
            b = jnp.dot(hs, wu_ref[e], preferred_element_type=F32)
            gate_e = jnp.sum(jnp.where(lane_t == g * EXPERTS_PER_GROUP + e, gs, 0.0), axis=-1, keepdims=True)
            y = y + jnp.dot((_silu(a) * b * gate_e).astype(BF16), wd_ref[e], preferred_element_type=F32)
        slot_c = (lax.broadcasted_iota(jnp.int32, (t, rows), 1) + base).astype(F32)
        place = jnp.where((col_ref[:, 1:2] == slot_c) & (col_ref[:, 0:1] == gf), 1.0, 0.0).astype(BF16)
        out_ref[...] += jnp.dot(place, y.astype(BF16), preferred_element_type=F32)

    half = MOE_TILE // 2
    n_full = (n_g + half - 1) // MOE_TILE

    def full_tile(j, carry):
        run_tile(j * MOE_TILE, MOE_TILE)
        return carry

    lax.fori_loop(0, n_full, full_tile, 0)

    @pl.when(n_g > n_full * MOE_TILE)
    def _():
        run_tile(n_full * MOE_TILE, half)

    @pl.when(g == N_GROUPS - 1)
    def _():
        gate2 = ada_ref[:, 5 * D_MODEL:6 * D_MODEL]
        out_ref[...] = _ln(ALPHA * x1_ref[...] + gate2 * out_ref[...]) * g2_ref[...] + b2_ref[...]


def _moe(h2, gates, x1, ada3, w_gate, w_up, w_down, ln_g, ln_b, seq_len, ada_row0, per_seq_ada):
    n_tok = h2.shape[0]
    t = MOE_WINDOW
    blocks_per_seq = seq_len // t
    assert not per_seq_ada or seq_len % t == 0

    def ada_map(i, g):
        return (ada_row0 + (i // blocks_per_seq if per_seq_ada else 0), 0, 0)

    return pl.pallas_call(
        _moe_kernel,
        grid=(n_tok // t, N_GROUPS),
        in_specs=[
            pl.BlockSpec((t, D_MODEL), lambda i, g: (i, 0)),
            pl.BlockSpec((t, LANES), lambda i, g: (i, 0)),
            pl.BlockSpec((t, D_MODEL), lambda i, g: (i, 0), pipeline_mode=pl.Buffered(1)),
            pl.BlockSpec((None, 1, 6 * D_MODEL), ada_map),
            pl.BlockSpec((EXPERTS_PER_GROUP, D_MODEL, D_EXPERT), lambda i, g: (g, 0, 0)),
            pl.BlockSpec((EXPERTS_PER_GROUP, D_MODEL, D_EXPERT), lambda i, g: (g, 0, 0)),
            pl.BlockSpec((EXPERTS_PER_GROUP, D_EXPERT, D_MODEL), lambda i, g: (g, 0, 0)),
            pl.BlockSpec((1, D_MODEL), lambda i, g: (0, 0)),
            pl.BlockSpec((1, D_MODEL), lambda i, g: (0, 0)),
        ],
        out_specs=pl.BlockSpec((t, D_MODEL), lambda i, g: (i, 0)),
        out_shape=jax.ShapeDtypeStruct((n_tok, D_MODEL), F32),
        scratch_shapes=[
            pltpu.VMEM((t, LANES), F32),
            pltpu.VMEM((SUBLANES, t), F32),
            pltpu.VMEM((t, LANES), BF16),
            pltpu.SMEM((N_GROUPS,), jnp.int32),
        ],
        compiler_params=pltpu.CompilerParams(dimension_semantics=("arbitrary", "arbitrary"),
                                             vmem_limit_bytes=VMEM_LIMIT),
        name="moe",
    )(h2, gates, x1, ada3, w_gate, w_up, w_down, ln_g, ln_b)


def _grid_pos_embed(n_tokens, d):
    rows = n_tokens // GRID_W
    rr, cc = np.meshgrid(np.arange(rows, dtype=np.float32), np.arange(GRID_W, dtype=np.float32), indexing="ij")
    quarter = d // 4
    freqs = np.exp(-math.log(POS_BASE) * np.arange(quarter, dtype=np.float32) / quarter).astype(np.float32)
    ang_r = rr.reshape(-1, 1) * freqs
    ang_c = cc.reshape(-1, 1) * freqs
    table = np.concatenate([np.sin(ang_r), np.cos(ang_r), np.sin(ang_c), np.cos(ang_c)], axis=-1)
    return jnp.asarray(table, dtype=F32)


def _lane_rep(v):
    return jnp.broadcast_to(v[..., None], v.shape + (LANES,)).astype(F32)


def _stream(x3d, pos, ada3, ada_row0, s_gdn0, h_ssd0t, want_state, wts):
    bsz, seq_len, _ = x3d.shape
    x2d = x3d.reshape(bsz * seq_len, D_MODEL)
    proj, small_t = _in_proj(x2d, pos, ada3, wts["w_big"], wts["w_small_t"], seq_len, ada_row0)
    gdn_out = _gdn(proj, small_t, wts["gdn_conv_w"], wts["gdn_conv_b"], wts["gdn_params"],
                   wts["gdn_norm_w"], s_gdn0, bsz, seq_len, want_state)
    ssd_out = _ssd(proj, small_t, wts["ssd_conv_w"], wts["ssd_conv_b"], wts["ssd_a_log"], wts["ssd_dt_bias"],
                   wts["ssd_d"], wts["ssd_norm_w"], h_ssd0t, bsz, seq_len, want_state)
    x1, h2, gates = _post(x2d, pos, gdn_out[0], ssd_out[0], proj, ada3, wts["w_gdn_out"], wts["w_ssd_out"],
                          wts["w_o"], wts["ln1_g"], wts["ln1_b"], wts["w_router"], wts["b_router"],
                          seq_len, ada_row0)
    y = _moe(h2, gates, x1, ada3, wts["w_exp_gate"], wts["w_exp_up"], wts["w_exp_down"], wts["ln2_g"],
             wts["ln2_b"], seq_len, ada_row0, pos is not None)
    states = (gdn_out[1], ssd_out[1]) if want_state else None
    return y.reshape(bsz, seq_len, D_MODEL), states


def kernel(x_prompt, x_sample, state_gdn, state_ssd, c, c_ctx, w_ada, b_ada, w_in, gdn_conv_w, gdn_conv_b, gdn_a_log, gdn_dt_bias, gdn_norm_w, w_gdn_out, ssd_conv_w, ssd_conv_b, ssd_a_log, ssd_dt_bias, ssd_d, ssd_norm_w, w_ssd_out, w_o, ln1_g, ln1_b, w_router_group, b_router_group, w_router_expert, b_router_expert, w_exp_gate, w_exp_up, w_exp_down, ln2_g, ln2_b):
    assert w_in.shape[0] == DEPTH == 1
    l = 0
    bsz_c = x_prompt.shape[0]
    bsz_l, seq_l, _ = x_sample.shape

    cvec = jnp.zeros((SUBLANES, D_MODEL), F32).at[0].set(c_ctx).at[1:1 + bsz_l].set(c)
    ada3 = _ada(cvec, w_ada[l], b_ada[l]).reshape(SUBLANES, 1, 6 * D_MODEL)

    wi = w_in[l]
    o_zg = 3072
    o_beta = 4096
    o_a = 4112
    o_xbc = 4128
    o_zs = 7200
    o_dt = 9248
    o_gate = 9312
    w_big = jnp.concatenate([wi[:, 0:o_zg], wi[:, o_zg:o_beta], wi[:, o_xbc:o_zs], wi[:, o_zs:o_dt],
                             wi[:, o_gate:]], axis=1).astype(BF16)
    w_beta = wi[:, o_beta:o_a].T.reshape(2, GDN_HEADS, D_MODEL)
    w_a = wi[:, o_a:o_xbc].T.reshape(2, GDN_HEADS, D_MODEL)
    w_head = jnp.concatenate([w_beta, w_a, jnp.zeros((SUBLANES - 4, GDN_HEADS, D_MODEL), F32)], axis=0)
    w_head = jnp.transpose(w_head, (1, 0, 2)).reshape(ROW_DT, D_MODEL)
    w_small_t = jnp.concatenate([w_head, wi[:, o_dt:o_gate].T], axis=0).astype(BF16)
    assert w_small_t.shape == (SMALL_ROWS, D_MODEL)
    gdn_params = jnp.concatenate([gdn_a_log[l], gdn_dt_bias[l], jnp.zeros((SUBLANES - 4, GDN_HEADS), F32)], axis=0)
    gdn_params = _lane_rep(gdn_params.T)

    w_router = jnp.zeros((D_MODEL, LANES), F32)
    w_router = w_router.at[:, 0:N_EXPERTS].set(w_router_expert[l]).at[:, N_EXPERTS:N_EXPERTS + N_GROUPS].set(
        w_router_group[l])
    b_router = jnp.zeros((1, LANES), F32)
    b_router = b_router.at[0, 0:N_EXPERTS].set(b_router_expert[l]).at[0, N_EXPERTS:N_EXPERTS + N_GROUPS].set(
        b_router_group[l])

    wts = {
        "w_big": w_big, "w_small_t": w_small_t,
        "gdn_conv_w": gdn_conv_w[l], "gdn_conv_b": gdn_conv_b[l].reshape(1, -1),
        "gdn_params": gdn_params,
        "gdn_norm_w": gdn_norm_w[l].reshape(1, -1),
        "ssd_conv_w": ssd_conv_w[l], "ssd_conv_b": ssd_conv_b[l].reshape(1, -1),
        "ssd_a_log": _lane_rep(ssd_a_log[l]), "ssd_dt_bias": _lane_rep(ssd_dt_bias[l]),
        "ssd_d": jnp.repeat(ssd_d[l], SSD_P).reshape(1, -1), "ssd_norm_w": ssd_norm_w[l].reshape(1, -1),
        "w_gdn_out": w_gdn_out[l].astype(BF16), "w_ssd_out": w_ssd_out[l].astype(BF16),
        "w_o": w_o[l].astype(BF16),
        "ln1_g": ln1_g[l].reshape(1, -1), "ln1_b": ln1_b[l].reshape(1, -1),
        "w_router": w_router.astype(BF16), "b_router": b_router,
        "w_exp_gate": w_exp_gate[l].astype(BF16), "w_exp_up": w_exp_up[l].astype(BF16),
        "w_exp_down": w_exp_down[l].astype(BF16),
        "ln2_g": ln2_g[l].reshape(1, -1), "ln2_b": ln2_b[l].reshape(1, -1),
    }

    pos = _grid_pos_embed(seq_l, D_MODEL)
    h0 = state_ssd[:, l].reshape(bsz_l, 2, SSD_GROUPS, GROUP_W, SSD_N)

    y_ctx, (s_gdn, h_ssd) = _stream(x_prompt, None, ada3, 0, None, None, True, wts)
    y_lat, _ = _stream(x_sample, pos, ada3, 1, state_gdn, h0, False, wts)

    new_ssd = h_ssd.reshape(bsz_c, 1, 2, SSD_HEADS, SSD_P, SSD_N)
    return (y_ctx, y_lat, s_gdn, new_ssd)
```

```python
import functools
import math

import jax
import jax.numpy as jnp
import numpy as np
from jax import lax
from jax.experimental import pallas as pl
from jax.experimental.pallas import tpu as pltpu

F32 = jnp.float32
BF16 = jnp.bfloat16
HIGHEST = lax.Precision.HIGHEST

D_MODEL = 1024
GRID_W = 64
POS_BASE = 10000.0
CONV_K = 5
GDN_HEADS = 8
GDN_DK = 128
GDN_DV = 128
SSD_HEADS = 32
SSD_P = 64
SSD_INNER = SSD_HEADS * SSD_P
SSD_GROUPS = 4
SSD_N = 128
HEADS_PER_GROUP = SSD_HEADS // SSD_GROUPS
GROUP_W = HEADS_PER_GROUP * SSD_P
N_GROUPS = 4
EXPERTS_PER_GROUP = 8
N_EXPERTS = N_GROUPS * EXPERTS_PER_GROUP
D_EXPERT = 256
EPS = 1e-6
LOG2E = math.log2(math.e)
DEPTH = 1
ALPHA = (2.0 * DEPTH) ** 0.25

LANES = 128
SUBLANES = 8
CHUNK = 128
CONV_PAD = SUBLANES
VMEM_LIMIT = 56 * 1024 * 1024

COL_QKV = 0
COL_ZG = 3072
COL_XBC = 4096
COL_ZS = 7168
COL_GATE = 9216
PROJ_COLS = 11264
PROJ_TN = 1024
GDN_ROW_BETA = 0
GDN_ROW_A = 2
ROW_DT = GDN_HEADS * SUBLANES
SMALL_ROWS = 128

TOK_BLOCK = 2048
POST_BLOCK = 512
MOE_WINDOW = 1024
MOE_TILE = 288
GROUP_LANE = N_EXPERTS
SSD_CHUNKS_PER_STEP = 8


def _bdot(a, b):
    return jnp.dot(a.astype(BF16), b.astype(BF16), preferred_element_type=F32)


def _sigmoid(x):
    return 0.5 * jnp.tanh(0.5 * x) + 0.5


def _silu(x):
    half = 0.5 * x
    return half + half * jnp.tanh(half)


def _softplus(x):
    return jnp.maximum(x, 0.0) + jnp.log1p(jnp.exp(-jnp.abs(x)))


def _ln(x):
    mu = jnp.mean(x, axis=-1, keepdims=True)
    xc = x - mu
    var = jnp.mean(xc * xc, axis=-1, keepdims=True)
    return xc * lax.rsqrt(var + EPS)


def _lane_scan(x, reverse):
    lane = lax.broadcasted_iota(jnp.int32, x.shape, 1)
    s = 1
    while s < CHUNK:
        if reverse:
            shifted = pltpu.roll(x, CHUNK - s, axis=1)
            x = x + jnp.where(lane < CHUNK - s, shifted, 0.0)
        else:
            shifted = pltpu.roll(x, s, axis=1)
            x = x + jnp.where(lane >= s, shifted, 0.0)
        s *= 2
    return x


def _tri_masks(reverse):
    row = lax.broadcasted_iota(jnp.int32, (CHUNK, CHUNK), 0)
    col = lax.broadcasted_iota(jnp.int32, (CHUNK, CHUNK), 1)
    if reverse:
        return row <= col, row < col
    return row >= col, row > col


def _decay_matrix(acc_row, incl):
    acc_rb = jnp.broadcast_to(acc_row, (CHUNK, CHUNK))
    acc_col = acc_rb.T
    decay = jnp.exp(jnp.where(incl, acc_col - acc_rb, -jnp.inf))
    return decay, acc_col


def _conv_tap_sum(pad_ref, t0, lo, hi, w_ref, b_ref):
    acc = jnp.broadcast_to(b_ref[...], (CHUNK, hi - lo))
    for j in range(CONV_K):
        start = t0 + CONV_PAD - CONV_K // 2 + j
        acc = acc + pad_ref[start:start + CHUNK, lo:hi] * w_ref[j:j + 1, :]
    return acc


def _ada_kernel(c_ref, w_ref, b_ref, o_ref):
    s = _silu(c_ref[...])
    o_ref[...] = jnp.dot(s, w_ref[...], precision=HIGHEST, preferred_element_type=F32) + b_ref[...]


def _ada(cvec, w_ada, b_ada):
    n_out = w_ada.shape[1]
    tn = 1024
    return pl.pallas_call(
        _ada_kernel,
        grid=(n_out // tn,),
        in_specs=[
            pl.BlockSpec((SUBLANES, D_MODEL), lambda j: (0, 0)),
            pl.BlockSpec((D_MODEL, tn), lambda j: (0, j)),
            pl.BlockSpec((1, tn), lambda j: (0, j)),
        ],
        out_specs=pl.BlockSpec((SUBLANES, tn), lambda j: (0, j)),
        out_shape=jax.ShapeDtypeStruct((SUBLANES, n_out), F32),
        compiler_params=pltpu.CompilerParams(dimension_semantics=("arbitrary",), vmem_limit_bytes=VMEM_LIMIT),
        name="ada",
    )(cvec, w_ada, b_ada.reshape(1, n_out))


def _inproj_kernel(*refs, has_pos):
    if has_pos:
        x_ref, pos_ref, ada_ref, w_ref, wst_ref, proj_ref, small_ref, h_ref = refs
    else:
        x_ref, ada_ref, w_ref, wst_ref, proj_ref, small_ref, h_ref = refs
        pos_ref = None

    @pl.when(pl.program_id(1) == 0)
    def _():
        x = x_ref[...]
        if pos_ref is not None:
            x = x + pos_ref[...]
        shift = ada_ref[:, 0:D_MODEL]
        scale = ada_ref[:, D_MODEL:2 * D_MODEL]
        h = (_ln(x) * (1.0 + scale) + shift).astype(BF16)
        h_ref[...] = h
        small_ref[...] = lax.dot_general(wst_ref[...], h, (((1,), (1,)), ((), ())), preferred_element_type=F32)

    proj_ref[...] = jnp.dot(h_ref[...], w_ref[...], preferred_element_type=F32).astype(BF16)


def _in_proj(x2d, pos, ada3, w_big, w_small_t, seq_len, ada_row0):
    n_tok = x2d.shape[0]
    t = TOK_BLOCK
    blocks_per_seq = seq_len // t
    has_pos = pos is not None

    def ada_map(i, j):
        return (ada_row0 + (i // blocks_per_seq if has_pos else 0), 0, 0)

    in_specs = [pl.BlockSpec((t, D_MODEL), lambda i, j: (i, 0))]
    args = [x2d]
    if has_pos:
        pos_mode = dict(pipeline_mode=pl.Buffered(1)) if blocks_per_seq == 1 else {}
        in_specs.append(pl.BlockSpec((t, D_MODEL), lambda i, j: (i % blocks_per_seq, 0), **pos_mode))
        args.append(pos)
    in_specs += [
        pl.BlockSpec((None, 1, 6 * D_MODEL), ada_map),
        pl.BlockSpec((D_MODEL, PROJ_TN), lambda i, j: (0, j)),
        pl.BlockSpec((SMALL_ROWS, D_MODEL), lambda i, j: (0, 0)),
    ]
    args += [ada3, w_big, w_small_t]
    return pl.pallas_call(
        functools.partial(_inproj_kernel, has_pos=has_pos),
        grid=(n_tok // t, PROJ_COLS // PROJ_TN),
        in_specs=in_specs,
        out_specs=[
            pl.BlockSpec((t, PROJ_TN), lambda i, j: (i, j)),
            pl.BlockSpec((SMALL_ROWS, t), lambda i, j: (0, i)),
        ],
        out_shape=[
            jax.ShapeDtypeStruct((n_tok, PROJ_COLS), BF16),
            jax.ShapeDtypeStruct((SMALL_ROWS, n_tok), F32),
        ],
        scratch_shapes=[pltpu.VMEM((t, D_MODEL), BF16)],
        compiler_params=pltpu.CompilerParams(dimension_semantics=("arbitrary", "arbitrary"),
                                             vmem_limit_bytes=VMEM_LIMIT),
        name="in_proj",
    )(*args)


INV_BASE = 8
GDN_GROUP = 16


def _take_blocks(m, size, odd):
    parts = [m[k * size:(k + 1) * size] for k in range(CHUNK // size) if (k % 2 == 1) == odd]
    return parts[0] if len(parts) == 1 else jnp.concatenate(parts, axis=0)


def _interleave_blocks(even_rows, odd_rows, size):
    parts = []
    for k in range(CHUNK // (2 * size)):
        parts.append(even_rows[k * size:(k + 1) * size])
        parts.append(odd_rows[k * size:(k + 1) * size])
    return jnp.concatenate(parts, axis=0)


def _unit_tri_inverses(nmats, uppers):
    row = lax.broadcasted_iota(jnp.int32, (CHUNK, CHUNK), 0)
    col = lax.broadcasted_iota(jnp.int32, (CHUNK, CHUNK), 1)

    def same_block(size):
        shift = int(math.log2(size))
        return (row >> shift) == (col >> shift)

    eye = (row == col).astype(F32)
    base = same_block(INV_BASE)
    nds = [jnp.where(base, n, 0.0) for n in nmats]
    xs = [eye - nd for nd in nds]
    pws = [_bdot(nd, nd) for nd in nds]
    size = 2
    while True:
        xs = [x + _bdot(x, pw) for x, pw in zip(xs, pws)]
        size *= 2
        if size >= INV_BASE:
            break
        pws = [_bdot(pw, pw) for pw in pws]
    size = INV_BASE
    while size < CHUNK:
        coupling = same_block(2 * size) & jnp.logical_not(same_block(size))
        offs = [_take_blocks(jnp.where(coupling, n, 0.0), size, odd=not up) for n, up in zip(nmats, uppers)]
        tmps = [_bdot(off, x) for off, x in zip(offs, xs)]
        zeros = jnp.zeros((CHUNK // 2, CHUNK), F32)
        fulls = [_interleave_blocks(t, zeros, size) if up else _interleave_blocks(zeros, t, size)
                 for t, up in zip(tmps, uppers)]
        moved = [_take_blocks(x, size, odd=not up) for x, up in zip(xs, uppers)]
        kept = [_take_blocks(x, size, odd=up) for x, up in zip(xs, uppers)]
        news = [m - _bdot(m, f) for m, f in zip(moved, fulls)]
        xs = [_interleave_blocks(nw, kp, size) if up else _interleave_blocks(kp, nw, size)
              for nw, kp, up in zip(news, kept, uppers)]
        size *= 2
    return xs


def _gdn_prepare(chunks, qn_ref, kn_ref, vn_ref, kt_ref, sc_ref, lhs_ref, sb_ref, ob_ref, eg_ref):
    grams = [_bdot(jnp.concatenate([kn_ref[g], qn_ref[g]], axis=0), kt_ref[g]) for g in chunks]
    units = []
    for g, gram in zip(chunks, grams):
        for d in range(2):
            reverse = d == 1
            incl, strict = _tri_masks(reverse)
            beta = sc_ref[g, d:d + 1, :]
            gc = sc_ref[g, 2 + d:3 + d, :]
            decay, gcol = _decay_matrix(gc, incl)
            beta_rb = jnp.broadcast_to(beta, (CHUNK, CHUNK))
            glast = gc[:, 0:1] if reverse else gc[:, CHUNK - 1:CHUNK]
            sc_ref[g, 4 + d:5 + d, :] = jnp.broadcast_to(jnp.exp(glast), (1, CHUNK))
            eg_ref[g, d] = jnp.exp(gcol)
            units.append(dict(
                g=g, d=d,
                nmat=jnp.where(strict, gram[0:CHUNK] * decay, 0.0) * beta_rb,
                att=gram[CHUNK:2 * CHUNK] * decay * beta_rb,
                kd_scale=beta * jnp.exp(glast - gc)))
    xinvs = _unit_tri_inverses([u["nmat"] for u in units], [u["d"] == 1 for u in units])
    uws = [_bdot(x, jnp.concatenate([vn_ref[u["g"]], kn_ref[u["g"]] * eg_ref[u["g"], u["d"]]], axis=1))
           for u, x in zip(units, xinvs)]
    mixeds = []
    for u, uw in zip(units, uws):
        kd = kt_ref[u["g"]] * jnp.broadcast_to(u["kd_scale"], (CHUNK, CHUNK))
        mixeds.append(_bdot(jnp.concatenate([kd, u["att"]], axis=0), uw))
    for u, mixed in zip(units, mixeds):
        g, d = u["g"], u["d"]
        lhs_ref[g, d, 0:CHUNK, :] = mixed[0:CHUNK, GDN_DV:].astype(BF16)
        lhs_ref[g, d, CHUNK:2 * CHUNK, :] = qn_ref[g].astype(BF16)
        lhs_ref[g, d, 2 * CHUNK:3 * CHUNK, :] = mixed[CHUNK:, GDN_DV:].astype(BF16)
        sb_ref[g, d] = mixed[0:CHUNK, 0:GDN_DV]
        ob_ref[g, d] = mixed[CHUNK:, 0:GDN_DV]


def _gdn_steps(steps, lhs_ref, sb_ref, ob_ref, eg_ref, sc_ref, s_ref, o_refs):
    states = [s_ref[s, d] for _, s, d in steps]
    rs = [jnp.dot(lhs_ref[g, d], st.astype(BF16), preferred_element_type=F32)
          for (g, _, d), st in zip(steps, states)]
    for (g, s, d), st, r in zip(steps, states, rs):
        s_ref[s, d] = st * sc_ref[g, 4 + d:5 + d, :] - r[0:CHUNK] + sb_ref[g, d]
        o_refs[d][g] = eg_ref[g, d] * r[CHUNK:2 * CHUNK] - r[2 * CHUNK:3 * CHUNK] + ob_ref[g, d]


def _gdn_kernel(*refs, n_seq, n_chunks, has_s0, want_state):
    it = iter(refs)
    q_ref, k_ref, v_ref, z_ref, sm_ref = (next(it) for _ in range(5))
    wq_ref, wk_ref, wv_ref, bq_ref, bk_ref, bv_ref = (next(it) for _ in range(6))
    par_ref, nw_ref = (next(it) for _ in range(2))
    s0_ref = next(it) if has_s0 else None
    og_ref = next(it)
    sout_ref = next(it) if want_state else None
    pad_ref, qn_ref, kn_ref, vn_ref, kt_ref, of_ref, ob_ref, sc_ref, s_ref = (next(it) for _ in range(9))
    lhs_ref, sb_ref, ou_ref, eg_ref, st_ref = (next(it) for _ in range(5))

    seq = n_chunks * CHUNK
    total = n_seq * n_chunks
    zero_rows = jnp.zeros((CONV_PAD, 3 * LANES), F32)
    for s in range(n_seq):
        pad_ref[s, 0:CONV_PAD, :] = zero_rows
        pad_ref[s, CONV_PAD + seq:2 * CONV_PAD + seq, :] = zero_rows
        for c in range(n_chunks):
            t0 = c * CHUNK
            r0 = s * seq + t0
            pad_ref[s, CONV_PAD + t0:CONV_PAD + t0 + CHUNK, 0:LANES] = q_ref[r0:r0 + CHUNK, :].astype(F32)
            pad_ref[s, CONV_PAD + t0:CONV_PAD + t0 + CHUNK, LANES:2 * LANES] = k_ref[r0:r0 + CHUNK, :].astype(F32)
            pad_ref[s, CONV_PAD + t0:CONV_PAD + t0 + CHUNK, 2 * LANES:3 * LANES] = v_ref[r0:r0 + CHUNK, :].astype(F32)

    a_neg = [-jnp.exp(par_ref[d:d + 1, :]) for d in range(2)]
    dt_bias = [par_ref[2 + d:3 + d, :] for d in range(2)]
    for s in range(n_seq):
        for c in range(n_chunks):
            t0 = c * CHUNK
            g = s * n_chunks + c
            pad_s = pad_ref.at[s]
            qa = _silu(_conv_tap_sum(pad_s, t0, 0, LANES, wq_ref, bq_ref))
            ka = _silu(_conv_tap_sum(pad_s, t0, LANES, 2 * LANES, wk_ref, bk_ref))
            va = _silu(_conv_tap_sum(pad_s, t0, 2 * LANES, 3 * LANES, wv_ref, bv_ref))
            qn = qa * lax.rsqrt(jnp.sum(qa * qa, axis=-1, keepdims=True) + EPS) * (GDN_DK ** -0.5)
            kn = ka * lax.rsqrt(jnp.sum(ka * ka, axis=-1, keepdims=True) + EPS)
            qn_ref[g] = qn
            kn_ref[g] = kn
            vn_ref[g] = va
            kt_ref[g] = kn.T

    for g in range(total):
        r0 = g * CHUNK
        for d in range(2):
            st_ref[d, g:g + 1, :] = sm_ref[GDN_ROW_A + d:GDN_ROW_A + d + 1, r0:r0 + CHUNK]
            st_ref[2 + d, g:g + 1, :] = sm_ref[GDN_ROW_BETA + d:GDN_ROW_BETA + d + 1, r0:r0 + CHUNK]
    for d in range(2):
        gcum = _lane_scan(a_neg[d] * _softplus(st_ref[d] + dt_bias[d]), reverse=d == 1)
        beta = _sigmoid(st_ref[2 + d])
        for g in range(total):
            sc_ref[g, d:d + 1, :] = beta[g:g + 1, :]
            sc_ref[g, 2 + d:3 + d, :] = gcum[g:g + 1, :]

    def prepare(i, carry):
        _gdn_prepare([i * GDN_GROUP + j for j in range(GDN_GROUP)], qn_ref, kn_ref, vn_ref, kt_ref, sc_ref,
                     lhs_ref, sb_ref, ou_ref, eg_ref)
        return carry

    lax.fori_loop(0, total // GDN_GROUP, prepare, 0)

    if has_s0:
        s_ref[...] = s0_ref[...]
    else:
        s_ref[...] = jnp.zeros(s_ref.shape, F32)

    def advance(i, carry):
        steps = []
        for s in range(n_seq):
            steps.append((s * n_chunks + i, s, 0))
            steps.append((s * n_chunks + n_chunks - 1 - i, s, 1))
        _gdn_steps(steps, lhs_ref, sb_ref, ou_ref, eg_ref, sc_ref, s_ref, (of_ref, ob_ref))
        return carry

    lax.fori_loop(0, n_chunks, advance, 0)

    if want_state:
        sout_ref[...] = s_ref[...]
    for g in range(total):
        r0 = g * CHUNK
        o = of_ref[g] + ob_ref[g]
        o = o * lax.rsqrt(jnp.mean(o * o, axis=-1, keepdims=True) + EPS)
        og_ref[r0:r0 + CHUNK, :] = (o * nw_ref[...] * _silu(z_ref[r0:r0 + CHUNK, :].astype(F32))).astype(BF16)


def _gdn(proj, small_t, conv_w, conv_b, head_params, norm_w, s0, bsz, seq_len, want_state):
    n_chunks = seq_len // CHUNK
    n_tok = bsz * seq_len
    has_s0 = s0 is not None
    n_seq = max(1, GDN_GROUP // n_chunks)
    assert bsz % n_seq == 0 and (n_seq * n_chunks) % GDN_GROUP == 0
    total = n_seq * n_chunks
    rows = n_seq * seq_len
    col = lambda off: (lambda b, h: (b, off + h))
    cw = lambda off: (lambda b, h: (0, off + h))
    in_specs = [
        pl.BlockSpec((rows, LANES), col(COL_QKV // LANES)),
        pl.BlockSpec((rows, LANES), col(COL_QKV // LANES + GDN_HEADS)),
        pl.BlockSpec((rows, LANES), col(COL_QKV // LANES + 2 * GDN_HEADS)),
        pl.BlockSpec((rows, LANES), col(COL_ZG // LANES)),
        pl.BlockSpec((SUBLANES, rows), lambda b, h: (h, b)),
        pl.BlockSpec((CONV_K, LANES), cw(0)),
        pl.BlockSpec((CONV_K, LANES), cw(GDN_HEADS)),
        pl.BlockSpec((CONV_K, LANES), cw(2 * GDN_HEADS)),
        pl.BlockSpec((1, LANES), cw(0)),
        pl.BlockSpec((1, LANES), cw(GDN_HEADS)),
        pl.BlockSpec((1, LANES), cw(2 * GDN_HEADS)),
        pl.BlockSpec((None, SUBLANES, LANES), lambda b, h: (h, 0, 0)),
        pl.BlockSpec((1, LANES), lambda b, h: (0, 0)),
    ]
    args = [proj, proj, proj, proj, small_t, conv_w, conv_w, conv_w, conv_b, conv_b, conv_b,
            head_params, norm_w]
    state_spec = pl.BlockSpec((n_seq, None, 2, None, GDN_DK, GDN_DV), lambda b, h: (b, 0, 0, h, 0, 0))
    if has_s0:
        in_specs.append(state_spec)
        args.append(s0)
    out_specs = [pl.BlockSpec((rows, LANES), lambda b, h: (b, h))]
    out_shape = [jax.ShapeDtypeStruct((n_tok, GDN_HEADS * GDN_DV), BF16)]
    if want_state:
        out_specs.append(state_spec)
        out_shape.append(jax.ShapeDtypeStruct((bsz, 1, 2, GDN_HEADS, GDN_DK, GDN_DV), F32))
    chunked = pltpu.VMEM((total, CHUNK, LANES), F32)
    per_dir = pltpu.VMEM((total, 2, CHUNK, LANES), F32)
    return pl.pallas_call(
        functools.partial(_gdn_kernel, n_seq=n_seq, n_chunks=n_chunks, has_s0=has_s0, want_state=want_state),
        grid=(bsz // n_seq, GDN_HEADS),
        in_specs=in_specs,
        out_specs=out_specs,
        out_shape=out_shape,
        scratch_shapes=[
            pltpu.VMEM((n_seq, seq_len + 2 * CONV_PAD, 3 * LANES), F32),
            chunked, chunked, chunked, chunked, chunked, chunked,
            pltpu.VMEM((total, SUBLANES, CHUNK), F32),
            pltpu.VMEM((n_seq, 2, GDN_DK, GDN_DV), F32),
            pltpu.VMEM((total, 2, 3 * CHUNK, GDN_DK), BF16),
            per_dir, per_dir, per_dir,
            pltpu.VMEM((4, total, CHUNK), F32),
        ],
        compiler_params=pltpu.CompilerParams(dimension_semantics=("arbitrary", "arbitrary"),
                                             vmem_limit_bytes=VMEM_LIMIT),
        name="gdn",
    )(*args)


def _ssd_steps(steps, xs_ref, cs_ref, bt_ref, dt_ref, ac_ref, ht_ref, y_ref):
    lane = lax.broadcasted_iota(jnp.int32, (CHUNK, LANES), 1)
    low_half = lane < SSD_P
    shared = [(_bdot(cs_ref[c], bt_ref[c]), _bdot(cs_ref[c], ht_ref[s, d])) for c, s, d in steps]
    pairs_per_stage = 2
    for p0 in range(0, HEADS_PER_GROUP // 2, pairs_per_stage):
        units = []
        for (c, s, d), (cb, y_off) in zip(steps, shared):
            reverse = d == 1
            incl, _ = _tri_masks(reverse)
            bt = bt_ref[c]
            for p in range(p0, p0 + pairs_per_stage):
                halves = []
                for r in (2 * p, 2 * p + 1):
                    row = d * HEADS_PER_GROUP + r
                    dt = dt_ref[c, row:row + 1, :]
                    ac = ac_ref[c, row:row + 1, :]
                    acol = jnp.broadcast_to(ac, (CHUNK, CHUNK)).T
                    shifted = jnp.broadcast_to(ac - jnp.log2(dt), (CHUNK, CHUNK))
                    last = ac[:, 0:1] if reverse else ac[:, CHUNK - 1:CHUNK]
                    halves.append(dict(
                        m=(cb * jnp.exp2(jnp.where(incl, acol - shifted, -jnp.inf))).astype(BF16),
                        bts=(bt * jnp.broadcast_to(dt * jnp.exp2(last - ac), (SSD_N, CHUNK))).astype(BF16),
                        acol=acol, elast=jnp.exp2(last)))
                units.append(dict(c=c, s=s, d=d, p=p, halves=halves, y_off=y_off[:, p * LANES:(p + 1) * LANES]))
        for u in units:
            x = xs_ref[u["c"], :, u["p"] * LANES:(u["p"] + 1) * LANES]
            zero = jnp.zeros_like(x)
            h0, h1 = u["halves"]
            lhs = jnp.concatenate([jnp.concatenate([h0["m"], h1["m"]], axis=1),
                                   jnp.concatenate([h0["bts"], h1["bts"]], axis=1)], axis=0)
            rhs = jnp.concatenate([jnp.where(low_half, x, zero), jnp.where(low_half, zero, x)], axis=0)
            u["prod"] = jnp.dot(lhs, rhs, preferred_element_type=F32)
        for u in units:
            c, s, d, p = u["c"], u["s"], u["d"], u["p"]
            h0, h1 = u["halves"]
            eoff = jnp.exp2(jnp.where(low_half, h0["acol"], h1["acol"]))
            elast = jnp.where(low_half, h0["elast"], h1["elast"])
            cols = slice(p * LANES, (p + 1) * LANES)
            y_ref[c, :, cols] = y_ref[c, :, cols] + u["prod"][0:CHUNK] + eoff * u["y_off"]
            ht_ref[s, d, :, cols] = ht_ref[s, d, :, cols] * elast + u["prod"][CHUNK:CHUNK + SSD_N]


def _ssd_kernel(*refs, n_seq, n_chunks, has_h0, want_state):
    it = iter(refs)
    x_ref, b_ref, c_ref, z_ref, dtf_ref, dtb_ref = (next(it) for _ in range(6))
    wx_ref, wb_ref, wc_ref, bx_ref, bb_ref, bc_ref = (next(it) for _ in range(6))
    alog_ref, dtbias_ref, dvec_ref, nw_ref = (next(it) for _ in range(4))
    h0_ref = next(it) if has_h0 else None
    yg_ref = next(it)
    hout_ref = next(it) if want_state else None
    pad_ref, xs_ref, cs_ref, bt_ref, dt_ref, ac_ref, y_ref, ht_ref = (next(it) for _ in range(8))

    seq = n_chunks * CHUNK
    total = n_seq * n_chunks
    width = GROUP_W + 2 * SSD_N
    zero_rows = jnp.zeros((CONV_PAD, width), F32)
    for s in range(n_seq):
        pad_ref[s, 0:CONV_PAD, :] = zero_rows
        pad_ref[s, CONV_PAD + seq:2 * CONV_PAD + seq, :] = zero_rows
        for c in range(n_chunks):
            t0 = c * CHUNK
            r0 = s * seq + t0
            pad_ref[s, CONV_PAD + t0:CONV_PAD + t0 + CHUNK, 0:GROUP_W] = x_ref[r0:r0 + CHUNK, :].astype(F32)
            pad_ref[s, CONV_PAD + t0:CONV_PAD + t0 + CHUNK, GROUP_W:GROUP_W + SSD_N] = b_ref[r0:r0 + CHUNK, :].astype(F32)
            pad_ref[s, CONV_PAD + t0:CONV_PAD + t0 + CHUNK, GROUP_W + SSD_N:width] = c_ref[r0:r0 + CHUNK, :].astype(F32)

    raw_dt = (dtf_ref, dtb_ref)
    for s in range(n_seq):
        pad_s = pad_ref.at[s]
        for c in range(n_chunks):
            t0 = c * CHUNK
            g = s * n_chunks + c
            for lo in range(0, GROUP_W, LANES):
                cols = slice(lo, lo + LANES)
                xa = _silu(_conv_tap_sum(pad_s, t0, lo, lo + LANES, wx_ref.at[:, cols], bx_ref.at[:, cols]))
                xs_ref[g, :, cols] = xa.astype(BF16)
                y_ref[g, :, cols] = xa * dvec_ref[:, cols]
            ba = _silu(_conv_tap_sum(pad_s, t0, GROUP_W, GROUP_W + SSD_N, wb_ref, bb_ref))
            ca = _silu(_conv_tap_sum(pad_s, t0, GROUP_W + SSD_N, width, wc_ref, bc_ref))
            cs_ref[g] = ca
            bt_ref[g] = ba.T
            for d in range(2):
                dt_ref[g, d * HEADS_PER_GROUP:(d + 1) * HEADS_PER_GROUP, :] = raw_dt[d][:, g * CHUNK:(g + 1) * CHUNK]

    for d in range(2):
        lo, hi = d * HEADS_PER_GROUP, (d + 1) * HEADS_PER_GROUP
        dt = _softplus(dt_ref[:, lo:hi, :] + dtbias_ref[d][None])
        dt_ref[:, lo:hi, :] = dt
        scaled = (dt * (-jnp.exp(alog_ref[d]))[None]).reshape(total * HEADS_PER_GROUP, CHUNK)
        ac_ref[:, lo:hi, :] = (_lane_scan(scaled, reverse=d == 1) * LOG2E).reshape(total, HEADS_PER_GROUP, CHUNK)

    if has_h0:
        for s in range(n_seq):
            for d in range(2):
                for lo in range(0, GROUP_W, LANES):
                    ht_ref[s, d, :, lo:lo + LANES] = h0_ref[s, d, lo:lo + LANES, :].T
    else:
        ht_ref[...] = jnp.zeros(ht_ref.shape, F32)

    def body(i, carry):
        for s in range(n_seq):
            _ssd_steps([(s * n_chunks + i, s, 0), (s * n_chunks + n_chunks - 1 - i, s, 1)],
                       xs_ref, cs_ref, bt_ref, dt_ref, ac_ref, ht_ref, y_ref)
        return carry

    lax.fori_loop(0, n_chunks, body, 0)

    if want_state:
        for s in range(n_seq):
            for d in range(2):
                for lo in range(0, GROUP_W, LANES):
                    hout_ref[s, d, lo:lo + LANES, :] = ht_ref[s, d, :, lo:lo + LANES].T
    for g in range(total):
        r0 = g * CHUNK
        ssq = jnp.zeros((CHUNK, 1), F32)
        for lo in range(0, GROUP_W, LANES):
            cols = slice(lo, lo + LANES)
            y = y_ref[g, :, cols] * _silu(z_ref[r0:r0 + CHUNK, cols].astype(F32))
            y_ref[g, :, cols] = y
            ssq = ssq + jnp.sum(y * y, axis=-1, keepdims=True)
        inv = lax.rsqrt(ssq * (1.0 / GROUP_W) + EPS)
        for lo in range(0, GROUP_W, LANES):
            cols = slice(lo, lo + LANES)
            yg_ref[r0:r0 + CHUNK, cols] = (y_ref[g, :, cols] * inv * nw_ref[:, cols]).astype(BF16)


def _ssd(proj, small_t, conv_w, conv_b, a_log_rep, dt_bias_rep, d_vec, norm_w, h0t, bsz, seq_len, want_state):
    n_chunks = seq_len // CHUNK
    n_tok = bsz * seq_len
    has_h0 = h0t is not None
    n_seq = max(1, SSD_CHUNKS_PER_STEP // n_chunks)
    assert bsz % n_seq == 0
    total = n_seq * n_chunks
    rows_per_step = n_seq * seq_len
    xbc_w = COL_XBC // GROUP_W
    bc_l = (COL_XBC + SSD_INNER) // LANES
    in_specs = [
        pl.BlockSpec((rows_per_step, GROUP_W), lambda b, g: (b, xbc_w + g)),
        pl.BlockSpec((rows_per_step, SSD_N), lambda b, g: (b, bc_l + g)),
        pl.BlockSpec((rows_per_step, SSD_N), lambda b, g: (b, bc_l + SSD_GROUPS + g)),
        pl.BlockSpec((rows_per_step, GROUP_W), lambda b, g: (b, COL_ZS // GROUP_W + g)),
        pl.BlockSpec((HEADS_PER_GROUP, rows_per_step), lambda b, g: (ROW_DT // HEADS_PER_GROUP + g, b)),
        pl.BlockSpec((HEADS_PER_GROUP, rows_per_step),
                     lambda b, g: ((ROW_DT + SSD_HEADS) // HEADS_PER_GROUP + g, b)),
        pl.BlockSpec((CONV_K, GROUP_W), lambda b, g: (0, g)),
        pl.BlockSpec((CONV_K, SSD_N), lambda b, g: (0, SSD_INNER // SSD_N + g)),
        pl.BlockSpec((CONV_K, SSD_N), lambda b, g: (0, SSD_INNER // SSD_N + SSD_GROUPS + g)),
        pl.BlockSpec((1, GROUP_W), lambda b, g: (0, g)),
        pl.BlockSpec((1, SSD_N), lambda b, g: (0, SSD_INNER // SSD_N + g)),
        pl.BlockSpec((1, SSD_N), lambda b, g: (0, SSD_INNER // SSD_N + SSD_GROUPS + g)),
        pl.BlockSpec((2, HEADS_PER_GROUP, LANES), lambda b, g: (0, g, 0)),
        pl.BlockSpec((2, HEADS_PER_GROUP, LANES), lambda b, g: (0, g, 0)),
        pl.BlockSpec((1, GROUP_W), lambda b, g: (0, g)),
        pl.BlockSpec((1, GROUP_W), lambda b, g: (0, g)),
    ]
    args = [proj, proj, proj, proj, small_t, small_t, conv_w, conv_w, conv_w, conv_b, conv_b, conv_b,
            a_log_rep, dt_bias_rep, d_vec, norm_w]
    state_spec = pl.BlockSpec((n_seq, 2, None, GROUP_W, SSD_N), lambda b, g: (b, 0, g, 0, 0))
    if has_h0:
        in_specs.append(state_spec)
        args.append(h0t)
    out_specs = [pl.BlockSpec((rows_per_step, GROUP_W), lambda b, g: (b, g))]
    out_shape = [jax.ShapeDtypeStruct((n_tok, SSD_INNER), BF16)]
    if want_state:
        out_specs.append(state_spec)
        out_shape.append(jax.ShapeDtypeStruct((bsz, 2, SSD_GROUPS, GROUP_W, SSD_N), F32))
    rows = pltpu.VMEM((total, 2 * HEADS_PER_GROUP, CHUNK), F32)
    return pl.pallas_call(
        functools.partial(_ssd_kernel, n_seq=n_seq, n_chunks=n_chunks, has_h0=has_h0, want_state=want_state),
        grid=(bsz // n_seq, SSD_GROUPS),
        in_specs=in_specs,
        out_specs=out_specs,
        out_shape=out_shape,
        scratch_shapes=[
            pltpu.VMEM((n_seq, seq_len + 2 * CONV_PAD, GROUP_W + 2 * SSD_N), F32),
            pltpu.VMEM((total, CHUNK, GROUP_W), BF16),
            pltpu.VMEM((total, CHUNK, SSD_N), F32),
            pltpu.VMEM((total, SSD_N, CHUNK), F32),
            rows, rows,
            pltpu.VMEM((total, CHUNK, GROUP_W), F32),
            pltpu.VMEM((n_seq, 2, SSD_N, GROUP_W), F32),
        ],
        compiler_params=pltpu.CompilerParams(dimension_semantics=("arbitrary", "arbitrary"),
                                             vmem_limit_bytes=VMEM_LIMIT),
        name="ssd",
    )(*args)


def _route(logits):
    lane = lax.broadcasted_iota(jnp.int32, logits.shape, 1)
    neg = -jnp.inf
    is_grp = (lane >= N_EXPERTS) & (lane < N_EXPERTS + N_GROUPS)
    gl = jnp.where(is_grp, logits, neg)
    gmax = jnp.max(gl, axis=-1, keepdims=True)
    ge = jnp.exp(gl - gmax)
    p_grp = ge / jnp.sum(ge, axis=-1, keepdims=True)
    p_top = jnp.max(p_grp, axis=-1, keepdims=True)
    g_idx = jnp.min(jnp.where(is_grp & (p_grp == p_top), lane, 2 * LANES), axis=-1, keepdims=True) - N_EXPERTS
    in_grp = (lane >= g_idx * EXPERTS_PER_GROUP) & (lane < (g_idx + 1) * EXPERTS_PER_GROUP)
    el = jnp.where(in_grp, logits, neg)
    emax = jnp.max(el, axis=-1, keepdims=True)
    ee = jnp.exp(el - emax)
    p_e = ee / jnp.sum(ee, axis=-1, keepdims=True)
    w1 = jnp.max(p_e, axis=-1, keepdims=True)
    i1 = jnp.min(jnp.where(in_grp & (p_e == w1), lane, 2 * LANES), axis=-1, keepdims=True)
    rest = jnp.where(in_grp & (lane != i1), p_e, -1.0)
    w2 = jnp.max(rest, axis=-1, keepdims=True)
    i2 = jnp.min(jnp.where(rest == w2, lane, 2 * LANES), axis=-1, keepdims=True)
    tot = w1 + w2
    gates = jnp.where(lane == i1, w1 / tot * p_top, 0.0) + jnp.where(lane == i2, w2 / tot * p_top, 0.0)
    return jnp.where(lane == GROUP_LANE, g_idx.astype(F32), gates)


def _post_kernel(*refs, has_pos):
    it = iter(refs)
    x_ref = next(it)
    pos_ref = next(it) if has_pos else None
    og_ref, yg_ref, gg_ref, gs_ref, ada_ref = (next(it) for _ in range(5))
    wg_ref, ws_ref, wo_ref, g1_ref, b1_ref, wr_ref, br_ref = (next(it) for _ in range(7))
    x1_ref, h2_ref, gates_ref = (next(it) for _ in range(3))

    x = x_ref[...]
    if has_pos:
        x = x + pos_ref[...]
    u_g = jnp.dot(og_ref[...], wg_ref[...], preferred_element_type=F32)
    u_s = jnp.dot(yg_ref[...], ws_ref[...], preferred_element_type=F32)
    m = _sigmoid(gg_ref[...].astype(F32)) * u_g + _sigmoid(gs_ref[...].astype(F32)) * u_s
    mix = jnp.dot(m.astype(BF16), wo_ref[...], preferred_element_type=F32)
    gate1 = ada_ref[:, 2 * D_MODEL:3 * D_MODEL]
    shift2 = ada_ref[:, 3 * D_MODEL:4 * D_MODEL]
    scale2 = ada_ref[:, 4 * D_MODEL:5 * D_MODEL]
    x1 = _ln(ALPHA * x + gate1 * mix) * g1_ref[...] + b1_ref[...]
    x1_ref[...] = x1
    h2 = _ln(x1) * (1.0 + scale2) + shift2
    h2b = h2.astype(BF16)
    h2_ref[...] = h2b
    logits = jnp.dot(h2b, wr_ref[...], preferred_element_type=F32) + br_ref[...]
    gates_ref[...] = _route(logits)


def _post(x2d, pos, og, yg, proj, ada3, w_gdn_out, w_ssd_out, w_o, ln_g, ln_b, w_router, b_router,
          seq_len, ada_row0):
    n_tok = x2d.shape[0]
    t = POST_BLOCK
    blocks_per_seq = seq_len // t
    has_pos = pos is not None

    def ada_map(i):
        return (ada_row0 + (i // blocks_per_seq if has_pos else 0), 0, 0)

    const = lambda i: (0, 0)
    in_specs = [pl.BlockSpec((t, D_MODEL), lambda i: (i, 0))]
    args = [x2d]
    if has_pos:
        in_specs.append(pl.BlockSpec((t, D_MODEL), lambda i: (i % blocks_per_seq, 0)))
        args.append(pos)
    in_specs += [
        pl.BlockSpec((t, GDN_HEADS * GDN_DV), lambda i: (i, 0)),
        pl.BlockSpec((t, SSD_INNER), lambda i: (i, 0)),
        pl.BlockSpec((t, D_MODEL), lambda i: (i, COL_GATE // D_MODEL)),
        pl.BlockSpec((t, D_MODEL), lambda i: (i, COL_GATE // D_MODEL + 1)),
        pl.BlockSpec((None, 1, 6 * D_MODEL), ada_map),
        pl.BlockSpec((GDN_HEADS * GDN_DV, D_MODEL), const, pipeline_mode=pl.Buffered(1)),
        pl.BlockSpec((SSD_INNER, D_MODEL), const, pipeline_mode=pl.Buffered(1)),
        pl.BlockSpec((D_MODEL, D_MODEL), const, pipeline_mode=pl.Buffered(1)),
        pl.BlockSpec((1, D_MODEL), const),
        pl.BlockSpec((1, D_MODEL), const),
        pl.BlockSpec((D_MODEL, LANES), const),
        pl.BlockSpec((1, LANES), const),
    ]
    args += [og, yg, proj, proj, ada3, w_gdn_out, w_ssd_out, w_o, ln_g, ln_b, w_router, b_router]
    return pl.pallas_call(
        functools.partial(_post_kernel, has_pos=has_pos),
        grid=(n_tok // t,),
        in_specs=in_specs,
        out_specs=[
            pl.BlockSpec((t, D_MODEL), lambda i: (i, 0)),
            pl.BlockSpec((t, D_MODEL), lambda i: (i, 0)),
            pl.BlockSpec((t, LANES), lambda i: (i, 0)),
        ],
        out_shape=[
            jax.ShapeDtypeStruct((n_tok, D_MODEL), F32),
            jax.ShapeDtypeStruct((n_tok, D_MODEL), BF16),
            jax.ShapeDtypeStruct((n_tok, LANES), F32),
        ],
        compiler_params=pltpu.CompilerParams(dimension_semantics=("arbitrary",), vmem_limit_bytes=VMEM_LIMIT),
        name="post",
    )(*args)


def _moe_kernel(h_ref, gates_ref, x1_ref, ada_ref, wg_ref, wu_ref, wd_ref, g2_ref, b2_ref, out_ref,
                col_ref, row_ref, gx_ref, cnt_ref):
    g = pl.program_id(1)
    t = MOE_WINDOW
    lane = lax.broadcasted_iota(jnp.int32, (t, LANES), 1)

    @pl.when(g == 0)
    def _():
        gates = gates_ref[...]
        grp = jnp.sum(jnp.where(lane == GROUP_LANE, gates, 0.0), axis=-1, keepdims=True)
        onehot = jnp.where((lane < N_GROUPS) & (grp == lane.astype(F32)), 1.0, 0.0)
        tri = (lax.broadcasted_iota(jnp.int32, (t, t), 0) >= lax.broadcasted_iota(jnp.int32, (t, t), 1))
        cum = jnp.dot(jnp.where(tri, 1.0, 0.0).astype(BF16), onehot.astype(BF16), preferred_element_type=F32)
        rank = jnp.sum(onehot * cum, axis=-1, keepdims=True) - 1.0
        info = jnp.where(lane == 0, grp, jnp.where(lane == 1, rank, 0.0))
        col_ref[...] = info
        row_ref[...] = info.T[0:SUBLANES, :]
        totals = cum[t - 1:t, :]
        for k in range(N_GROUPS):
            cnt_ref[k] = totals[0, k].astype(jnp.int32)
        hi = gates.astype(BF16).astype(F32)
        mid = (gates - hi).astype(BF16).astype(F32)
        low = gates - hi - mid
        packed = jnp.where(lane < N_EXPERTS, hi,
                           jnp.where(lane < 2 * N_EXPERTS, pltpu.roll(mid, N_EXPERTS, axis=1),
                                     jnp.where(lane < 3 * N_EXPERTS, pltpu.roll(low, 2 * N_EXPERTS, axis=1), 0.0)))
        gx_ref[...] = packed.astype(BF16)
        out_ref[...] = jnp.zeros(out_ref.shape, F32)

    n_g = cnt_ref[g]
    gf = g.astype(F32)
    def run_tile(base, rows):
        slot_r = (lax.broadcasted_iota(jnp.int32, (rows, t), 0) + base).astype(F32)
        pick = jnp.where((row_ref[1:2, :] == slot_r) & (row_ref[0:1, :] == gf), 1.0, 0.0).astype(BF16)
        hs = jnp.dot(pick, h_ref[...], preferred_element_type=F32).astype(BF16)
        gsx = jnp.dot(pick, gx_ref[...], preferred_element_type=F32)
        gs = gsx + pltpu.roll(gsx, LANES - N_EXPERTS, axis=1) + pltpu.roll(gsx, LANES - 2 * N_EXPERTS, axis=1)
        lane_t = lax.broadcasted_iota(jnp.int32, (rows, LANES), 1)
        hidden = []
        for e in range(EXPERTS_PER_GROUP):
            a = jnp.dot(hs, wg_ref[e], preferred_element_type=F32)
            b = jnp.dot(hs, wu_ref[e], preferred_element_type=F32)
            gate_e = jnp.sum(jnp.where(lane_t == g * EXPERTS_PER_GROUP + e, gs, 0.0), axis=-1, keepdims=True)
            hidden.append((_silu(a) * b * gate_e).astype(BF16))
        w_down = wd_ref[...].reshape(EXPERTS_PER_GROUP * D_EXPERT, D_MODEL)
        y = jnp.dot(jnp.concatenate(hidden, axis=1), w_down, preferred_element_type=F32)
        slot_c = (lax.broadcasted_iota(jnp.int32, (t, rows), 1) + base).astype(F32)
        place = jnp.where((col_ref[:, 1:2] == slot_c) & (col_ref[:, 0:1] == gf), 1.0, 0.0).astype(BF16)
        out_ref[...] += jnp.dot(place, y.astype(BF16), preferred_element_type=F32)

    half = MOE_TILE // 2
    n_full = (n_g + half - 1) // MOE_TILE

    def full_tile(j, carry):
        run_tile(j * MOE_TILE, MOE_TILE)
        return carry

    lax.fori_loop(0, n_full, full_tile, 0)

    @pl.when(n_g > n_full * MOE_TILE)
    def _():
        run_tile(n_full * MOE_TILE, half)

    @pl.when(g == N_GROUPS - 1)
    def _():
        gate2 = ada_ref[:, 5 * D_MODEL:6 * D_MODEL]
        out_ref[...] = _ln(ALPHA * x1_ref[...] + gate2 * out_ref[...]) * g2_ref[...] + b2_ref[...]


def _moe(h2, gates, x1, ada3, w_gate, w_up, w_down, ln_g, ln_b, seq_len, ada_row0, per_seq_ada):
    n_tok = h2.shape[0]
    t = MOE_WINDOW
    blocks_per_seq = seq_len // t
    assert not per_seq_ada or seq_len % t == 0

    def ada_map(i, g):
        return (ada_row0 + (i // blocks_per_seq if per_seq_ada else 0), 0, 0)

    return pl.pallas_call(
        _moe_kernel,
        grid=(n_tok // t, N_GROUPS),
        in_specs=[
            pl.BlockSpec((t, D_MODEL), lambda i, g: (i, 0)),
            pl.BlockSpec((t, LANES), lambda i, g: (i, 0)),
            pl.BlockSpec((t, D_MODEL), lambda i, g: (i, 0), pipeline_mode=pl.Buffered(1)),
            pl.BlockSpec((None, 1, 6 * D_MODEL), ada_map),
            pl.BlockSpec((EXPERTS_PER_GROUP, D_MODEL, D_EXPERT), lambda i, g: (g, 0, 0)),
            pl.BlockSpec((EXPERTS_PER_GROUP, D_MODEL, D_EXPERT), lambda i, g: (g, 0, 0)),
            pl.BlockSpec((EXPERTS_PER_GROUP, D_EXPERT, D_MODEL), lambda i, g: (g, 0, 0)),
            pl.BlockSpec((1, D_MODEL), lambda i, g: (0, 0)),
            pl.BlockSpec((1, D_MODEL), lambda i, g: (0, 0)),
        ],
        out_specs=pl.BlockSpec((t, D_MODEL), lambda i, g: (i, 0)),
        out_shape=jax.ShapeDtypeStruct((n_tok, D_MODEL), F32),
        scratch_shapes=[
            pltpu.VMEM((t, LANES), F32),
            pltpu.VMEM((SUBLANES, t), F32),
            pltpu.VMEM((t, LANES), BF16),
            pltpu.SMEM((N_GROUPS,), jnp.int32),
        ],
        compiler_params=pltpu.CompilerParams(dimension_semantics=("arbitrary", "arbitrary"),
                                             vmem_limit_bytes=VMEM_LIMIT),
        name="moe",
    )(h2, gates, x1, ada3, w_gate, w_up, w_down, ln_g, ln_b)


def _grid_pos_embed(n_tokens, d):
    rows = n_tokens // GRID_W
    rr, cc = np.meshgrid(np.arange(rows, dtype=np.float32), np.arange(GRID_W, dtype=np.float32), indexing="ij")
    quarter = d // 4
    freqs = np.exp(-math.log(POS_BASE) * np.arange(quarter, dtype=np.float32) / quarter).astype(np.float32)
    ang_r = rr.reshape(-1, 1) * freqs
    ang_c = cc.reshape(-1, 1) * freqs
    table = np.concatenate([np.sin(ang_r), np.cos(ang_r), np.sin(ang_c), np.cos(ang_c)], axis=-1)
    return jnp.asarray(table, dtype=F32)


def _lane_rep(v):
    return jnp.broadcast_to(v[..., None], v.shape + (LANES,)).astype(F32)


def _stream(x3d, pos, ada3, ada_row0, s_gdn0, h_ssd0t, want_state, wts):
    bsz, seq_len, _ = x3d.shape
    x2d = x3d.reshape(bsz * seq_len, D_MODEL)
    proj, small_t = _in_proj(x2d, pos, ada3, wts["w_big"], wts["w_small_t"], seq_len, ada_row0)
    gdn_out = _gdn(proj, small_t, wts["gdn_conv_w"], wts["gdn_conv_b"], wts["gdn_params"],
                   wts["gdn_norm_w"], s_gdn0, bsz, seq_len, want_state)
    ssd_out = _ssd(proj, small_t, wts["ssd_conv_w"], wts["ssd_conv_b"], wts["ssd_a_log"], wts["ssd_dt_bias"],
                   wts["ssd_d"], wts["ssd_norm_w"], h_ssd0t, bsz, seq_len, want_state)
    x1, h2, gates = _post(x2d, pos, gdn_out[0], ssd_out[0], proj, ada3, wts["w_gdn_out"], wts["w_ssd_out"],
                          wts["w_o"], wts["ln1_g"], wts["ln1_b"], wts["w_router"], wts["b_router"],
                          seq_len, ada_row0)
    y = _moe(h2, gates, x1, ada3, wts["w_exp_gate"], wts["w_exp_up"], wts["w_exp_down"], wts["ln2_g"],
             wts["ln2_b"], seq_len, ada_row0, pos is not None)
    states = (gdn_out[1], ssd_out[1]) if want_state else None
    return y.reshape(bsz, seq_len, D_MODEL), states


def kernel(x_prompt, x_sample, state_gdn, state_ssd, c, c_ctx, w_ada, b_ada, w_in, gdn_conv_w, gdn_conv_b, gdn_a_log, gdn_dt_bias, gdn_norm_w, w_gdn_out, ssd_conv_w, ssd_conv_b, ssd_a_log, ssd_dt_bias, ssd_d, ssd_norm_w, w_ssd_out, w_o, ln1_g, ln1_b, w_router_group, b_router_group, w_router_expert, b_router_expert, w_exp_gate, w_exp_up, w_exp_down, ln2_g, ln2_b):
    assert w_in.shape[0] == DEPTH == 1
    l = 0
    bsz_c = x_prompt.shape[0]
    bsz_l, seq_l, _ = x_sample.shape

    cvec = jnp.zeros((SUBLANES, D_MODEL), F32).at[0].set(c_ctx).at[1:1 + bsz_l].set(c)
    ada3 = _ada(cvec, w_ada[l], b_ada[l]).reshape(SUBLANES, 1, 6 * D_MODEL)

    wi = w_in[l]
    o_zg = 3072
    o_beta = 4096
    o_a = 4112
    o_xbc = 4128
    o_zs = 7200
    o_dt = 9248
    o_gate = 9312
    w_big = jnp.concatenate([wi[:, 0:o_zg], wi[:, o_zg:o_beta], wi[:, o_xbc:o_zs], wi[:, o_zs:o_dt],
                             wi[:, o_gate:]], axis=1).astype(BF16)
    w_beta = wi[:, o_beta:o_a].T.reshape(2, GDN_HEADS, D_MODEL)
    w_a = wi[:, o_a:o_xbc].T.reshape(2, GDN_HEADS, D_MODEL)
    w_head = jnp.concatenate([w_beta, w_a, jnp.zeros((SUBLANES - 4, GDN_HEADS, D_MODEL), F32)], axis=0)
    w_head = jnp.transpose(w_head, (1, 0, 2)).reshape(ROW_DT, D_MODEL)
    w_small_t = jnp.concatenate([w_head, wi[:, o_dt:o_gate].T], axis=0).astype(BF16)
    assert w_small_t.shape == (SMALL_ROWS, D_MODEL)
    gdn_params = jnp.concatenate([gdn_a_log[l], gdn_dt_bias[l], jnp.zeros((SUBLANES - 4, GDN_HEADS), F32)], axis=0)
    gdn_params = _lane_rep(gdn_params.T)

    w_router = jnp.zeros((D_MODEL, LANES), F32)
    w_router = w_router.at[:, 0:N_EXPERTS].set(w_router_expert[l]).at[:, N_EXPERTS:N_EXPERTS + N_GROUPS].set(
        w_router_group[l])
    b_router = jnp.zeros((1, LANES), F32)
    b_router = b_router.at[0, 0:N_EXPERTS].set(b_router_expert[l]).at[0, N_EXPERTS:N_EXPERTS + N_GROUPS].set(
        b_router_group[l])

    wts = {
        "w_big": w_big, "w_small_t": w_small_t,
        "gdn_conv_w": gdn_conv_w[l], "gdn_conv_b": gdn_conv_b[l].reshape(1, -1),
        "gdn_params": gdn_params,
        "gdn_norm_w": gdn_norm_w[l].reshape(1, -1),
        "ssd_conv_w": ssd_conv_w[l], "ssd_conv_b": ssd_conv_b[l].reshape(1, -1),
        "ssd_a_log": _lane_rep(ssd_a_log[l]), "ssd_dt_bias": _lane_rep(ssd_dt_bias[l]),
        "ssd_d": jnp.repeat(ssd_d[l], SSD_P).reshape(1, -1), "ssd_norm_w": ssd_norm_w[l].reshape(1, -1),
        "w_gdn_out": w_gdn_out[l].astype(BF16), "w_ssd_out": w_ssd_out[l].astype(BF16),
        "w_o": w_o[l].astype(BF16),
        "ln1_g": ln1_g[l].reshape(1, -1), "ln1_b": ln1_b[l].reshape(1, -1),
        "w_router": w_router.astype(BF16), "b_router": b_router,
        "w_exp_gate": w_exp_gate[l].astype(BF16), "w_exp_up": w_exp_up[l].astype(BF16),
        "w_exp_down": w_exp_down[l].astype(BF16),
        "ln2_g": ln2_g[l].reshape(1, -1), "ln2_b": ln2_b[l].reshape(1, -1),
    }

    pos = _grid_pos_embed(seq_l, D_MODEL)
    h0 = state_ssd[:, l].reshape(bsz_l, 2, SSD_GROUPS, GROUP_W, SSD_N)

    y_ctx, (s_gdn, h_ssd) = _stream(x_prompt, None, ada3, 0, None, None, True, wts)
    y_lat, _ = _stream(x_sample, pos, ada3, 1, state_gdn, h0, False, wts)

    new_ssd = h_ssd.reshape(bsz_c, 1, 2, SSD_HEADS, SSD_P, SSD_N)
    return (y_ctx, y_lat, s_gdn, new_ssd)
```
